```python
import jax, jax.numpy as jnp
from jax import lax
import numpy as np

D_MODEL = 2048
BATCH = 4
SEQ = 2048
DEPTH = 2
DEC_BATCH = 32
DEC_SEQ = 8
PAST_LEN = 8192
PAGE_SIZE = 128

N_MIXERS = 2
N_A_LAYERS = (DEPTH + 1) // 2
N_B_LAYERS = DEPTH // 2
A_HEADS = 16
A_DK = D_MODEL // A_HEADS
A_DV = D_MODEL // A_HEADS
A_QKV = 2 * A_HEADS * A_DK + A_HEADS * A_DV
A_IN = A_QKV + A_HEADS * A_DV + 2 * A_HEADS
CONV_W = 4
CHUNK_A = 64
DT_BIAS_MEAN = -3.0
B_HEADS = 16
B_HD = D_MODEL // B_HEADS
B_IN = 4 * D_MODEL + B_HEADS
Q_BLOCK = 128
FORGET_BIAS_MEAN = 3.0
D_FF = 4 * D_MODEL
EPS = 1e-6
POOL_NUM = 5
POOL_DEN = 4

kernel_name = 'hybrid_gdn_fox_decode_step'


def _rmsnorm(x, w):
    xf = x.astype(jnp.float32)
    y = xf * lax.rsqrt(jnp.mean(xf * xf, axis=-1, keepdims=True) + EPS)
    return (y * w.astype(jnp.float32)).astype(x.dtype)


def _l2norm(x):
    xf = x.astype(jnp.float32)
    return xf * lax.rsqrt(jnp.sum(xf * xf, axis=-1, keepdims=True) + EPS)


def _gated_delta(q, k, v, g, beta, s0):
    b, t, h, dk = q.shape
    dv = v.shape[-1]
    c = min(CHUNK_A, t)
    pad = (-t) % c
    if pad:
        pw = ((0, 0), (0, pad), (0, 0), (0, 0))
        q, k, v = jnp.pad(q, pw), jnp.pad(k, pw), jnp.pad(v, pw)
        g, beta = jnp.pad(g, pw[:3]), jnp.pad(beta, pw[:3])
    n = (t + pad) // c

    def chunks(a):
        return jnp.moveaxis(a.reshape((b, n, c, h) + a.shape[3:]), 3, 1)

    q, k, v, g, beta = chunks(q), chunks(k), chunks(v), chunks(g), chunks(beta)
    gc = jnp.cumsum(g, axis=-1)
    incl = jnp.tril(jnp.ones((c, c), bool))
    strict = jnp.tril(jnp.ones((c, c), bool), -1)
    decay = jnp.exp(jnp.where(incl, gc[..., :, None] - gc[..., None, :], -jnp.inf))
    kb = k * beta[..., None]
    lmat = jnp.where(strict, jnp.einsum('bhnid,bhnjd->bhnij', kb, k) * decay, 0.0)
    eye = jnp.eye(c, dtype=jnp.float32)
    tm = lax.linalg.triangular_solve(eye + lmat, jnp.broadcast_to(eye, lmat.shape),
                                     left_side=True, lower=True, unit_diagonal=True)
    u = jnp.einsum('bhnij,bhnjd->bhnid', tm, v * beta[..., None])
    w = jnp.einsum('bhnij,bhnjd->bhnid', tm, kb * jnp.exp(gc)[..., None])
    intra = jnp.where(incl, jnp.einsum('bhnid,bhnjd->bhnij', q, k) * decay, 0.0)

    def step(s, inp):
        qc, kc, uc, wc, gcc, ac = inp
        v_new = uc - jnp.einsum('bhid,bhde->bhie', wc, s)
        o = (jnp.einsum('bhid,bhde->bhie', qc * jnp.exp(gcc)[..., None], s)
             + jnp.einsum('bhij,bhje->bhie', ac, v_new))
        glast = gcc[..., -1]
        s = (s * jnp.exp(glast)[..., None, None]
             + jnp.einsum('bhid,bhie->bhde', kc * jnp.exp(glast[..., None] - gcc)[..., None], v_new))
        return s, o

    xs = (jnp.moveaxis(q, 2, 0), jnp.moveaxis(k, 2, 0), jnp.moveaxis(u, 2, 0),
          jnp.moveaxis(w, 2, 0), jnp.moveaxis(gc, 2, 0), jnp.moveaxis(intra, 2, 0))
    s_fin, o = lax.scan(step, s0, xs)
    o = jnp.transpose(o, (1, 0, 3, 2, 4)).reshape(b, n * c, h, dv)[:, :t]
    return o, s_fin


def _mixer_a(h, conv_buf, s0, w_in, conv_w, a_log, dt_bias, norm_w, w_out):
    b, t, _ = h.shape
    proj = h @ w_in
    zo = A_QKV + A_HEADS * A_DV
    qkv = proj[..., :A_QKV]
    z = proj[..., A_QKV:zo]
    beta_logit = proj[..., zo:zo + A_HEADS]
    a_in = proj[..., zo + A_HEADS:]
    xc = jnp.concatenate([conv_buf.astype(qkv.dtype), qkv], axis=1)
    conv = sum(xc[:, i:i + t] * conv_w[i] for i in range(CONV_W))
    new_buf = xc[:, t:]
    act = jax.nn.silu(conv)
    nq = A_HEADS * A_DK
    q = act[..., :nq].reshape(b, t, A_HEADS, A_DK)
    k = act[..., nq:2 * nq].reshape(b, t, A_HEADS, A_DK)
    v = act[..., 2 * nq:].reshape(b, t, A_HEADS, A_DV).astype(jnp.float32)
    q = _l2norm(q) * (A_DK ** -0.5)
    k = _l2norm(k)
    beta = jax.nn.sigmoid(beta_logit.astype(jnp.float32))
    g = -jnp.exp(a_log.astype(jnp.float32)) * jax.nn.softplus(a_in.astype(jnp.float32) + dt_bias.astype(jnp.float32))
    o, s_fin = _gated_delta(q, k, v, g, beta, s0.astype(jnp.float32))
    o = _rmsnorm(o, norm_w) * jax.nn.silu(z.reshape(b, t, A_HEADS, A_DV).astype(jnp.float32))
    y = o.reshape(b, t, D_MODEL).astype(h.dtype) @ w_out
    return y, new_buf, s_fin.astype(h.dtype)


def _fox_project(h, w_in, f_bias, q_norm, k_norm):
    b, t, _ = h.shape
    proj = h @ w_in
    q = _rmsnorm(proj[..., :D_MODEL].reshape(b, t, B_HEADS, B_HD), q_norm)
    k = _rmsnorm(proj[..., D_MODEL:2 * D_MODEL].reshape(b, t, B_HEADS, B_HD), k_norm)
    v = proj[..., 2 * D_MODEL:3 * D_MODEL].reshape(b, t, B_HEADS, B_HD)
    gate = proj[..., 3 * D_MODEL:4 * D_MODEL]
    logf = jax.nn.log_sigmoid(proj[..., 4 * D_MODEL:].astype(jnp.float32) + f_bias.astype(jnp.float32))
    return q, k, v, gate, logf


def _fox_attend(q, k_all, v_all, c_q, c_k, q_offset):
    b, t, h, hd = q.shape
    l = k_all.shape[1]
    qb = Q_BLOCK if t % Q_BLOCK == 0 else t
    nb = t // qb
    kpos = jnp.arange(l)
    ck = jnp.transpose(c_k, (0, 2, 1))

    def block(i):
        start = i * qb
        qi = lax.dynamic_slice_in_dim(q, start, qb, axis=1)
        ci = jnp.transpose(lax.dynamic_slice_in_dim(c_q, start, qb, axis=1), (0, 2, 1))
        s = jnp.einsum('bqhd,bkhd->bhqk', qi, k_all, preferred_element_type=jnp.float32) * (hd ** -0.5)
        s = s + ci[..., :, None] - ck[:, :, None, :]
        qpos = q_offset + start + jnp.arange(qb)
        s = jnp.where(kpos[None, :] <= qpos[:, None], s, -jnp.inf)
        p = jax.nn.softmax(s, axis=-1)
        return jnp.einsum('bhqk,bkhd->bqhd', p.astype(v_all.dtype), v_all)

    o = lax.map(block, jnp.arange(nb))
    return jnp.moveaxis(o, 0, 1).reshape(b, t, h, hd)


def _fox_out(o, gate, w_out):
    b, t = o.shape[:2]
    return (o.reshape(b, t, D_MODEL) * jax.nn.sigmoid(gate)) @ w_out


def _mlp(h, w_up, w_down):
    a = jax.nn.relu(h @ w_up)
    return (a * a) @ w_down


def setup_inputs(seed: int = 0) -> dict:
    key = jax.random.key(seed)
    ks = jax.random.split(key, 24)
    f32 = jnp.float32
    n_pages = PAST_LEN // PAGE_SIZE
    n_pool = (DEC_BATCH * n_pages * POOL_NUM) // POOL_DEN

    def nrm(k, shape, scale):
        return jax.random.normal(k, shape, f32) * scale

    x_prompt = nrm(ks[0], (BATCH, SEQ, D_MODEL), 1.0)
    x_sample = nrm(ks[1], (DEC_BATCH, DEC_SEQ, D_MODEL), 1.0)
    state_delta = nrm(ks[2], (N_A_LAYERS, DEC_BATCH, A_HEADS, A_DK, A_DV), 0.5)
    state_conv = nrm(ks[3], (N_A_LAYERS, DEC_BATCH, CONV_W - 1, A_QKV), 1.0)
    cache_k = nrm(ks[4], (N_B_LAYERS, n_pool, PAGE_SIZE, B_HEADS, B_HD), 1.0)
    cache_v = nrm(ks[5], (N_B_LAYERS, n_pool, PAGE_SIZE, B_HEADS, B_HD), 1.0)
    cache_logf = jax.nn.log_sigmoid(FORGET_BIAS_MEAN + nrm(ks[6], (N_B_LAYERS, n_pool, PAGE_SIZE, B_HEADS), 1.0))
    page_table = jax.random.permutation(ks[7], n_pool)[:DEC_BATCH * n_pages].reshape(DEC_BATCH, n_pages).astype(jnp.int32)
    norm_mix = 1.0 + nrm(ks[8], (DEPTH, D_MODEL), 0.02)
    norm_mlp = 1.0 + nrm(ks[9], (DEPTH, D_MODEL), 0.02)
    a_w_in = nrm(ks[10], (N_A_LAYERS, D_MODEL, A_IN), D_MODEL ** -0.5)
    a_conv = nrm(ks[11], (N_A_LAYERS, CONV_W, A_QKV), CONV_W ** -0.5)
    a_A_log = jnp.log(jax.random.uniform(ks[12], (N_A_LAYERS, A_HEADS), f32, 1.0, 16.0))
    a_dt_bias = DT_BIAS_MEAN + nrm(ks[13], (N_A_LAYERS, A_HEADS), 0.3)
    a_out_norm = 1.0 + nrm(ks[14], (N_A_LAYERS, A_DV), 0.02)
    a_w_out = nrm(ks[15], (N_A_LAYERS, D_MODEL, D_MODEL), D_MODEL ** -0.5)
    b_w_in = nrm(ks[16], (N_B_LAYERS, D_MODEL, B_IN), D_MODEL ** -0.5)
    b_f_bias = FORGET_BIAS_MEAN + nrm(ks[17], (N_B_LAYERS, B_HEADS), 0.1)
    b_q_norm = 1.0 + nrm(ks[18], (N_B_LAYERS, B_HD), 0.02)
    b_k_norm = 1.0 + nrm(ks[19], (N_B_LAYERS, B_HD), 0.02)
    b_w_out = nrm(ks[20], (N_B_LAYERS, D_MODEL, D_MODEL), D_MODEL ** -0.5)
    w_up = nrm(ks[21], (DEPTH, D_MODEL, D_FF), D_MODEL ** -0.5)
    w_down = nrm(ks[22], (DEPTH, D_FF, D_MODEL), D_FF ** -0.5)
    return {'x_prompt': x_prompt, 'x_sample': x_sample, 'state_delta': state_delta, 'state_conv': state_conv,
            'cache_k': cache_k, 'cache_v': cache_v, 'cache_logf': cache_logf, 'page_table': page_table,
            'norm_mix': norm_mix, 'norm_mlp': norm_mlp, 'a_w_in': a_w_in, 'a_conv': a_conv, 'a_A_log': a_A_log,
            'a_dt_bias': a_dt_bias, 'a_out_norm': a_out_norm, 'a_w_out': a_w_out, 'b_w_in': b_w_in,
            'b_f_bias': b_f_bias, 'b_q_norm': b_q_norm, 'b_k_norm': b_k_norm, 'b_w_out': b_w_out,
            'w_up': w_up, 'w_down': w_down}


def reference(x_prompt, x_sample, state_delta, state_conv, cache_k, cache_v, cache_logf, page_table,
              norm_mix, norm_mlp, a_w_in, a_conv, a_A_log, a_dt_bias, a_out_norm, a_w_out,
              b_w_in, b_f_bias, b_q_norm, b_k_norm, b_w_out, w_up, w_down):
    xp, xs = x_prompt, x_sample
    bp, bs = xp.shape[0], xs.shape[0]
    past = page_table.shape[1] * cache_k.shape[2]
    p_delta, p_conv, s_delta, s_conv = [], [], [], []
    p_k, p_v, p_lf, s_k, s_v, s_lf = [], [], [], [], [], []
    for i in range(DEPTH):
        j = i // N_MIXERS
        hp = _rmsnorm(xp, norm_mix[i])
        hs = _rmsnorm(xs, norm_mix[i])
        if i % N_MIXERS == 0:
            wa = (a_w_in[j], a_conv[j], a_A_log[j], a_dt_bias[j], a_out_norm[j], a_w_out[j])
            yp, cbp, stp = _mixer_a(hp, jnp.zeros((bp, CONV_W - 1, A_QKV), hp.dtype),
                                    jnp.zeros((bp, A_HEADS, A_DK, A_DV), jnp.float32), *wa)
            ys, cbs, sts = _mixer_a(hs, state_conv[j], state_delta[j], *wa)
            p_conv.append(cbp); p_delta.append(stp)
            s_conv.append(cbs); s_delta.append(sts)
        else:
            wb = (b_w_in[j], b_f_bias[j], b_q_norm[j], b_k_norm[j])
            q, k, v, gate, lf = _fox_project(hp, *wb)
            c = jnp.cumsum(lf, axis=1)
            yp = _fox_out(_fox_attend(q, k, v, c, c, 0), gate, b_w_out[j])
            p_k.append(k); p_v.append(v); p_lf.append(lf.astype(cache_logf.dtype))
            q, k, v, gate, lf = _fox_project(hs, *wb)
            kp = cache_k[j, page_table].reshape(bs, past, B_HEADS, B_HD).astype(k.dtype)
            vp = cache_v[j, page_table].reshape(bs, past, B_HEADS, B_HD).astype(v.dtype)
            lp = cache_logf[j, page_table].reshape(bs, past, B_HEADS).astype(jnp.float32)
            c_all = jnp.cumsum(jnp.concatenate([lp, lf], axis=1), axis=1)
            o = _fox_attend(q, jnp.concatenate([kp, k], axis=1), jnp.concatenate([vp, v], axis=1),
                            c_all[:, past:], c_all, past)
            ys = _fox_out(o, gate, b_w_out[j])
            s_k.append(k); s_v.append(v); s_lf.append(lf.astype(cache_logf.dtype))
        xp = xp + yp
        xs = xs + ys
        xp = xp + _mlp(_rmsnorm(xp, norm_mlp[i]), w_up[i], w_down[i])
        xs = xs + _mlp(_rmsnorm(xs, norm_mlp[i]), w_up[i], w_down[i])
    return (xp, xs, jnp.stack(p_delta), jnp.stack(p_conv), jnp.stack(p_k), jnp.stack(p_v), jnp.stack(p_lf),
            jnp.stack(s_delta), jnp.stack(s_conv), jnp.stack(s_k), jnp.stack(s_v), jnp.stack(s_lf))
```

```python
import functools

import jax
import jax.numpy as jnp
from jax import lax
from jax.experimental import pallas as pl
from jax.experimental.pallas import tpu as pltpu

F32 = jnp.float32
BF16 = jnp.bfloat16
EPS = 1e-6
NEG = -1e30
LANES = 128
SUBLANES = 8
VMEM_LIMIT = 48 * 1024 * 1024


def _cparams(sem):
    return pltpu.CompilerParams(dimension_semantics=sem, vmem_limit_bytes=VMEM_LIMIT)


def _dot(a, b):
    return jnp.dot(a.astype(BF16), b.astype(BF16), preferred_element_type=F32)


def _dot_nt(a, b):
    return lax.dot_general(a.astype(BF16), b.astype(BF16), (((1,), (1,)), ((), ())),
                           preferred_element_type=F32)


def _split3(x):
    hi = x.astype(BF16)
    r = x - hi.astype(F32)
    mid = r.astype(BF16)
    lo = (r - mid.astype(F32)).astype(BF16)
    return hi, mid, lo


def _dot_x3(a, b):
    ah, am, _ = _split3(a)
    bh, bm, _ = _split3(b)
    d = functools.partial(jnp.dot, preferred_element_type=F32)
    return d(ah, bh) + (d(ah, bm) + d(am, bh))


def _dot_exact_rhs01(a, b01):
    hi, mid, lo = _split3(a)
    b = b01.astype(BF16)
    d = functools.partial(jnp.dot, preferred_element_type=F32)
    return d(hi, b) + (d(mid, b) + d(lo, b))


def _dot_exact_lhs01(a01, b):
    hi, mid, lo = _split3(b)
    a = a01.astype(BF16)
    d = functools.partial(jnp.dot, preferred_element_type=F32)
    return d(a, hi) + (d(a, mid) + d(a, lo))


def _sigmoid(x):
    return 1.0 / (1.0 + jnp.exp(-x))


def _softplus(x):
    return jnp.maximum(x, 0.0) + jnp.log1p(jnp.exp(-jnp.abs(x)))


def _rmsnorm_kernel(x_ref, w_ref, o_ref):
    x = x_ref[...]
    ms = jnp.mean(x * x, axis=-1, keepdims=True)
    o_ref[...] = (x * lax.rsqrt(ms + EPS) * w_ref[...]).astype(o_ref.dtype)


def rmsnorm_bf16(x, w, tm=256):
    m, d = x.shape
    assert m % tm == 0
    return pl.pallas_call(
        _rmsnorm_kernel,
        grid=(m // tm,),
        in_specs=[pl.BlockSpec((tm, d), lambda i: (i, 0)),
                  pl.BlockSpec((1, d), lambda i: (0, 0))],
        out_specs=pl.BlockSpec((tm, d), lambda i: (i, 0)),
        out_shape=jax.ShapeDtypeStruct((m, d), BF16),
        compiler_params=_cparams(("parallel",)),
        name="rmsnorm",
    )(x, w.reshape(1, d))


def _mm_kernel(*refs, epilogue, nk):
    if epilogue == "residual":
        a_ref, b_ref, r_ref, o_ref = refs[:4]
        rest = refs[4:]
    else:
        a_ref, b_ref, o_ref = refs[:3]
        r_ref = None
        rest = refs[3:]

    def finish(acc):
        if epilogue == "relu2":
            acc = jnp.maximum(acc, 0.0)
            acc = acc * acc
        elif epilogue == "residual":
            acc = acc + r_ref[...]
        o_ref[...] = acc.astype(o_ref.dtype)

    part = jnp.dot(a_ref[...], b_ref[...], preferred_element_type=F32)
    if nk == 1:
        finish(part)
        return
    acc_ref, = rest
    k = pl.program_id(2)

    @pl.when(k == 0)
    def _():
        acc_ref[...] = part

    @pl.when(k > 0)
    def _():
        acc_ref[...] += part

    @pl.when(k == nk - 1)
    def _():
        finish(acc_ref[...])


def matmul(a, b, *, epilogue="none", residual=None, out_dtype=F32, tm=768, tn=1024, tk=2048):
    m, kdim = a.shape
    _, n = b.shape
    tn = min(tn, n)
    tk = min(tk, kdim)
    assert m % tm == 0 and n % tn == 0 and kdim % tk == 0, (m, n, kdim)
    nk = kdim // tk
    in_specs = [pl.BlockSpec((tm, tk), lambda i, j, k: (i, k)),
                pl.BlockSpec((tk, tn), lambda i, j, k: (k, j))]
    args = [a, b]
    if epilogue == "residual":
        in_specs.append(pl.BlockSpec((tm, tn), lambda i, j, k: (i, j)))
        args.append(residual)
    scratch = [pltpu.VMEM((tm, tn), F32)] if nk > 1 else []
    return pl.pallas_call(
        functools.partial(_mm_kernel, epilogue=epilogue, nk=nk),
        grid=(m // tm, n // tn, nk),
        in_specs=in_specs,
        out_specs=pl.BlockSpec((tm, tn), lambda i, j, k: (i, j)),
        out_shape=jax.ShapeDtypeStruct((m, n), out_dtype),
        scratch_shapes=scratch,
        compiler_params=_cparams(("parallel", "parallel", "arbitrary")),
        name="matmul_" + epilogue,
    )(*args)


def _tri_inv(lmat, rblk, cblk, eye, c):
    base = SUBLANES
    same8 = (rblk(base) == cblk(base))
    ld = jnp.where(same8, lmat, 0.0)
    ld2 = _dot_x3(ld, ld)
    ld4 = _dot_x3(ld2, ld2)
    x = jnp.where(eye, 1.0, 0.0) - ld
    x = x + _dot_x3(x, ld2)
    x = x + _dot_x3(x, ld4)
    m = base
    while m < c:
        off = jnp.where((rblk(2 * m) == cblk(2 * m)) & (rblk(m) != cblk(m)), lmat, 0.0)
        x = x - _dot_x3(x, _dot_x3(off, x))
        m *= 2
    return x


def _gdn_kernel(*refs, C, G, H, DK, has_cbuf, has_s0):
    it = iter(refs)
    qkv_ref = next(it)
    z_ref = next(it)
    gates_ref = next(it)
    cw_ref = next(it)
    alog_ref = next(it)
    dtb_ref = next(it)
    onw_ref = next(it)
    cbuf_ref = next(it) if has_cbuf else None
    s0_ref = next(it) if has_s0 else None
    y_ref = next(it)
    sfin_ref = next(it)
    ext_ref = next(it)
    scat_ref = next(it)

    c_idx = pl.program_id(1)
    nc = pl.num_programs(1)
    R = G * C
    NQK = H * DK
    pad = SUBLANES

    @pl.when(c_idx == 0)
    def _init():
        ext_ref[0:pad, :] = jnp.zeros((pad, ext_ref.shape[1]), F32)
        if has_cbuf:
            ext_ref[pad - 3:pad, :] = cbuf_ref[0]
        for h in range(H):
            if has_s0:
                scat_ref[:, h * DK:(h + 1) * DK] = s0_ref[0, h]
            else:
                scat_ref[:, h * DK:(h + 1) * DK] = jnp.zeros((DK, DK), F32)

    ext_ref[pad:pad + C, :] = qkv_ref[...]

    def conv_act(col):
        sl = slice(col, col + DK)
        acc = ext_ref[pad:pad + C, sl] * cw_ref[3:4, sl]
        for i in range(3):
            acc = acc + ext_ref[pad - 3 + i:pad - 3 + i + C, sl] * cw_ref[i:i + 1, sl]
        return acc * _sigmoid(acc)

    gt = gates_ref[...]
    beta_full = _sigmoid(gt)
    g_full = -jnp.exp(alog_ref[...]) * _softplus(gt + dtb_ref[...])

    ri = lax.broadcasted_iota(jnp.int32, (R, R), 0)
    ci = lax.broadcasted_iota(jnp.int32, (R, R), 1)

    def rblk(m):
        return lax.shift_right_logical(ri, m.bit_length() - 1)

    def cblk(m):
        return lax.shift_right_logical(ci, m.bit_length() - 1)

    same = rblk(C) == cblk(C)
    eye = ri == ci
    incl = same & (ci <= ri)
    strict = same & (ci < ri)
    incl_t = same & (ri <= ci)
    rowblk = lax.broadcasted_iota(jnp.int32, (R, DK), 0) // C

    for gi in range(H // G):
        heads = list(range(gi * G, (gi + 1) * G))
        qs, ks, vs, bs, gs = [], [], [], [], []
        for h in heads:
            q = conv_act(h * DK)
            k = conv_act(NQK + h * DK)
            v = conv_act(2 * NQK + h * DK)
            q = q * lax.rsqrt(jnp.sum(q * q, axis=-1, keepdims=True) + EPS) * (DK ** -0.5)
            k = k * lax.rsqrt(jnp.sum(k * k, axis=-1, keepdims=True) + EPS)
            qs.append(q)
            ks.append(k)
            vs.append(v)
            bs.append(beta_full[:, h:h + 1])
            gs.append(g_full[:, H + h:H + h + 1])
        cat = (lambda xs: xs[0]) if G == 1 else (lambda xs: jnp.concatenate(xs, axis=0))
        Q, K, V = cat(qs), cat(ks), cat(vs)
        beta_c, g_c = cat(bs), cat(gs)

        g_r = jnp.sum(jnp.where(eye, g_c, 0.0), axis=0, keepdims=True)
        gc_c = jnp.sum(jnp.where(incl, g_r, 0.0), axis=1, keepdims=True)
        gc_r = jnp.sum(jnp.where(incl_t, g_c, 0.0), axis=0, keepdims=True)
        gl_c = jnp.sum(jnp.where(same, g_r, 0.0), axis=1, keepdims=True)
        decay = jnp.exp(jnp.where(incl, gc_c - gc_r, NEG))

        KB = K * beta_c
        lmat = jnp.where(strict, _dot_nt(KB, K) * decay, 0.0)
        amat = jnp.where(incl, _dot_nt(Q, K) * decay, 0.0)
        tmat = _tri_inv(lmat, rblk, cblk, eye, C)
        eg = jnp.exp(gc_c)
        UW = _dot(tmat, jnp.concatenate([V * beta_c, KB * eg], axis=1))
        U, W = UW[:, :DK], UW[:, DK:]
        QE = Q * eg
        KE = K * jnp.exp(gl_c - gc_c)

        lo, hi = heads[0] * DK, (heads[-1] + 1) * DK
        s_cat = scat_ref[:, lo:hi]
        WQS = _dot(jnp.concatenate([W, QE], axis=0), s_cat)
        vn, oi = [], []
        for j in range(G):
            rs = slice(j * C, (j + 1) * C)
            cs = slice(j * DK, (j + 1) * DK)
            vn.append(U[rs] - WQS[rs, cs])
            oi.append(WQS[R + j * C:R + (j + 1) * C, cs])
        Vn, Oi = cat(vn), cat(oi)
        O = Oi + _dot(amat, Vn)
        vwide = jnp.concatenate([jnp.where(rowblk == j, Vn, 0.0) for j in range(G)], axis=1)
        dS = _dot(KE.T, vwide)
        for j, h in enumerate(heads):
            cs = slice(j * DK, (j + 1) * DK)
            e_last = jnp.exp(gl_c[j * C:j * C + 1, :])
            scat_ref[:, h * DK:(h + 1) * DK] = s_cat[:, cs] * e_last + dS[:, cs]
            o = O[j * C:(j + 1) * C]
            o = o * lax.rsqrt(jnp.mean(o * o, axis=-1, keepdims=True) + EPS) * onw_ref[...]
            zg = z_ref[:, h * DK:(h + 1) * DK]
            y_ref[:, h * DK:(h + 1) * DK] = (o * (zg * _sigmoid(zg))).astype(y_ref.dtype)

    carry = ext_ref[C:C + pad, :]
    ext_ref[0:pad, :] = carry

    @pl.when(c_idx == nc - 1)
    def _fin():
        for h in range(H):
            sfin_ref[0, h] = scat_ref[:, h * DK:(h + 1) * DK]


def gdn_core(proj, gates, conv_w, alog_pad, dtb_pad, out_norm, conv_buf, s0, *,
             row0, nb, t, C, G, H, DK):
    nqkv = 3 * H * DK
    nc = t // C
    assert t % C == 0 and row0 % C == 0 and H % G == 0 and C & (C - 1) == 0
    rb0 = row0 // C
    row_map = lambda b, c: (rb0 + b * nc + c, 0)
    has_cbuf = conv_buf is not None
    has_s0 = s0 is not None
    in_specs = [
        pl.BlockSpec((C, nqkv), row_map),
        pl.BlockSpec((C, H * DK), lambda b, c: (rb0 + b * nc + c, 3)),
        pl.BlockSpec((C, LANES), row_map),
        pl.BlockSpec((4, nqkv), lambda b, c: (0, 0)),
        pl.BlockSpec((1, LANES), lambda b, c: (0, 0)),
        pl.BlockSpec((1, LANES), lambda b, c: (0, 0)),
        pl.BlockSpec((1, DK), lambda b, c: (0, 0)),
    ]
    args = [proj, proj, gates, conv_w, alog_pad, dtb_pad, out_norm.reshape(1, DK)]
    if has_cbuf:
        in_specs.append(pl.BlockSpec((1, 3, nqkv), lambda b, c: (b, 0, 0)))
        args.append(conv_buf)
    if has_s0:
        in_specs.append(pl.BlockSpec((1, H, DK, DK), lambda b, c: (b, 0, 0, 0)))
        args.append(s0)
    y, sfin = pl.pallas_call(
        functools.partial(_gdn_kernel, C=C, G=G, H=H, DK=DK, has_cbuf=has_cbuf, has_s0=has_s0),
        grid=(nb, nc),
        in_specs=in_specs,
        out_specs=[pl.BlockSpec((C, H * DK), lambda b, c: (b * nc + c, 0)),
                   pl.BlockSpec((1, H, DK, DK), lambda b, c: (b, 0, 0, 0))],
        out_shape=[jax.ShapeDtypeStruct((nb * t, H * DK), F32),
                   jax.ShapeDtypeStruct((nb, H, DK, DK), F32)],
        scratch_shapes=[pltpu.VMEM((C + SUBLANES, nqkv), F32),
                        pltpu.VMEM((DK, H * DK), F32)],
        compiler_params=_cparams(("parallel", "arbitrary")),
        name="gdn_core_c%d" % C,
    )(*args)
    return y, sfin


def _fox_prep_kernel(q_ref, k_ref, v_ref, f_ref, qn_ref, kn_ref, fb_ref,
                     qo_ref, ko_ref, kb_ref, vb_ref, lf_ref, *, H, HD):
    for h in range(H):
        sl = slice(h * HD, (h + 1) * HD)
        q = q_ref[:, sl]
        q = q * lax.rsqrt(jnp.mean(q * q, axis=-1, keepdims=True) + EPS) * qn_ref[...]
        qo_ref[:, sl] = (q * (HD ** -0.5)).astype(qo_ref.dtype)
        k = k_ref[:, sl]
        k = k * lax.rsqrt(jnp.mean(k * k, axis=-1, keepdims=True) + EPS) * kn_ref[...]
        ko_ref[:, sl] = k
        kb_ref[:, sl] = k.astype(kb_ref.dtype)
    vb_ref[...] = v_ref[...].astype(vb_ref.dtype)
    x = f_ref[...] + fb_ref[...]
    lf_ref[...] = jnp.minimum(x, 0.0) - jnp.log1p(jnp.exp(-jnp.abs(x)))


def fox_prep(proj, fproj, q_norm, k_norm, fb_pad, *, H, HD, tm=256):
    m = proj.shape[0]
    d = H * HD
    assert m % tm == 0
    col = lambda j: (lambda i: (i, j))
    return pl.pallas_call(
        functools.partial(_fox_prep_kernel, H=H, HD=HD),
        grid=(m // tm,),
        in_specs=[pl.BlockSpec((tm, d), col(0)), pl.BlockSpec((tm, d), col(1)),
                  pl.BlockSpec((tm, d), col(2)), pl.BlockSpec((tm, LANES), col(0)),
                  pl.BlockSpec((1, HD), lambda i: (0, 0)), pl.BlockSpec((1, HD), lambda i: (0, 0)),
                  pl.BlockSpec((1, LANES), lambda i: (0, 0))],
        out_specs=[pl.BlockSpec((tm, d), col(0)), pl.BlockSpec((tm, d), col(0)),
                   pl.BlockSpec((tm, d), col(0)), pl.BlockSpec((tm, d), col(0)),
                   pl.BlockSpec((tm, LANES), col(0))],
        out_shape=[jax.ShapeDtypeStruct((m, d), BF16), jax.ShapeDtypeStruct((m, d), F32),
                   jax.ShapeDtypeStruct((m, d), BF16), jax.ShapeDtypeStruct((m, d), BF16),
                   jax.ShapeDtypeStruct((m, LANES), F32)],
        compiler_params=_cparams(("parallel",)),
        name="fox_prep",
    )(proj, proj, proj, fproj, q_norm.reshape(1, HD), k_norm.reshape(1, HD), fb_pad)


def _cumsum_block(x, carry_ref, o_ref_setter):
    n = x.shape[0]
    ri = lax.broadcasted_iota(jnp.int32, (n, n), 0)
    ci = lax.broadcasted_iota(jnp.int32, (n, n), 1)
    tril = jnp.where(ci <= ri, 1.0, 0.0)
    cs = _dot_exact_lhs01(tril, x) + carry_ref[...]
    o_ref_setter(cs)
    carry_ref[...] = cs[n - 1:n, :]


def _cumsum_kernel(x_ref, o_ref, carry_ref):
    @pl.when(pl.program_id(1) == 0)
    def _():
        carry_ref[...] = jnp.zeros_like(carry_ref)

    def put(v):
        o_ref[0] = v

    _cumsum_block(x_ref[0], carry_ref, put)


def cumsum_rows(x, tb=256):
    b, t, w = x.shape
    assert t % tb == 0
    return pl.pallas_call(
        _cumsum_kernel,
        grid=(b, t // tb),
        in_specs=[pl.BlockSpec((1, tb, w), lambda i, j: (i, j, 0))],
        out_specs=pl.BlockSpec((1, tb, w), lambda i, j: (i, j, 0)),
        out_shape=jax.ShapeDtypeStruct((b, t, w), F32),
        scratch_shapes=[pltpu.VMEM((1, w), F32)],
        compiler_params=_cparams(("parallel", "arbitrary")),
        name="cumsum_rows",
    )(x)


def _paged_cumsum_kernel(pt_ref, lp_ref, ln_ref, o_ref, carry_ref, *, npages):
    p = pl.program_id(1)

    @pl.when(p == 0)
    def _():
        carry_ref[...] = jnp.zeros_like(carry_ref)

    def put(v):
        o_ref[0, 0] = v

    @pl.when(p < npages)
    def _():
        _cumsum_block(lp_ref[0], carry_ref, put)

    @pl.when(p == npages)
    def _():
        _cumsum_block(ln_ref[0], carry_ref, put)


def paged_cumsum(page_table_flat, cache_logf, lf_new_pad, *, layer_off, nb, npages, page, H):
    def pool_map(b, p, pt):
        return (layer_off + pt[b * npages + jnp.minimum(p, npages - 1)], 0, 0)

    return pl.pallas_call(
        functools.partial(_paged_cumsum_kernel, npages=npages),
        grid_spec=pltpu.PrefetchScalarGridSpec(
            num_scalar_prefetch=1,
            grid=(nb, npages + 1),
            in_specs=[pl.BlockSpec((1, page, H), pool_map),
                      pl.BlockSpec((1, page, H), lambda b, p, pt: (b, 0, 0))],
            out_specs=pl.BlockSpec((1, 1, page, H), lambda b, p, pt: (b, p, 0, 0)),
            scratch_shapes=[pltpu.VMEM((1, H), F32)],
        ),
        out_shape=jax.ShapeDtypeStruct((nb, npages + 1, page, H), F32),
        compiler_params=_cparams(("parallel", "arbitrary")),
        name="paged_cumsum",
    )(page_table_flat, cache_logf, lf_new_pad)


def _fox_prompt_kernel(q_ref, k_ref, v_ref, c_ref, g_ref, o_ref, *, tq):
    qi = pl.program_id(2)
    q = q_ref[...]
    hd = q.shape[1]
    rpos = lax.broadcasted_iota(jnp.int32, (tq, tq), 0)
    cpos = lax.broadcasted_iota(jnp.int32, (tq, tq), 1)

    def body(j, carry):
        m, l, acc = carry
        start = pl.multiple_of(j * tq, tq)
        k = k_ref[pl.ds(start, tq), :]
        v = v_ref[pl.ds(start, tq), :]
        s = _dot_nt(q, k) - c_ref[0, 0, :, pl.ds(start, tq)]
        s = jnp.where((j < qi) | (cpos <= rpos), s, NEG)
        m_new = jnp.maximum(m, jnp.max(s, axis=1, keepdims=True))
        alpha = jnp.exp(m - m_new)
        p = jnp.exp(s - m_new)
        l = alpha * l + jnp.sum(p, axis=1, keepdims=True)
        acc = alpha * acc + _dot(p, v)
        return m_new, l, acc

    init = (jnp.full((tq, 1), NEG, F32), jnp.zeros((tq, 1), F32), jnp.zeros((tq, hd), F32))
    m, l, acc = lax.fori_loop(0, qi + 1, body, init)
    g = g_ref[...]
    o_ref[...] = ((acc / l) * _sigmoid(g)).astype(o_ref.dtype)


def fox_prompt_attention(qb, kb, vb, c_row, proj, *, nb, t, H, HD, tq=256):
    nq = t // tq
    return pl.pallas_call(
        functools.partial(_fox_prompt_kernel, tq=tq),
        grid=(nb, H, nq),
        in_specs=[pl.BlockSpec((tq, HD), lambda b, h, i: (b * nq + i, h)),
                  pl.BlockSpec((t, HD), lambda b, h, i: (b, h)),
                  pl.BlockSpec((t, HD), lambda b, h, i: (b, h)),
                  pl.BlockSpec((1, 1, 1, t), lambda b, h, i: (b, h, 0, 0)),
                  pl.BlockSpec((tq, HD), lambda b, h, i: (b * nq + i, 3 * H + h))],
        out_specs=pl.BlockSpec((tq, HD), lambda b, h, i: (b * nq + i, h)),
        out_shape=jax.ShapeDtypeStruct((nb * t, H * HD), F32),
        compiler_params=_cparams(("parallel", "parallel", "arbitrary")),
        name="fox_prompt_attn",
    )(qb, kb, vb, c_row, proj)


def _fox_decode_kernel(pt_ref, q_ref, kp_ref, vp_ref, kn_ref, vn_ref, cum_ref, g_ref,
                       o_ref, qbd_ref, m_ref, l_ref, acc_ref, kf_ref, vf_ref, *,
                       npages, H, HD, NQ, page):
    p = pl.program_id(1)
    rows = H * NQ

    @pl.when(p == 0)
    def _():
        qt = jnp.concatenate([q_ref[0].astype(F32)] * H, axis=0)
        r = lax.broadcasted_iota(jnp.int32, (rows, H * HD), 0) // NQ
        c = lax.broadcasted_iota(jnp.int32, (rows, H * HD), 1) // HD
        qbd_ref[...] = jnp.where(r == c, qt, 0.0).astype(qbd_ref.dtype)
        m_ref[...] = jnp.full_like(m_ref, NEG)
        l_ref[...] = jnp.zeros_like(l_ref)
        acc_ref[...] = jnp.zeros_like(acc_ref)

    er = lax.broadcasted_iota(jnp.int32, (H, rows), 0)
    ec = lax.broadcasted_iota(jnp.int32, (H, rows), 1) // NQ
    expand = jnp.where(er == ec, 1.0, 0.0)

    def step(k, v, new):
        s = _dot_nt(qbd_ref[...], k)
        bias = _dot_exact_rhs01(cum_ref[0, 0], expand)
        s = s - bias.T
        if new:
            kpos = lax.broadcasted_iota(jnp.int32, (rows, page), 1)
            qpos = lax.broadcasted_iota(jnp.int32, (rows, page), 0) % NQ
            s = jnp.where(kpos <= qpos, s, NEG)
        m = m_ref[...]
        m_new = jnp.maximum(m, jnp.max(s, axis=1, keepdims=True))
        alpha = jnp.exp(m - m_new)
        pr = jnp.exp(s - m_new)
        l_ref[...] = alpha * l_ref[...] + jnp.sum(pr, axis=1, keepdims=True)
        acc_ref[...] = alpha * acc_ref[...] + _dot(pr, v)
        m_ref[...] = m_new

    @pl.when(p < npages)
    def _():
        for h in range(H):
            cs = slice(h * HD, (h + 1) * HD)
            kf_ref[:, cs] = kp_ref[0, pl.ds(h, page, stride=H), :].astype(kf_ref.dtype)
            vf_ref[:, cs] = vp_ref[0, pl.ds(h, page, stride=H), :].astype(vf_ref.dtype)
        step(kf_ref[...], vf_ref[...], False)

    @pl.when(p == npages)
    def _():
        step(kn_ref[0], vn_ref[0], True)
        inv = 1.0 / l_ref[...]
        for h in range(H):
            rs = slice(h * NQ, (h + 1) * NQ)
            cs = slice(h * HD, (h + 1) * HD)
            g = g_ref[:, cs]
            o_ref[:, cs] = (acc_ref[rs, cs] * inv[rs] * _sigmoid(g)).astype(o_ref.dtype)


def fox_decode_attention(page_table_flat, q_s, cache_k, cache_v, kn_pad, vn_pad, cum, gate_src, *,
                         layer_off, nb, npages, page, H, HD, NQ, gate_rb0):
    d = H * HD

    def pool_map(b, p, pt):
        return (layer_off + pt[b * npages + jnp.minimum(p, npages - 1)], 0, 0)

    return pl.pallas_call(
        functools.partial(_fox_decode_kernel, npages=npages, H=H, HD=HD, NQ=NQ, page=page),
        grid_spec=pltpu.PrefetchScalarGridSpec(
            num_scalar_prefetch=1,
            grid=(nb, npages + 1),
            in_specs=[pl.BlockSpec((1, NQ, d), lambda b, p, pt: (b, 0, 0)),
                      pl.BlockSpec((1, page * H, HD), pool_map),
                      pl.BlockSpec((1, page * H, HD), pool_map),
                      pl.BlockSpec((1, page, d), lambda b, p, pt: (b, 0, 0)),
                      pl.BlockSpec((1, page, d), lambda b, p, pt: (b, 0, 0)),
                      pl.BlockSpec((1, 1, page, H), lambda b, p, pt: (b, p, 0, 0)),
                      pl.BlockSpec((NQ, d), lambda b, p, pt: (gate_rb0 + b, 3))],
            out_specs=pl.BlockSpec((NQ, d), lambda b, p, pt: (b, 0)),
            scratch_shapes=[pltpu.VMEM((H * NQ, d), BF16),
                            pltpu.VMEM((H * NQ, 1), F32),
                            pltpu.VMEM((H * NQ, 1), F32),
                            pltpu.VMEM((H * NQ, d), F32),
                            pltpu.VMEM((page, d), BF16),
                            pltpu.VMEM((page, d), BF16)],
        ),
        out_shape=jax.ShapeDtypeStruct((nb * NQ, d), F32),
        compiler_params=_cparams(("parallel", "arbitrary")),
        name="fox_decode_attn",
    )(page_table_flat, q_s, cache_k, cache_v, kn_pad, vn_pad, cum, gate_src)


def _pad_lanes(v, start=0):
    out = jnp.zeros((1, LANES), F32)
    return lax.dynamic_update_slice(out, v.reshape(1, -1).astype(F32), (0, start))


def _mlp(x, norm_w, w_up, w_down):
    h = rmsnorm_bf16(x, norm_w)
    a = matmul(h, w_up.astype(BF16), epilogue="relu2", out_dtype=BF16)
    return matmul(a, w_down.astype(BF16), epilogue="residual", residual=x)


def kernel(x_prompt, x_sample, state_delta, state_conv, cache_k, cache_v, cache_logf, page_table,
           norm_mix, norm_mlp, a_w_in, a_conv, a_A_log, a_dt_bias, a_out_norm, a_w_out,
           b_w_in, b_f_bias, b_q_norm, b_k_norm, b_w_out, w_up, w_down):
    bp, t, d = x_prompt.shape
    bs, ts, _ = x_sample.shape
    depth = norm_mix.shape[0]
    n_mixers = 2
    H = state_delta.shape[2]
    DK = state_delta.shape[3]
    nqkv = state_conv.shape[-1]
    HB = cache_k.shape[3]
    HD = cache_k.shape[4]
    n_pool, page = cache_k.shape[1], cache_k.shape[2]
    npages = page_table.shape[1]
    mp, ms = bp * t, bs * ts
    assert H * DK == d and HB * HD == d and nqkv == 3 * d and DK == LANES and HD == LANES
    assert 2 * H <= LANES and HB <= LANES and ts <= page and ts >= 3 and t >= 3

    x = jnp.concatenate([x_prompt.reshape(mp, d), x_sample.reshape(ms, d)], axis=0)
    pt_flat = page_table.reshape(-1).astype(jnp.int32)

    p_delta, p_conv, s_delta, s_conv = [], [], [], []
    p_k, p_v, p_lf, s_k, s_v, s_lf = [], [], [], [], [], []
    for i in range(depth):
        j = i // n_mixers
        hn = rmsnorm_bf16(x, norm_mix[i])
        if i % n_mixers == 0:
            w_in = a_w_in[j]
            proj = matmul(hn, w_in[:, :nqkv + d].astype(BF16))
            wg = jnp.zeros((d, LANES), F32).at[:, :2 * H].set(w_in[:, nqkv + d:])
            gates = matmul(hn, wg.astype(BF16))
            alog_pad = _pad_lanes(a_A_log[j], H)
            dtb_pad = _pad_lanes(a_dt_bias[j], H)
            common = dict(H=H, DK=DK)
            cp = 64 if t % 64 == 0 else t
            yp, stp = gdn_core(proj, gates, a_conv[j], alog_pad, dtb_pad, a_out_norm[j], None, None,
                               row0=0, nb=bp, t=t, C=cp, G=max(1, min(H, 256 // cp)), **common)
            ys, sts = gdn_core(proj, gates, a_conv[j], alog_pad, dtb_pad, a_out_norm[j],
                               state_conv[j], state_delta[j],
                               row0=mp, nb=bs, t=ts, C=ts, G=H, **common)
            o = jnp.concatenate([yp, ys], axis=0).astype(BF16)
            x = matmul(o, a_w_out[j].astype(BF16), epilogue="residual", residual=x)
            p_conv.append(proj[:mp].reshape(bp, t, -1)[:, t - 3:, :nqkv])
            s_conv.append(proj[mp:].reshape(bs, ts, -1)[:, ts - 3:, :nqkv])
            p_delta.append(stp)
            s_delta.append(sts)
        else:
            w_in = b_w_in[j]
            proj = matmul(hn, w_in[:, :4 * d].astype(BF16))
            wf = jnp.zeros((d, LANES), F32).at[:, :HB].set(w_in[:, 4 * d:])
            fproj = matmul(hn, wf.astype(BF16))
            qb, kf, kb, vb, lf = fox_prep(proj, fproj, b_q_norm[j], b_k_norm[j],
                                          _pad_lanes(b_f_bias[j]), H=HB, HD=HD)
            c_p = cumsum_rows(lf[:mp].reshape(bp, t, LANES))
            c_row = jnp.transpose(c_p[:, :, :HB], (0, 2, 1)).reshape(bp, HB, 1, t)
            op = fox_prompt_attention(qb, kb, vb, c_row, proj, nb=bp, t=t, H=HB, HD=HD)
            padr = lambda a: jnp.pad(a.reshape(bs, ts, -1), ((0, 0), (0, page - ts), (0, 0)))
            cum = paged_cumsum(pt_flat, cache_logf.reshape(-1, page, HB), padr(lf[mp:, :HB]),
                               layer_off=j * n_pool, nb=bs, npages=npages, page=page, H=HB)
            os_ = fox_decode_attention(pt_flat, qb[mp:].reshape(bs, ts, d), cache_k.reshape(-1, page * HB, HD),
                                       cache_v.reshape(-1, page * HB, HD), padr(kb[mp:]), padr(vb[mp:]),
                                       cum, proj, layer_off=j * n_pool, nb=bs, npages=npages,
                                       page=page, H=HB, HD=HD, NQ=ts, gate_rb0=mp // ts)
            o = jnp.concatenate([op, os_], axis=0).astype(BF16)
            x = matmul(o, b_w_out[j].astype(BF16), epilogue="residual", residual=x)
            p_k.append(kf[:mp].reshape(bp, t, HB, HD))
            p_v.append(proj[:mp, 2 * d:3 * d].reshape(bp, t, HB, HD))
            p_lf.append(lf[:mp, :HB].reshape(bp, t, HB))
            s_k.append(kf[mp:].reshape(bs, ts, HB, HD))
            s_v.append(proj[mp:, 2 * d:3 * d].reshape(bs, ts, HB, HD))
            s_lf.append(lf[mp:, :HB].reshape(bs, ts, HB))
        x = _mlp(x, norm_mlp[i], w_up[i], w_down[i])
    return (x[:mp].reshape(bp, t, d), x[mp:].reshape(bs, ts, d),
            jnp.stack(p_delta), jnp.stack(p_conv), jnp.stack(p_k), jnp.stack(p_v), jnp.stack(p_lf),
            jnp.stack(s_delta), jnp.stack(s_conv), jnp.stack(s_k), jnp.stack(s_v), jnp.stack(s_lf))
```

```python
import functools

import jax
import jax.numpy as jnp
from jax import lax
from jax.experimental import pallas as pl
from jax.experimental.pallas import tpu as pltpu

F32 = jnp.float32
BF16 = jnp.bfloat16
EPS = 1e-6
NEG = -1e30
LANES = 128
SUBLANES = 8
MXU_DIM = 256
VMEM_LIMIT = 48 * 1024 * 1024

ROW_TILES = (768, 512, 528, 384, 256, 128, 64, 32, 16, 8)
COL_TILE = 1024
K_TILE = 2048
ATTN_TQ = (512, 256, 128)
DECODE_PAGES_PER_STEP = (4, 2, 1)
CUMSUM_PAGES_PER_STEP = (8, 4, 2, 1)


def _pick(n, cands):
    for c in cands:
        if n % c == 0:
            return c
    return n


def _cparams(sem):
    return pltpu.CompilerParams(dimension_semantics=sem, vmem_limit_bytes=VMEM_LIMIT)


def _dot(a, b):
    return jnp.dot(a.astype(BF16), b.astype(BF16), preferred_element_type=F32)


def _dot_nt(a, b):
    return lax.dot_general(a.astype(BF16), b.astype(BF16), (((1,), (1,)), ((), ())),
                           preferred_element_type=F32)


def _split3(x):
    hi = x.astype(BF16)
    r = x - hi.astype(F32)
    mid = r.astype(BF16)
    lo = (r - mid.astype(F32)).astype(BF16)
    return hi, mid, lo


def _dot_exact_lhs01(a01, b):
    hi, mid, lo = _split3(b)
    a = a01.astype(BF16)
    d = functools.partial(jnp.dot, preferred_element_type=F32)
    return d(a, hi) + (d(a, mid) + d(a, lo))


def _sigmoid(x):
    return 1.0 / (1.0 + jnp.exp(-x))


def _softplus(x):
    return jnp.maximum(x, 0.0) + jnp.log1p(jnp.exp(-jnp.abs(x)))


def _rmsnorm_kernel(x_ref, w_ref, o_ref):
    x = x_ref[...]
    ms = jnp.mean(x * x, axis=-1, keepdims=True)
    o_ref[...] = (x * lax.rsqrt(ms + EPS) * w_ref[...]).astype(o_ref.dtype)


def rmsnorm_bf16(x, w):
    m, d = x.shape
    tm = _pick(m, (256, 128, 64, 32, 16))
    return pl.pallas_call(
        _rmsnorm_kernel,
        grid=(m // tm,),
        in_specs=[pl.BlockSpec((tm, d), lambda i: (i, 0)),
                  pl.BlockSpec((1, d), lambda i: (0, 0))],
        out_specs=pl.BlockSpec((tm, d), lambda i: (i, 0)),
        out_shape=jax.ShapeDtypeStruct((m, d), BF16),
        compiler_params=_cparams(("parallel",)),
        name="rmsnorm",
    )(x, w.reshape(1, d))


def _epilogue(acc, kind, r_ref):
    if kind == "relu2":
        acc = jnp.maximum(acc, 0.0)
        acc = acc * acc
    elif kind == "residual":
        acc = acc + r_ref[...]
    return acc


def _mm_wcast_kernel(*refs, epilogue):
    if epilogue == "residual":
        a_ref, w_ref, r_ref, o_ref, wb_ref = refs
    else:
        a_ref, w_ref, o_ref, wb_ref = refs
        r_ref = None

    @pl.when(pl.program_id(1) == 0)
    def _():
        wb_ref[...] = w_ref[...].astype(wb_ref.dtype)

    acc = jnp.dot(a_ref[...], wb_ref[...], preferred_element_type=F32)
    o_ref[...] = _epilogue(acc, epilogue, r_ref).astype(o_ref.dtype)


def matmul_wcast(a, w, *, n_out, epilogue="none", residual=None, out_dtype=F32, row0=0):
    ma, kdim = a.shape
    tm = _pick(ma, ROW_TILES if epilogue != "residual" else (512, 256, 128, 64, 32, 16, 8))
    tn = min(COL_TILE, n_out)
    assert n_out % tn == 0 and row0 % tm == 0 and w.shape[0] == kdim
    r0 = row0 // tm
    in_specs = [pl.BlockSpec((tm, kdim), lambda j, i: (i, 0)),
                pl.BlockSpec((kdim, tn), lambda j, i: (0, j))]
    args = [a, w]
    aliases = {}
    if epilogue == "residual":
        in_specs.append(pl.BlockSpec((tm, tn), lambda j, i: (r0 + i, j)))
        args.append(residual)
        aliases = {2: 0}
        out_rows = residual.shape[0]
    else:
        out_rows = ma
    return pl.pallas_call(
        functools.partial(_mm_wcast_kernel, epilogue=epilogue),
        grid=(n_out // tn, ma // tm),
        in_specs=in_specs,
        out_specs=pl.BlockSpec((tm, tn), lambda j, i: (r0 + i, j)),
        out_shape=jax.ShapeDtypeStruct((out_rows, n_out), out_dtype),
        scratch_shapes=[pltpu.VMEM((kdim, tn), BF16)],
        input_output_aliases=aliases,
        compiler_params=_cparams(("arbitrary", "arbitrary")),
        name="matmul_wcast_" + epilogue,
    )(*args)


def _mm_kernel(*refs, epilogue, nk):
    if epilogue == "residual":
        a_ref, b_ref, r_ref, o_ref, acc_ref = refs
    else:
        a_ref, b_ref, o_ref, acc_ref = refs
        r_ref = None
    part = jnp.dot(a_ref[...], b_ref[...], preferred_element_type=F32)
    k = pl.program_id(2)

    @pl.when(k == 0)
    def _():
        acc_ref[...] = part

    @pl.when(k > 0)
    def _():
        acc_ref[...] += part

    @pl.when(k == nk - 1)
    def _():
        o_ref[...] = _epilogue(acc_ref[...], epilogue, r_ref).astype(o_ref.dtype)


def matmul_ktiled(a, b, *, epilogue="none", residual=None, out_dtype=F32):
    m, kdim = a.shape
    _, n = b.shape
    tm, tn, tk = _pick(m, ROW_TILES), min(COL_TILE, n), min(K_TILE, kdim)
    assert n % tn == 0 and kdim % tk == 0
    nk = kdim // tk
    in_specs = [pl.BlockSpec((tm, tk), lambda i, j, k: (i, k)),
                pl.BlockSpec((tk, tn), lambda i, j, k: (k, j))]
    args = [a, b]
    if epilogue == "residual":
        in_specs.append(pl.BlockSpec((tm, tn), lambda i, j, k: (i, j)))
        args.append(residual)
    return pl.pallas_call(
        functools.partial(_mm_kernel, epilogue=epilogue, nk=nk),
        grid=(m // tm, n // tn, nk),
        in_specs=in_specs,
        out_specs=pl.BlockSpec((tm, tn), lambda i, j, k: (i, j)),
        out_shape=jax.ShapeDtypeStruct((m, n), out_dtype),
        scratch_shapes=[pltpu.VMEM((tm, tn), F32)],
        compiler_params=_cparams(("parallel", "parallel", "arbitrary")),
        name="matmul_ktiled_" + epilogue,
    )(*args)


def _tri_inv_all(lmats, blk, eye, c):
    base = SUBLANES
    same8 = blk(0, base) == blk(1, base)
    lds = [jnp.where(same8, lm, 0.0) for lm in lmats]
    ld2 = [_dot(ld, ld) for ld in lds]
    ld4 = [_dot(a, a) for a in ld2]
    xs = [jnp.where(eye, 1.0, 0.0) - ld for ld in lds]
    xs = [x + _dot(x, a) for x, a in zip(xs, ld2)]
    xs = [x + _dot(x, a) for x, a in zip(xs, ld4)]
    m = base
    while m < c:
        sel = (blk(0, 2 * m) == blk(1, 2 * m)) & (blk(0, m) != blk(1, m))
        offx = [_dot(jnp.where(sel, lm, 0.0), x) for lm, x in zip(lmats, xs)]
        xs = [x - _dot(x, ox) for x, ox in zip(xs, offx)]
        m *= 2
    return xs


def _gdn_kernel(*refs, C, G, H, DK, has_cbuf, has_s0):
    it = iter(refs)
    qkv_ref = next(it)
    z_ref = next(it)
    gates_ref = next(it)
    cw_ref = next(it)
    alog_ref = next(it)
    dtb_ref = next(it)
    onw_ref = next(it)
    cbuf_ref = next(it) if has_cbuf else None
    s0_ref = next(it) if has_s0 else None
    y_ref = next(it)
    sfin_ref = next(it)
    ext_ref = next(it)
    scat_ref = next(it)

    c_idx = pl.program_id(1)
    nc = pl.num_programs(1)
    R = G * C
    NG = H // G
    NQK = H * DK
    pad = SUBLANES

    @pl.when(c_idx == 0)
    def _init():
        ext_ref[0:pad, :] = jnp.zeros((pad, ext_ref.shape[1]), F32)
        if has_cbuf:
            ext_ref[pad - 3:pad, :] = cbuf_ref[0]
        for h in range(H):
            if has_s0:
                scat_ref[:, h * DK:(h + 1) * DK] = s0_ref[0, h]
            else:
                scat_ref[:, h * DK:(h + 1) * DK] = jnp.zeros((DK, DK), F32)

    ext_ref[pad:pad + C, :] = qkv_ref[...]

    def conv_act(col):
        sl = slice(col, col + DK)
        acc = ext_ref[pad:pad + C, sl] * cw_ref[3:4, sl]
        for i in range(3):
            acc = acc + ext_ref[pad - 3 + i:pad - 3 + i + C, sl] * cw_ref[i:i + 1, sl]
        return acc * _sigmoid(acc)

    gt = gates_ref[...]
    beta_full = _sigmoid(gt)
    g_full = -jnp.exp(alog_ref[...]) * _softplus(gt + dtb_ref[...])

    iotas = (lax.broadcasted_iota(jnp.int32, (R, R), 0), lax.broadcasted_iota(jnp.int32, (R, R), 1))

    def blk(axis, m):
        return lax.shift_right_logical(iotas[axis], m.bit_length() - 1)

    same = blk(0, C) == blk(1, C)
    eye = iotas[0] == iotas[1]
    incl = same & (iotas[1] <= iotas[0])
    strict = same & (iotas[1] < iotas[0])
    incl_t = same & (iotas[0] <= iotas[1])
    rowblk = lax.shift_right_logical(lax.broadcasted_iota(jnp.int32, (R, DK), 0), C.bit_length() - 1)
    cat = (lambda xs: xs[0]) if G == 1 else (lambda xs: jnp.concatenate(xs, axis=0))
    groups = [list(range(gi * G, (gi + 1) * G)) for gi in range(NG)]

    Qs, Ks, Vs, betas, gcols = [], [], [], [], []
    for heads in groups:
        qs, ks, vs = [], [], []
        for h in heads:
            q = conv_act(h * DK)
            k = conv_act(NQK + h * DK)
            qs.append(q * lax.rsqrt(jnp.sum(q * q, axis=-1, keepdims=True) + EPS) * (DK ** -0.5))
            ks.append(k * lax.rsqrt(jnp.sum(k * k, axis=-1, keepdims=True) + EPS))
            vs.append(conv_act(2 * NQK + h * DK))
        Qs.append(cat(qs))
        Ks.append(cat(ks))
        Vs.append(cat(vs))
        betas.append(cat([beta_full[:, h:h + 1] for h in heads]))
        gcols.append(cat([g_full[:, H + h:H + h + 1] for h in heads]))

    gcs, gls, decays = [], [], []
    for g_c in gcols:
        g_r = jnp.sum(jnp.where(eye, g_c, 0.0), axis=0, keepdims=True)
        gc_c = jnp.sum(jnp.where(incl, g_r, 0.0), axis=1, keepdims=True)
        gc_r = jnp.sum(jnp.where(incl_t, g_c, 0.0), axis=0, keepdims=True)
        gls.append(jnp.sum(jnp.where(same, g_r, 0.0), axis=1, keepdims=True))
        gcs.append(gc_c)
        decays.append(jnp.exp(jnp.where(incl, gc_c - gc_r, NEG)))

    KBs = [k * b for k, b in zip(Ks, betas)]
    lmats = [jnp.where(strict, _dot_nt(kb, k) * d, 0.0) for kb, k, d in zip(KBs, Ks, decays)]
    amats = [jnp.where(incl, _dot_nt(q, k) * d, 0.0) for q, k, d in zip(Qs, Ks, decays)]
    tmats = _tri_inv_all(lmats, blk, eye, C)
    egs = [jnp.exp(gc) for gc in gcs]
    UWs = [_dot(t, jnp.concatenate([v * b, kb * eg], axis=1))
           for t, v, b, kb, eg in zip(tmats, Vs, betas, KBs, egs)]
    scats = [scat_ref[:, heads[0] * DK:(heads[-1] + 1) * DK] for heads in groups]
    WQSs = [_dot(jnp.concatenate([uw[:, DK:], q * eg], axis=0), sc)
            for uw, q, eg, sc in zip(UWs, Qs, egs, scats)]
    Vns, Ois = [], []
    for uw, wqs in zip(UWs, WQSs):
        vn, oi = [], []
        for j in range(G):
            rs = slice(j * C, (j + 1) * C)
            cs = slice(j * DK, (j + 1) * DK)
            vn.append(uw[rs, :DK] - wqs[rs, cs])
            oi.append(wqs[R + j * C:R + (j + 1) * C, cs])
        Vns.append(cat(vn))
        Ois.append(cat(oi))
    Os = [oi + _dot(a, vn) for oi, a, vn in zip(Ois, amats, Vns)]
    dSs = []
    for k, gl, gc, vn in zip(Ks, gls, gcs, Vns):
        ke = k * jnp.exp(gl - gc)
        vwide = jnp.concatenate([jnp.where(rowblk == j, vn, 0.0) for j in range(G)], axis=1)
        dSs.append(_dot(ke.T, vwide))
    for heads, sc, ds, gl, o_all in zip(groups, scats, dSs, gls, Os):
        for j, h in enumerate(heads):
            cs = slice(j * DK, (j + 1) * DK)
            e_last = jnp.exp(gl[j * C:j * C + 1, :])
            scat_ref[:, h * DK:(h + 1) * DK] = sc[:, cs] * e_last + ds[:, cs]
            o = o_all[j * C:(j + 1) * C]
            o = o * lax.rsqrt(jnp.mean(o * o, axis=-1, keepdims=True) + EPS) * onw_ref[...]
            zg = z_ref[:, h * DK:(h + 1) * DK]
            y_ref[:, h * DK:(h + 1) * DK] = (o * (zg * _sigmoid(zg))).astype(y_ref.dtype)

    carry = ext_ref[C:C + pad, :]
    ext_ref[0:pad, :] = carry

    @pl.when(c_idx == nc - 1)
    def _fin():
        for h in range(H):
            sfin_ref[0, h] = scat_ref[:, h * DK:(h + 1) * DK]


def gdn_core(proj, gates, conv_w, alog_pad, dtb_pad, out_norm, conv_buf, s0, *,
             row0, nb, t, C, G, H, DK, out_dtype):
    nqkv = 3 * H * DK
    nc = t // C
    assert t % C == 0 and row0 % C == 0 and H % G == 0 and C & (C - 1) == 0
    rb0 = row0 // C
    row_map = lambda b, c: (rb0 + b * nc + c, 0)
    has_cbuf = conv_buf is not None
    has_s0 = s0 is not None
    in_specs = [
        pl.BlockSpec((C, nqkv), row_map),
        pl.BlockSpec((C, H * DK), lambda b, c: (rb0 + b * nc + c, 3)),
        pl.BlockSpec((C, LANES), row_map),
        pl.BlockSpec((4, nqkv), lambda b, c: (0, 0)),
        pl.BlockSpec((1, LANES), lambda b, c: (0, 0)),
        pl.BlockSpec((1, LANES), lambda b, c: (0, 0)),
        pl.BlockSpec((1, DK), lambda b, c: (0, 0)),
    ]
    args = [proj, proj, gates, conv_w, alog_pad, dtb_pad, out_norm.reshape(1, DK)]
    if has_cbuf:
        in_specs.append(pl.BlockSpec((1, 3, nqkv), lambda b, c: (b, 0, 0)))
        args.append(conv_buf)
    if has_s0:
        in_specs.append(pl.BlockSpec((1, H, DK, DK), lambda b, c: (b, 0, 0, 0)))
        args.append(s0)
    y, sfin = pl.pallas_call(
        functools.partial(_gdn_kernel, C=C, G=G, H=H, DK=DK, has_cbuf=has_cbuf, has_s0=has_s0),
        grid=(nb, nc),
        in_specs=in_specs,
        out_specs=[pl.BlockSpec((C, H * DK), lambda b, c: (b * nc + c, 0)),
                   pl.BlockSpec((1, H, DK, DK), lambda b, c: (b, 0, 0, 0))],
        out_shape=[jax.ShapeDtypeStruct((nb * t, H * DK), out_dtype),
                   jax.ShapeDtypeStruct((nb, H, DK, DK), F32)],
        scratch_shapes=[pltpu.VMEM((C + SUBLANES, nqkv), F32),
                        pltpu.VMEM((DK, H * DK), F32)],
        compiler_params=_cparams(("parallel", "arbitrary")),
        name="gdn_core_c%d" % C,
    )(*args)
    return y, sfin


def _fox_prep_kernel(q_ref, k_ref, v_ref, f_ref, qn_ref, kn_ref, fb_ref,
                     qo_ref, ko_ref, kb_ref, vb_ref, lf_ref, *, H, HD):
    for h in range(H):
        sl = slice(h * HD, (h + 1) * HD)
        q = q_ref[:, sl]
        q = q * lax.rsqrt(jnp.mean(q * q, axis=-1, keepdims=True) + EPS) * qn_ref[...]
        qo_ref[:, sl] = (q * (HD ** -0.5)).astype(qo_ref.dtype)
        k = k_ref[:, sl]
        k = k * lax.rsqrt(jnp.mean(k * k, axis=-1, keepdims=True) + EPS) * kn_ref[...]
        ko_ref[:, sl] = k
        kb_ref[:, sl] = k.astype(kb_ref.dtype)
    vb_ref[...] = v_ref[...].astype(vb_ref.dtype)
    x = f_ref[...] + fb_ref[...]
    lf_ref[...] = jnp.minimum(x, 0.0) - jnp.log1p(jnp.exp(-jnp.abs(x)))


def fox_prep(proj, fproj, q_norm, k_norm, fb_pad, *, H, HD):
    m = proj.shape[0]
    d = H * HD
    tm = _pick(m, (256, 128, 64, 32, 16))
    col = lambda j: (lambda i: (i, j))
    return pl.pallas_call(
        functools.partial(_fox_prep_kernel, H=H, HD=HD),
        grid=(m // tm,),
        in_specs=[pl.BlockSpec((tm, d), col(0)), pl.BlockSpec((tm, d), col(1)),
                  pl.BlockSpec((tm, d), col(2)), pl.BlockSpec((tm, LANES), col(0)),
                  pl.BlockSpec((1, HD), lambda i: (0, 0)), pl.BlockSpec((1, HD), lambda i: (0, 0)),
                  pl.BlockSpec((1, LANES), lambda i: (0, 0))],
        out_specs=[pl.BlockSpec((tm, d), col(0)), pl.BlockSpec((tm, d), col(0)),
                   pl.BlockSpec((tm, d), col(0)), pl.BlockSpec((tm, d), col(0)),
                   pl.BlockSpec((tm, LANES), col(0))],
        out_shape=[jax.ShapeDtypeStruct((m, d), BF16), jax.ShapeDtypeStruct((m, d), F32),
                   jax.ShapeDtypeStruct((m, d), BF16), jax.ShapeDtypeStruct((m, d), BF16),
                   jax.ShapeDtypeStruct((m, LANES), F32)],
        compiler_params=_cparams(("parallel",)),
        name="fox_prep",
    )(proj, proj, proj, fproj, q_norm.reshape(1, HD), k_norm.reshape(1, HD), fb_pad)


def _cumsum_block(x, carry):
    n = x.shape[0]
    ri = lax.broadcasted_iota(jnp.int32, (n, n), 0)
    ci = lax.broadcasted_iota(jnp.int32, (n, n), 1)
    tril = jnp.where(ci <= ri, 1.0, 0.0)
    return _dot_exact_lhs01(tril, x) + carry


def _cumsum_kernel(x_ref, o_ref, carry_ref):
    @pl.when(pl.program_id(1) == 0)
    def _():
        carry_ref[...] = jnp.zeros_like(carry_ref)

    cs = _cumsum_block(x_ref[0], carry_ref[...])
    o_ref[0] = cs
    carry_ref[...] = cs[cs.shape[0] - 1:, :]


def cumsum_rows(x):
    b, t, w = x.shape
    tb = _pick(t, (256, 128, 64, 32, 16, 8))
    return pl.pallas_call(
        _cumsum_kernel,
        grid=(b, t // tb),
        in_specs=[pl.BlockSpec((1, tb, w), lambda i, j: (i, j, 0))],
        out_specs=pl.BlockSpec((1, tb, w), lambda i, j: (i, j, 0)),
        out_shape=jax.ShapeDtypeStruct((b, t, w), F32),
        scratch_shapes=[pltpu.VMEM((1, w), F32)],
        compiler_params=_cparams(("parallel", "arbitrary")),
        name="cumsum_rows",
    )(x)


def _paged_cumsum_kernel(pt_ref, *refs, pps):
    lp_refs = refs[:pps]
    ln_ref, o_ref, on_ref, carry_ref = refs[pps:]
    p = pl.program_id(1)

    @pl.when(p == 0)
    def _():
        carry_ref[...] = jnp.zeros_like(carry_ref)

    carry = carry_ref[...]
    for j in range(pps):
        cs = _cumsum_block(lp_refs[j][0], carry)
        o_ref[0, j] = cs
        carry = cs[cs.shape[0] - 1:, :]
    carry_ref[...] = carry

    @pl.when(p == pl.num_programs(1) - 1)
    def _():
        on_ref[0] = _cumsum_block(ln_ref[0], carry)


def paged_cumsum(page_table_flat, cache_logf, lf_new, *, layer_off, nb, npages, page, H):
    pps = _pick(npages, CUMSUM_PAGES_PER_STEP)
    ts = lf_new.shape[1]

    def pool_map(j):
        return lambda b, p, pt: (layer_off + pt[b * npages + p * pps + j], 0, 0)

    return pl.pallas_call(
        functools.partial(_paged_cumsum_kernel, pps=pps),
        grid_spec=pltpu.PrefetchScalarGridSpec(
            num_scalar_prefetch=1,
            grid=(nb, npages // pps),
            in_specs=[pl.BlockSpec((1, page, H), pool_map(j)) for j in range(pps)]
            + [pl.BlockSpec((1, ts, H), lambda b, p, pt: (b, 0, 0))],
            out_specs=[pl.BlockSpec((1, pps, page, H), lambda b, p, pt: (b, p, 0, 0)),
                       pl.BlockSpec((1, ts, H), lambda b, p, pt: (b, 0, 0))],
            scratch_shapes=[pltpu.VMEM((1, H), F32)],
        ),
        out_shape=[jax.ShapeDtypeStruct((nb, npages, page, H), F32),
                   jax.ShapeDtypeStruct((nb, ts, H), F32)],
        compiler_params=_cparams(("parallel", "arbitrary")),
        name="paged_cumsum",
    )(page_table_flat, *([cache_logf] * pps), lf_new)


def _fox_prompt_kernel(q_ref, k_ref, v_ref, c_ref, g_ref, o_ref, *, tq):
    qi = pl.program_id(2)
    q = q_ref[...]
    hd = q.shape[1]

    def block(j, carry, diag):
        m, l, acc = carry
        start = pl.multiple_of(j * tq, tq)
        k = k_ref[pl.ds(start, tq), :]
        v = v_ref[pl.ds(start, tq), :]
        s = _dot_nt(q, k) - c_ref[0, 0, :, pl.ds(start, tq)]
        if diag:
            rpos = lax.broadcasted_iota(jnp.int32, (tq, tq), 0)
            cpos = lax.broadcasted_iota(jnp.int32, (tq, tq), 1)
            s = jnp.where(cpos <= rpos, s, NEG)
        m_new = jnp.maximum(m, jnp.max(s, axis=1, keepdims=True))
        alpha = jnp.exp(m - m_new)
        p = jnp.exp(s - m_new)
        l = alpha * l + jnp.sum(p, axis=1, keepdims=True)
        acc = alpha * acc + _dot(p, v)
        return m_new, l, acc

    init = (jnp.full((tq, 1), NEG, F32), jnp.zeros((tq, 1), F32), jnp.zeros((tq, hd), F32))
    carry = lax.fori_loop(0, qi, lambda j, c: block(j, c, False), init)
    m, l, acc = block(qi, carry, True)
    g = g_ref[...]
    o_ref[...] = ((acc / l) * _sigmoid(g)).astype(o_ref.dtype)


def fox_prompt_attention(qb, kb, vb, c_row, proj, *, nb, t, H, HD):
    tq = _pick(t, ATTN_TQ)
    nq = t // tq
    return pl.pallas_call(
        functools.partial(_fox_prompt_kernel, tq=tq),
        grid=(nb, H, nq),
        in_specs=[pl.BlockSpec((tq, HD), lambda b, h, i: (b * nq + i, h)),
                  pl.BlockSpec((t, HD), lambda b, h, i: (b, h)),
                  pl.BlockSpec((t, HD), lambda b, h, i: (b, h)),
                  pl.BlockSpec((1, 1, 1, t), lambda b, h, i: (b, h, 0, 0)),
                  pl.BlockSpec((tq, HD), lambda b, h, i: (b * nq + i, 3 * H + h))],
        out_specs=pl.BlockSpec((tq, HD), lambda b, h, i: (b * nq + i, h)),
        out_shape=jax.ShapeDtypeStruct((nb * t, H * HD), BF16),
        compiler_params=_cparams(("parallel", "parallel", "arbitrary")),
        name="fox_prompt_attn",
    )(qb, kb, vb, c_row, proj)


def _fox_decode_kernel(pt_ref, *refs, pps, H, HD, NQ):
    q_ref = refs[0]
    kp_refs = refs[1:1 + pps]
    vp_refs = refs[1 + pps:1 + 2 * pps]
    (kn_ref, vn_ref, cum_ref, cn_ref, g_ref,
     o_ref, qcat_ref, mask_ref, m_ref, l_ref, acc_ref) = refs[1 + 2 * pps:]
    p = pl.program_id(1)
    rows = H * NQ
    width = mask_ref.shape[1]

    @pl.when(p == 0)
    def _():
        q = q_ref[0].astype(F32)
        for h in range(H):
            qcat_ref[h * NQ:(h + 1) * NQ, :] = q[:, h * HD:(h + 1) * HD]
        rh = lax.broadcasted_iota(jnp.int32, (rows, width), 0) // NQ
        ch = lax.broadcasted_iota(jnp.int32, (rows, width), 1) % H
        mask_ref[...] = jnp.where(rh == ch, 0.0, NEG)
        m_ref[...] = jnp.full_like(m_ref, NEG)
        l_ref[...] = jnp.zeros_like(l_ref)
        acc_ref[...] = jnp.zeros_like(acc_ref)

    qc = qcat_ref[...].astype(BF16)

    def update(s, v):
        m = m_ref[...]
        m_new = jnp.maximum(m, jnp.max(s, axis=1, keepdims=True))
        alpha = jnp.exp(m - m_new)
        pr = jnp.exp(s - m_new)
        l_ref[...] = alpha * l_ref[...] + jnp.sum(pr, axis=1, keepdims=True)
        acc_ref[...] = alpha * acc_ref[...] + _dot(pr, v)
        m_ref[...] = m_new

    for j in range(pps):
        s = _dot_nt(qc, kp_refs[j][0]) + (mask_ref[...] - cum_ref[0, j])
        update(s, vp_refs[j][0])

    @pl.when(p == pl.num_programs(1) - 1)
    def _():
        wn = kn_ref.shape[1]
        r = lax.broadcasted_iota(jnp.int32, (rows, wn), 0)
        c = lax.broadcasted_iota(jnp.int32, (rows, wn), 1)
        ok = (r // NQ == c % H) & (c // H <= r % NQ)
        s = jnp.where(ok, _dot_nt(qc, kn_ref[0]) - cn_ref[0], NEG)
        update(s, vn_ref[0])
        out = acc_ref[...] / l_ref[...]
        for h in range(H):
            cs = slice(h * HD, (h + 1) * HD)
            o_ref[:, cs] = (out[h * NQ:(h + 1) * NQ, :] * _sigmoid(g_ref[:, cs])).astype(o_ref.dtype)


def fox_decode_attention(page_table_flat, q_s, cache_k, cache_v, kn, vn, cum, cnew, gate_src, *,
                         layer_off, nb, npages, page, H, HD, NQ, gate_rb0):
    d = H * HD
    pps = _pick(npages, DECODE_PAGES_PER_STEP)
    rows = H * NQ

    def pool_map(j):
        return lambda b, p, pt: (layer_off + pt[b * npages + p * pps + j], 0, 0)

    per_b = lambda b, p, pt: (b, 0, 0)
    pool_specs = [pl.BlockSpec((1, page * H, HD), pool_map(j)) for j in range(pps)]
    return pl.pallas_call(
        functools.partial(_fox_decode_kernel, pps=pps, H=H, HD=HD, NQ=NQ),
        grid_spec=pltpu.PrefetchScalarGridSpec(
            num_scalar_prefetch=1,
            grid=(nb, npages // pps),
            in_specs=[pl.BlockSpec((1, NQ, d), per_b)] + pool_specs + pool_specs
            + [pl.BlockSpec((1, NQ * H, HD), per_b),
               pl.BlockSpec((1, NQ * H, HD), per_b),
               pl.BlockSpec((1, pps, 1, page * H), lambda b, p, pt: (b, p, 0, 0)),
               pl.BlockSpec((1, 1, NQ * H), per_b),
               pl.BlockSpec((NQ, d), lambda b, p, pt: (gate_rb0 + b, 3))],
            out_specs=pl.BlockSpec((NQ, d), lambda b, p, pt: (b, 0)),
            scratch_shapes=[pltpu.VMEM((rows, HD), F32),
                            pltpu.VMEM((rows, page * H), F32),
                            pltpu.VMEM((rows, 1), F32),
                            pltpu.VMEM((rows, 1), F32),
                            pltpu.VMEM((rows, HD), F32)],
        ),
        out_shape=jax.ShapeDtypeStruct((nb * NQ, d), F32),
        compiler_params=_cparams(("parallel", "arbitrary")),
        name="fox_decode_attn",
    )(page_table_flat, q_s, *([cache_k] * pps), *([cache_v] * pps), kn, vn, cum, cnew, gate_src)


def _pad_lanes(v, start=0):
    out = jnp.zeros((1, LANES), F32)
    return lax.dynamic_update_slice(out, v.reshape(1, -1).astype(F32), (0, start))


def _pad_cols(w):
    return jnp.pad(w, ((0, 0), (0, LANES - w.shape[1])))


def kernel(x_prompt, x_sample, state_delta, state_conv, cache_k, cache_v, cache_logf, page_table,
           norm_mix, norm_mlp, a_w_in, a_conv, a_A_log, a_dt_bias, a_out_norm, a_w_out,
           b_w_in, b_f_bias, b_q_norm, b_k_norm, b_w_out, w_up, w_down):
    bp, t, d = x_prompt.shape
    bs, ts, _ = x_sample.shape
    depth = norm_mix.shape[0]
    n_mixers = 2
    H = state_delta.shape[2]
    DK = state_delta.shape[3]
    nqkv = state_conv.shape[-1]
    HB = cache_k.shape[3]
    HD = cache_k.shape[4]
    n_pool, page = cache_k.shape[1], cache_k.shape[2]
    npages = page_table.shape[1]
    d_ff = w_up.shape[2]
    mp, ms = bp * t, bs * ts
    assert H * DK == d and HB * HD == d and nqkv == 3 * d and DK == LANES and HD == LANES
    assert 2 * H <= LANES and HB <= LANES and ts >= 3 and t >= 3

    x = jnp.concatenate([x_prompt.reshape(mp, d), x_sample.reshape(ms, d)], axis=0)
    pt_flat = page_table.reshape(-1).astype(jnp.int32)

    p_delta, p_conv, s_delta, s_conv = [], [], [], []
    p_k, p_v, p_lf, s_k, s_v, s_lf = [], [], [], [], [], []
    for i in range(depth):
        j = i // n_mixers
        hn = rmsnorm_bf16(x, norm_mix[i])
        if i % n_mixers == 0:
            w_in = a_w_in[j]
            proj = matmul_wcast(hn, w_in, n_out=nqkv + d)
            gates = matmul_wcast(hn, _pad_cols(w_in[:, nqkv + d:]), n_out=LANES)
            alog_pad = _pad_lanes(a_A_log[j], H)
            dtb_pad = _pad_lanes(a_dt_bias[j], H)
            common = dict(H=H, DK=DK)
            cp = 64 if t % 64 == 0 else t
            yp, stp = gdn_core(proj, gates, a_conv[j], alog_pad, dtb_pad, a_out_norm[j], None, None,
                               row0=0, nb=bp, t=t, C=cp, G=max(1, min(H, MXU_DIM // cp)),
                               out_dtype=BF16, **common)
            ys, sts = gdn_core(proj, gates, a_conv[j], alog_pad, dtb_pad, a_out_norm[j],
                               state_conv[j], state_delta[j],
                               row0=mp, nb=bs, t=ts, C=ts, G=H, out_dtype=F32, **common)
            x = matmul_wcast(yp, a_w_out[j], n_out=d, epilogue="residual", residual=x, row0=0)
            x = matmul_wcast(ys.astype(BF16), a_w_out[j], n_out=d, epilogue="residual", residual=x, row0=mp)
            p_conv.append(proj[:mp].reshape(bp, t, -1)[:, t - 3:, :nqkv])
            s_conv.append(proj[mp:].reshape(bs, ts, -1)[:, ts - 3:, :nqkv])
            p_delta.append(stp)
            s_delta.append(sts)
        else:
            w_in = b_w_in[j]
            proj = matmul_wcast(hn, w_in, n_out=4 * d)
            fproj = matmul_wcast(hn, _pad_cols(w_in[:, 4 * d:]), n_out=LANES)
            qb, kf, kb, vb, lf = fox_prep(proj, fproj, b_q_norm[j], b_k_norm[j],
                                          _pad_lanes(b_f_bias[j]), H=HB, HD=HD)
            c_p = cumsum_rows(lf[:mp].reshape(bp, t, LANES))
            c_row = jnp.transpose(c_p[:, :, :HB], (0, 2, 1)).reshape(bp, HB, 1, t)
            op = fox_prompt_attention(qb, kb, vb, c_row, proj, nb=bp, t=t, H=HB, HD=HD)
            cum, cnew = paged_cumsum(pt_flat, cache_logf.reshape(-1, page, HB),
                                     lf[mp:, :HB].reshape(bs, ts, HB),
                                     layer_off=j * n_pool, nb=bs, npages=npages, page=page, H=HB)
            os_ = fox_decode_attention(
                pt_flat, qb[mp:].reshape(bs, ts, d),
                cache_k.reshape(-1, page * HB, HD), cache_v.reshape(-1, page * HB, HD),
                kb[mp:].reshape(bs, ts * HB, HD), vb[mp:].reshape(bs, ts * HB, HD),
                cum.reshape(bs, npages, 1, page * HB), cnew.reshape(bs, 1, ts * HB), proj,
                layer_off=j * n_pool, nb=bs, npages=npages, page=page, H=HB, HD=HD, NQ=ts,
                gate_rb0=mp // ts)
            x = matmul_wcast(op, b_w_out[j], n_out=d, epilogue="residual", residual=x, row0=0)
            x = matmul_wcast(os_.astype(BF16), b_w_out[j], n_out=d, epilogue="residual", residual=x, row0=mp)
            p_k.append(kf[:mp].reshape(bp, t, HB, HD))
            p_v.append(proj[:mp, 2 * d:3 * d].reshape(bp, t, HB, HD))
            p_lf.append(lf[:mp, :HB].reshape(bp, t, HB))
            s_k.append(kf[mp:].reshape(bs, ts, HB, HD))
            s_v.append(proj[mp:, 2 * d:3 * d].reshape(bs, ts, HB, HD))
            s_lf.append(lf[mp:, :HB].reshape(bs, ts, HB))
        hm = rmsnorm_bf16(x, norm_mlp[i])
        a = matmul_wcast(hm, w_up[i], n_out=d_ff, epilogue="relu2", out_dtype=BF16)
        x = matmul_ktiled(a, w_down[i].astype(BF16), epilogue="residual", residual=x)
    return (x[:mp].reshape(bp, t, d), x[mp:].reshape(bs, ts, d),
            jnp.stack(p_delta), jnp.stack(p_conv), jnp.stack(p_k), jnp.stack(p_v), jnp.stack(p_lf),
            jnp.stack(s_delta), jnp.stack(s_conv), jnp.stack(s_k), jnp.stack(s_v), jnp.stack(s_lf))
```

```python
import functools

import jax
import jax.numpy as jnp
from jax import lax
from jax.experimental import pallas as pl
from jax.experimental.pallas import tpu as pltpu

F32 = jnp.float32
BF16 = jnp.bfloat16
EPS = 1e-6
NEG = -1e30
LANES = 128
SUBLANES = 8
MXU_DIM = 256
VMEM_LIMIT = 48 * 1024 * 1024

ROW_TILES = (768, 512, 528, 384, 256, 128, 64, 32, 16, 8)
COL_TILE = 1024
K_TILE = 2048
ATTN_TQ = (512, 256, 128)
DECODE_PAGES_PER_STEP = (4, 2, 1)
DECODE_KEY_CHUNK = (256, 128)
CUMSUM_PAGES_PER_STEP = (8, 4, 2, 1)


def _pick(n, cands):
    for c in cands:
        if n % c == 0:
            return c
    return n


def _cparams(sem):
    return pltpu.CompilerParams(dimension_semantics=sem, vmem_limit_bytes=VMEM_LIMIT)


def _dot(a, b):
    return jnp.dot(a.astype(BF16), b.astype(BF16), preferred_element_type=F32)


def _dot_nt(a, b):
    return lax.dot_general(a.astype(BF16), b.astype(BF16), (((1,), (1,)), ((), ())),
                           preferred_element_type=F32)


def _split3(x):
    hi = x.astype(BF16)
    r = x - hi.astype(F32)
    mid = r.astype(BF16)
    lo = (r - mid.astype(F32)).astype(BF16)
    return hi, mid, lo


def _dot_exact_lhs01(a01, b):
    hi, mid, lo = _split3(b)
    a = a01.astype(BF16)
    d = functools.partial(jnp.dot, preferred_element_type=F32)
    return d(a, hi) + (d(a, mid) + d(a, lo))


def _sigmoid(x):
    return 1.0 / (1.0 + jnp.exp(-x))


def _softplus(x):
    return jnp.maximum(x, 0.0) + jnp.log1p(jnp.exp(-jnp.abs(x)))


def _rmsnorm_kernel(x_ref, w_ref, o_ref):
    x = x_ref[...]
    ms = jnp.mean(x * x, axis=-1, keepdims=True)
    o_ref[...] = (x * lax.rsqrt(ms + EPS) * w_ref[...]).astype(o_ref.dtype)


def rmsnorm_bf16(x, w):
    m, d = x.shape
    tm = _pick(m, (256, 128, 64, 32, 16))
    return pl.pallas_call(
        _rmsnorm_kernel,
        grid=(m // tm,),
        in_specs=[pl.BlockSpec((tm, d), lambda i: (i, 0)),
                  pl.BlockSpec((1, d), lambda i: (0, 0))],
        out_specs=pl.BlockSpec((tm, d), lambda i: (i, 0)),
        out_shape=jax.ShapeDtypeStruct((m, d), BF16),
        compiler_params=_cparams(("parallel",)),
        name="rmsnorm",
    )(x, w.reshape(1, d))


def _epilogue(acc, kind, r_ref):
    if kind == "relu2":
        acc = jnp.maximum(acc, 0.0)
        acc = acc * acc
    elif kind == "residual":
        acc = acc + r_ref[...]
    return acc


def _mm_wcast_kernel(*refs, epilogue, w_transposed):
    if epilogue == "residual":
        a_ref, w_ref, r_ref, o_ref, wb_ref = refs
    else:
        a_ref, w_ref, o_ref, wb_ref = refs
        r_ref = None

    @pl.when(pl.program_id(1) == 0)
    def _():
        wb_ref[...] = w_ref[...].astype(wb_ref.dtype)

    if w_transposed:
        acc = _dot_nt(a_ref[...], wb_ref[...])
    else:
        acc = jnp.dot(a_ref[...], wb_ref[...], preferred_element_type=F32)
    o_ref[...] = _epilogue(acc, epilogue, r_ref).astype(o_ref.dtype)


def matmul_wcast(a, w, layer, *, n_out, w_transposed=False, epilogue="none", residual=None,
                 out_dtype=F32, row0=0):
    ma, kdim = a.shape
    tm = _pick(ma, ROW_TILES if epilogue != "residual" else (512, 256, 128, 64, 32, 16, 8))
    tn = min(COL_TILE, n_out)
    assert n_out % tn == 0 and row0 % tm == 0 and w.shape[2 if w_transposed else 1] == kdim
    r0 = row0 // tm
    if w_transposed:
        w_spec = pl.BlockSpec((None, tn, kdim), lambda j, i: (layer, j, 0))
        wb_shape = (tn, kdim)
    else:
        w_spec = pl.BlockSpec((None, kdim, tn), lambda j, i: (layer, 0, j))
        wb_shape = (kdim, tn)
    in_specs = [pl.BlockSpec((tm, kdim), lambda j, i: (i, 0)), w_spec]
    args = [a, w]
    aliases = {}
    if epilogue == "residual":
        in_specs.append(pl.BlockSpec((tm, tn), lambda j, i: (r0 + i, j)))
        args.append(residual)
        aliases = {2: 0}
        out_rows = residual.shape[0]
    else:
        out_rows = ma
    return pl.pallas_call(
        functools.partial(_mm_wcast_kernel, epilogue=epilogue, w_transposed=w_transposed),
        grid=(n_out // tn, ma // tm),
        in_specs=in_specs,
        out_specs=pl.BlockSpec((tm, tn), lambda j, i: (r0 + i, j)),
        out_shape=jax.ShapeDtypeStruct((out_rows, n_out), out_dtype),
        scratch_shapes=[pltpu.VMEM(wb_shape, BF16)],
        input_output_aliases=aliases,
        compiler_params=_cparams(("arbitrary", "arbitrary")),
        name="matmul_wcast_" + epilogue,
    )(*args)


def _mm_kernel(*refs, epilogue, nk):
    if epilogue == "residual":
        a_ref, b_ref, r_ref, o_ref, acc_ref = refs
    else:
        a_ref, b_ref, o_ref, acc_ref = refs
        r_ref = None
    part = jnp.dot(a_ref[...], b_ref[...], preferred_element_type=F32)
    k = pl.program_id(2)

    @pl.when(k == 0)
    def _():
        acc_ref[...] = part

    @pl.when(k > 0)
    def _():
        acc_ref[...] += part

    @pl.when(k == nk - 1)
    def _():
        o_ref[...] = _epilogue(acc_ref[...], epilogue, r_ref).astype(o_ref.dtype)


def matmul_ktiled(a, b, layer, *, epilogue="none", residual=None, out_dtype=F32):
    m, kdim = a.shape
    _, _, n = b.shape
    tm, tn, tk = _pick(m, ROW_TILES), min(COL_TILE, n), min(K_TILE, kdim)
    assert n % tn == 0 and kdim % tk == 0
    nk = kdim // tk
    in_specs = [pl.BlockSpec((tm, tk), lambda i, j, k: (i, k)),
                pl.BlockSpec((None, tk, tn), lambda i, j, k: (layer, k, j))]
    args = [a, b]
    if epilogue == "residual":
        in_specs.append(pl.BlockSpec((tm, tn), lambda i, j, k: (i, j)))
        args.append(residual)
    return pl.pallas_call(
        functools.partial(_mm_kernel, epilogue=epilogue, nk=nk),
        grid=(m // tm, n // tn, nk),
        in_specs=in_specs,
        out_specs=pl.BlockSpec((tm, tn), lambda i, j, k: (i, j)),
        out_shape=jax.ShapeDtypeStruct((m, n), out_dtype),
        scratch_shapes=[pltpu.VMEM((tm, tn), F32)],
        compiler_params=_cparams(("parallel", "parallel", "arbitrary")),
        name="matmul_ktiled_" + epilogue,
    )(*args)


def _tri_inv_all(lmats, blk, eye, c):
    base = SUBLANES
    same8 = blk(0, base) == blk(1, base)
    lds = [jnp.where(same8, lm, 0.0) for lm in lmats]
    ld2 = [_dot(ld, ld) for ld in lds]
    ld4 = [_dot(a, a) for a in ld2]
    xs = [jnp.where(eye, 1.0, 0.0) - ld for ld in lds]
    xs = [x + _dot(x, a) for x, a in zip(xs, ld2)]
    xs = [x + _dot(x, a) for x, a in zip(xs, ld4)]
    m = base
    while m < c:
        sel = (blk(0, 2 * m) == blk(1, 2 * m)) & (blk(0, m) != blk(1, m))
        offx = [_dot(jnp.where(sel, lm, 0.0), x) for lm, x in zip(lmats, xs)]
        xs = [x - _dot(x, ox) for x, ox in zip(xs, offx)]
        m *= 2
    return xs


def _gdn_kernel(*refs, C, G, H, DK, has_cbuf, has_s0):
    it = iter(refs)
    qkv_ref = next(it)
    z_ref = next(it)
    gates_ref = next(it)
    cw_ref = next(it)
    alog_ref = next(it)
    dtb_ref = next(it)
    onw_ref = next(it)
    cbuf_ref = next(it) if has_cbuf else None
    s0_ref = next(it) if has_s0 else None
    y_ref = next(it)
    sfin_ref = next(it)
    ext_ref = next(it)
    scat_ref = next(it)

    c_idx = pl.program_id(1)
    nc = pl.num_programs(1)
    R = G * C
    NG = H // G
    NQK = H * DK
    pad = SUBLANES

    @pl.when(c_idx == 0)
    def _init():
        ext_ref[0:pad, :] = jnp.zeros((pad, ext_ref.shape[1]), F32)
        if has_cbuf:
            ext_ref[pad - 3:pad, :] = cbuf_ref[0]
        for h in range(H):
            if has_s0:
                scat_ref[:, h * DK:(h + 1) * DK] = s0_ref[0, h]
            else:
                scat_ref[:, h * DK:(h + 1) * DK] = jnp.zeros((DK, DK), F32)

    ext_ref[pad:pad + C, :] = qkv_ref[...]

    def conv_act(col):
        sl = slice(col, col + DK)
        acc = ext_ref[pad:pad + C, sl] * cw_ref[3:4, sl]
        for i in range(3):
            acc = acc + ext_ref[pad - 3 + i:pad - 3 + i + C, sl] * cw_ref[i:i + 1, sl]
        return acc * _sigmoid(acc)

    gt = gates_ref[...]
    beta_full = _sigmoid(gt)
    g_full = -jnp.exp(alog_ref[...]) * _softplus(gt + dtb_ref[...])

    iotas = (lax.broadcasted_iota(jnp.int32, (R, R), 0), lax.broadcasted_iota(jnp.int32, (R, R), 1))

    def blk(axis, m):
        return lax.shift_right_logical(iotas[axis], m.bit_length() - 1)

    same = blk(0, C) == blk(1, C)
    eye = iotas[0] == iotas[1]
    incl = same & (iotas[1] <= iotas[0])
    strict = same & (iotas[1] < iotas[0])
    incl_t = same & (iotas[0] <= iotas[1])
    rowblk = lax.shift_right_logical(lax.broadcasted_iota(jnp.int32, (R, DK), 0), C.bit_length() - 1)
    cat = (lambda xs: xs[0]) if G == 1 else (lambda xs: jnp.concatenate(xs, axis=0))
    groups = [list(range(gi * G, (gi + 1) * G)) for gi in range(NG)]

    Qs, Ks, Vs, betas, gcols = [], [], [], [], []
    for heads in groups:
        qs, ks, vs = [], [], []
        for h in heads:
            q = conv_act(h * DK)
            k = conv_act(NQK + h * DK)
            qs.append(q * lax.rsqrt(jnp.sum(q * q, axis=-1, keepdims=True) + EPS) * (DK ** -0.5))
            ks.append(k * lax.rsqrt(jnp.sum(k * k, axis=-1, keepdims=True) + EPS))
            vs.append(conv_act(2 * NQK + h * DK))
        Qs.append(cat(qs))
        Ks.append(cat(ks))
        Vs.append(cat(vs))
        betas.append(cat([beta_full[:, h:h + 1] for h in heads]))
        gcols.append(cat([g_full[:, H + h:H + h + 1] for h in heads]))

    gcs, gls, decays = [], [], []
    for g_c in gcols:
        g_r = jnp.sum(jnp.where(eye, g_c, 0.0), axis=0, keepdims=True)
        gc_c = jnp.sum(jnp.where(incl, g_r, 0.0), axis=1, keepdims=True)
        gc_r = jnp.sum(jnp.where(incl_t, g_c, 0.0), axis=0, keepdims=True)
        gls.append(jnp.sum(jnp.where(same, g_r, 0.0), axis=1, keepdims=True))
        gcs.append(gc_c)
        decays.append(jnp.exp(jnp.where(incl, gc_c - gc_r, NEG)))

    KBs = [k * b for k, b in zip(Ks, betas)]
    lmats = [jnp.where(strict, _dot_nt(kb, k) * d, 0.0) for kb, k, d in zip(KBs, Ks, decays)]
    amats = [jnp.where(incl, _dot_nt(q, k) * d, 0.0) for q, k, d in zip(Qs, Ks, decays)]
    tmats = _tri_inv_all(lmats, blk, eye, C)
    egs = [jnp.exp(gc) for gc in gcs]
    UWs = [_dot(t, jnp.concatenate([v * b, kb * eg], axis=1))
           for t, v, b, kb, eg in zip(tmats, Vs, betas, KBs, egs)]
    scats = [scat_ref[:, heads[0] * DK:(heads[-1] + 1) * DK] for heads in groups]
    WQSs = [_dot(jnp.concatenate([uw[:, DK:], q * eg], axis=0), sc)
            for uw, q, eg, sc in zip(UWs, Qs, egs, scats)]
    Vns, Ois = [], []
    for uw, wqs in zip(UWs, WQSs):
        vn, oi = [], []
        for j in range(G):
            rs = slice(j * C, (j + 1) * C)
            cs = slice(j * DK, (j + 1) * DK)
            vn.append(uw[rs, :DK] - wqs[rs, cs])
            oi.append(wqs[R + j * C:R + (j + 1) * C, cs])
        Vns.append(cat(vn))
        Ois.append(cat(oi))
    Os = [oi + _dot(a, vn) for oi, a, vn in zip(Ois, amats, Vns)]
    dSs = []
    for k, gl, gc, vn in zip(Ks, gls, gcs, Vns):
        ke = k * jnp.exp(gl - gc)
        vwide = jnp.concatenate([jnp.where(rowblk == j, vn, 0.0) for j in range(G)], axis=1)
        dSs.append(_dot(ke.T, vwide))
    for heads, sc, ds, gl, o_all in zip(groups, scats, dSs, gls, Os):
        for j, h in enumerate(heads):
            cs = slice(j * DK, (j + 1) * DK)
            e_last = jnp.exp(gl[j * C:j * C + 1, :])
            scat_ref[:, h * DK:(h + 1) * DK] = sc[:, cs] * e_last + ds[:, cs]
            o = o_all[j * C:(j + 1) * C]
            o = o * lax.rsqrt(jnp.mean(o * o, axis=-1, keepdims=True) + EPS) * onw_ref[...]
            zg = z_ref[:, h * DK:(h + 1) * DK]
            y_ref[:, h * DK:(h + 1) * DK] = (o * (zg * _sigmoid(zg))).astype(y_ref.dtype)

    carry = ext_ref[C:C + pad, :]
    ext_ref[0:pad, :] = carry

    @pl.when(c_idx == nc - 1)
    def _fin():
        for h in range(H):
            sfin_ref[0, h] = scat_ref[:, h * DK:(h + 1) * DK]


def gdn_core(proj, gates, conv_w, alog_pad, dtb_pad, out_norm, conv_buf, s0, *,
             row0, nb, t, C, G, H, DK, out_dtype):
    nqkv = 3 * H * DK
    nc = t // C
    assert t % C == 0 and row0 % C == 0 and H % G == 0 and C & (C - 1) == 0
    rb0 = row0 // C
    row_map = lambda b, c: (rb0 + b * nc + c, 0)
    has_cbuf = conv_buf is not None
    has_s0 = s0 is not None
    in_specs = [
        pl.BlockSpec((C, nqkv), row_map),
        pl.BlockSpec((C, H * DK), lambda b, c: (rb0 + b * nc + c, 3)),
        pl.BlockSpec((C, LANES), row_map),
        pl.BlockSpec((4, nqkv), lambda b, c: (0, 0)),
        pl.BlockSpec((1, LANES), lambda b, c: (0, 0)),
        pl.BlockSpec((1, LANES), lambda b, c: (0, 0)),
        pl.BlockSpec((1, DK), lambda b, c: (0, 0)),
    ]
    args = [proj, proj, gates, conv_w, alog_pad, dtb_pad, out_norm.reshape(1, DK)]
    if has_cbuf:
        in_specs.append(pl.BlockSpec((1, 3, nqkv), lambda b, c: (b, 0, 0)))
        args.append(conv_buf)
    if has_s0:
        in_specs.append(pl.BlockSpec((1, H, DK, DK), lambda b, c: (b, 0, 0, 0)))
        args.append(s0)
    y, sfin = pl.pallas_call(
        functools.partial(_gdn_kernel, C=C, G=G, H=H, DK=DK, has_cbuf=has_cbuf, has_s0=has_s0),
        grid=(nb, nc),
        in_specs=in_specs,
        out_specs=[pl.BlockSpec((C, H * DK), lambda b, c: (b * nc + c, 0)),
                   pl.BlockSpec((1, H, DK, DK), lambda b, c: (b, 0, 0, 0))],
        out_shape=[jax.ShapeDtypeStruct((nb * t, H * DK), out_dtype),
                   jax.ShapeDtypeStruct((nb, H, DK, DK), F32)],
        scratch_shapes=[pltpu.VMEM((C + SUBLANES, nqkv), F32),
                        pltpu.VMEM((DK, H * DK), F32)],
        compiler_params=_cparams(("parallel", "arbitrary")),
        name="gdn_core_c%d" % C,
    )(*args)
    return y, sfin


def _fox_prep_kernel(q_ref, k_ref, v_ref, f_ref, qn_ref, kn_ref, fb_ref,
                     qo_ref, ko_ref, kb_ref, vb_ref, lf_ref, *, H, HD):
    for h in range(H):
        sl = slice(h * HD, (h + 1) * HD)
        q = q_ref[:, sl]
        q = q * lax.rsqrt(jnp.mean(q * q, axis=-1, keepdims=True) + EPS) * qn_ref[...]
        qo_ref[:, sl] = (q * (HD ** -0.5)).astype(qo_ref.dtype)
        k = k_ref[:, sl]
        k = k * lax.rsqrt(jnp.mean(k * k, axis=-1, keepdims=True) + EPS) * kn_ref[...]
        ko_ref[:, sl] = k
        kb_ref[:, sl] = k.astype(kb_ref.dtype)
    vb_ref[...] = v_ref[...].astype(vb_ref.dtype)
    x = f_ref[...] + fb_ref[...]
    lf_ref[...] = jnp.minimum(x, 0.0) - jnp.log1p(jnp.exp(-jnp.abs(x)))


def fox_prep(proj, fproj, q_norm, k_norm, fb_pad, *, H, HD):
    m = proj.shape[0]
    d = H * HD
    tm = _pick(m, (256, 128, 64, 32, 16))
    col = lambda j: (lambda i: (i, j))
    return pl.pallas_call(
        functools.partial(_fox_prep_kernel, H=H, HD=HD),
        grid=(m // tm,),
        in_specs=[pl.BlockSpec((tm, d), col(0)), pl.BlockSpec((tm, d), col(1)),
                  pl.BlockSpec((tm, d), col(2)), pl.BlockSpec((tm, LANES), col(0)),
                  pl.BlockSpec((1, HD), lambda i: (0, 0)), pl.BlockSpec((1, HD), lambda i: (0, 0)),
                  pl.BlockSpec((1, LANES), lambda i: (0, 0))],
        out_specs=[pl.BlockSpec((tm, d), col(0)), pl.BlockSpec((tm, d), col(0)),
                   pl.BlockSpec((tm, d), col(0)), pl.BlockSpec((tm, d), col(0)),
                   pl.BlockSpec((tm, LANES), col(0))],
        out_shape=[jax.ShapeDtypeStruct((m, d), BF16), jax.ShapeDtypeStruct((m, d), F32),
                   jax.ShapeDtypeStruct((m, d), BF16), jax.ShapeDtypeStruct((m, d), BF16),
                   jax.ShapeDtypeStruct((m, LANES), F32)],
        compiler_params=_cparams(("parallel",)),
        name="fox_prep",
    )(proj, proj, proj, fproj, q_norm.reshape(1, HD), k_norm.reshape(1, HD), fb_pad)


def _cumsum_block(x, carry):
    n = x.shape[0]
    ri = lax.broadcasted_iota(jnp.int32, (n, n), 0)
    ci = lax.broadcasted_iota(jnp.int32, (n, n), 1)
    tril = jnp.where(ci <= ri, 1.0, 0.0)
    return _dot_exact_lhs01(tril, x) + carry


def _cumsum_kernel(x_ref, o_ref, carry_ref):
    @pl.when(pl.program_id(1) == 0)
    def _():
        carry_ref[...] = jnp.zeros_like(carry_ref)

    cs = _cumsum_block(x_ref[0], carry_ref[...])
    o_ref[0] = cs
    carry_ref[...] = cs[cs.shape[0] - 1:, :]


def cumsum_rows(x):
    b, t, w = x.shape
    tb = _pick(t, (256, 128, 64, 32, 16, 8))
    return pl.pallas_call(
        _cumsum_kernel,
        grid=(b, t // tb),
        in_specs=[pl.BlockSpec((1, tb, w), lambda i, j: (i, j, 0))],
        out_specs=pl.BlockSpec((1, tb, w), lambda i, j: (i, j, 0)),
        out_shape=jax.ShapeDtypeStruct((b, t, w), F32),
        scratch_shapes=[pltpu.VMEM((1, w), F32)],
        compiler_params=_cparams(("parallel", "arbitrary")),
        name="cumsum_rows",
    )(x)


def _cumsum_lanes(x, carry):
    n = x.shape[1]
    ri = lax.broadcasted_iota(jnp.int32, (n, n), 0)
    ci = lax.broadcasted_iota(jnp.int32, (n, n), 1)
    triu = jnp.where(ri <= ci, 1.0, 0.0).astype(BF16)
    hi, mid, lo = _split3(x)
    d = functools.partial(jnp.dot, preferred_element_type=F32)
    return d(hi, triu) + (d(mid, triu) + d(lo, triu)) + carry


def _paged_cumsum_kernel(pt_ref, *refs, pps):
    lp_refs = refs[:pps]
    ln_ref, o_ref, on_ref, carry_ref = refs[pps:]
    p = pl.program_id(1)

    @pl.when(p == 0)
    def _():
        carry_ref[...] = jnp.zeros_like(carry_ref)

    carry = carry_ref[...]
    for j in range(pps):
        cs = _cumsum_lanes(lp_refs[j][0], carry)
        o_ref[0, j] = cs
        carry = cs[:, cs.shape[1] - 1:]
    carry_ref[...] = carry

    @pl.when(p == pl.num_programs(1) - 1)
    def _():
        on_ref[0] = _cumsum_lanes(ln_ref[0], carry)


def paged_cumsum(page_table_flat, logf_hp, lf_new_hp, *, layer_off, nb, npages, page, H):
    pps = _pick(npages, CUMSUM_PAGES_PER_STEP)
    ts = lf_new_hp.shape[2]
    assert ts == page

    def pool_map(j):
        return lambda b, p, pt: (layer_off + pt[b * npages + p * pps + j], 0, 0)

    return pl.pallas_call(
        functools.partial(_paged_cumsum_kernel, pps=pps),
        grid_spec=pltpu.PrefetchScalarGridSpec(
            num_scalar_prefetch=1,
            grid=(nb, npages // pps),
            in_specs=[pl.BlockSpec((1, H, page), pool_map(j)) for j in range(pps)]
            + [pl.BlockSpec((1, H, ts), lambda b, p, pt: (b, 0, 0))],
            out_specs=[pl.BlockSpec((1, pps, H, page), lambda b, p, pt: (b, p, 0, 0)),
                       pl.BlockSpec((1, H, ts), lambda b, p, pt: (b, 0, 0))],
            scratch_shapes=[pltpu.VMEM((H, 1), F32)],
        ),
        out_shape=[jax.ShapeDtypeStruct((nb, npages, H, page), F32),
                   jax.ShapeDtypeStruct((nb, H, ts), F32)],
        compiler_params=_cparams(("parallel", "arbitrary")),
        name="paged_cumsum",
    )(page_table_flat, *([logf_hp] * pps), lf_new_hp)


def _fox_prompt_kernel(q_ref, k_ref, v_ref, c_ref, g_ref, o_ref, *, tq):
    qi = pl.program_id(2)
    q = q_ref[...]
    hd = q.shape[1]

    def block(j, carry, diag):
        m, l, acc = carry
        start = pl.multiple_of(j * tq, tq)
        k = k_ref[pl.ds(start, tq), :]
        v = v_ref[pl.ds(start, tq), :]
        s = _dot_nt(q, k) - c_ref[0, 0, :, pl.ds(start, tq)]
        if diag:
            rpos = lax.broadcasted_iota(jnp.int32, (tq, tq), 0)
            cpos = lax.broadcasted_iota(jnp.int32, (tq, tq), 1)
            s = jnp.where(cpos <= rpos, s, NEG)
        m_new = jnp.maximum(m, jnp.max(s, axis=1, keepdims=True))
        alpha = jnp.exp(m - m_new)
        p = jnp.exp(s - m_new)
        l = alpha * l + jnp.sum(p, axis=1, keepdims=True)
        acc = alpha * acc + _dot(p, v)
        return m_new, l, acc

    init = (jnp.full((tq, 1), NEG, F32), jnp.zeros((tq, 1), F32), jnp.zeros((tq, hd), F32))
    carry = lax.fori_loop(0, qi, lambda j, c: block(j, c, False), init)
    m, l, acc = block(qi, carry, True)
    g = g_ref[...]
    o_ref[...] = ((acc / l) * _sigmoid(g)).astype(o_ref.dtype)


def fox_prompt_attention(qb, kb, vb, c_row, proj, *, nb, t, H, HD):
    tq = _pick(t, ATTN_TQ)
    nq = t // tq
    return pl.pallas_call(
        functools.partial(_fox_prompt_kernel, tq=tq),
        grid=(nb, H, nq),
        in_specs=[pl.BlockSpec((tq, HD), lambda b, h, i: (b * nq + i, h)),
                  pl.BlockSpec((t, HD), lambda b, h, i: (b, h)),
                  pl.BlockSpec((t, HD), lambda b, h, i: (b, h)),
                  pl.BlockSpec((1, 1, 1, t), lambda b, h, i: (b, h, 0, 0)),
                  pl.BlockSpec((tq, HD), lambda b, h, i: (b * nq + i, 3 * H + h))],
        out_specs=pl.BlockSpec((tq, HD), lambda b, h, i: (b * nq + i, h)),
        out_shape=jax.ShapeDtypeStruct((nb * t, H * HD), BF16),
        compiler_params=_cparams(("parallel", "parallel", "arbitrary")),
        name="fox_prompt_attn",
    )(qb, kb, vb, c_row, proj)


def _fox_decode_kernel(pt_ref, *refs, pps, H, HD, NQ):
    q_ref = refs[0]
    kp_refs = refs[1:1 + pps]
    vp_refs = refs[1 + pps:1 + 2 * pps]
    (kn_ref, vn_ref, cum_ref, cn_ref, g_ref,
     o_ref, qcat_ref, mask_ref, m_ref, acc_ref, s_ref, vx_ref) = refs[1 + 2 * pps:]
    p = pl.program_id(1)
    rows = H * NQ
    kc = mask_ref.shape[1]
    width = kp_refs[0].shape[1]

    @pl.when(p == 0)
    def _():
        q = q_ref[0].astype(F32)
        for h in range(H):
            qcat_ref[h * NQ:(h + 1) * NQ, :] = q[:, h * HD:(h + 1) * HD]
        rh = lax.broadcasted_iota(jnp.int32, (rows, kc), 0) // NQ
        ch = lax.broadcasted_iota(jnp.int32, (rows, kc), 1) % H
        mask_ref[...] = jnp.where(rh == ch, 0.0, NEG)
        m_ref[...] = jnp.full_like(m_ref, NEG)
        acc_ref[...] = jnp.zeros_like(acc_ref)
        vx_ref[:, :, HD:] = jnp.ones((pps, width, HD), vx_ref.dtype)

    qc = qcat_ref[...].astype(BF16)

    m, acc = m_ref[...], acc_ref[...]
    for j in range(pps):
        vx_ref[j, :, :HD] = vp_refs[j][0].astype(vx_ref.dtype)
        mx = jnp.full((rows, kc), NEG, F32)
        for c0 in range(0, width, kc):
            s = _dot_nt(qc, kp_refs[j][0, c0:c0 + kc, :]) + (mask_ref[...] - cum_ref[0, j, :, c0:c0 + kc])
            s_ref[:, j * width + c0:j * width + c0 + kc] = s
            mx = jnp.maximum(mx, s)
        m_new = jnp.maximum(m, jnp.max(mx, axis=1, keepdims=True))
        pv = jnp.zeros((rows, 2 * HD), F32)
        for c0 in range(0, width, kc):
            pr = jnp.exp(s_ref[:, j * width + c0:j * width + c0 + kc] - m_new)
            pv = pv + _dot(pr, vx_ref[j, c0:c0 + kc, :])
        acc = jnp.exp(m - m_new) * acc + pv
        m = m_new
    m_ref[...], acc_ref[...] = m, acc

    @pl.when(p == pl.num_programs(1) - 1)
    def _():
        wn = kn_ref.shape[1]
        r = lax.broadcasted_iota(jnp.int32, (rows, wn), 0)
        c = lax.broadcasted_iota(jnp.int32, (rows, wn), 1)
        ok = (r // NQ == c % H) & (c // H <= r % NQ)
        s = jnp.where(ok, _dot_nt(qc, kn_ref[0]) - cn_ref[0], NEG)
        m_fin = jnp.maximum(m, jnp.max(s, axis=1, keepdims=True))
        vn = vn_ref[0]
        fin = (jnp.exp(m - m_fin) * acc
               + _dot(jnp.exp(s - m_fin), jnp.concatenate([vn, jnp.ones_like(vn)], axis=1)))
        out = fin[:, :HD] / fin[:, HD:]
        for h in range(H):
            cs = slice(h * HD, (h + 1) * HD)
            o_ref[:, cs] = (out[h * NQ:(h + 1) * NQ, :] * _sigmoid(g_ref[:, cs])).astype(o_ref.dtype)


def fox_decode_attention(page_table_flat, q_s, cache_k, cache_v, kn, vn, cum, cnew, gate_src, *,
                         layer_off, nb, npages, page, H, HD, NQ, gate_rb0):
    d = H * HD
    pps = _pick(npages, DECODE_PAGES_PER_STEP)
    rows = H * NQ
    kc = _pick(page * H, DECODE_KEY_CHUNK)
    assert kc % H == 0 and (page * H) % kc == 0

    def pool_map(j):
        return lambda b, p, pt: (layer_off + pt[b * npages + p * pps + j], 0, 0)

    per_b = lambda b, p, pt: (b, 0, 0)
    pool_specs = [pl.BlockSpec((1, page * H, HD), pool_map(j)) for j in range(pps)]
    return pl.pallas_call(
        functools.partial(_fox_decode_kernel, pps=pps, H=H, HD=HD, NQ=NQ),
        grid_spec=pltpu.PrefetchScalarGridSpec(
            num_scalar_prefetch=1,
            grid=(nb, npages // pps),
            in_specs=[pl.BlockSpec((1, NQ, d), per_b)] + pool_specs + pool_specs
            + [pl.BlockSpec((1, NQ * H, HD), per_b),
               pl.BlockSpec((1, NQ * H, HD), per_b),
               pl.BlockSpec((1, pps, 1, page * H), lambda b, p, pt: (b, p, 0, 0)),
               pl.BlockSpec((1, 1, NQ * H), per_b),
               pl.BlockSpec((NQ, d), lambda b, p, pt: (gate_rb0 + b, 3))],
            out_specs=pl.BlockSpec((NQ, d), lambda b, p, pt: (b, 0)),
            scratch_shapes=[pltpu.VMEM((rows, HD), F32),
                            pltpu.VMEM((rows, kc), F32),
                            pltpu.VMEM((rows, 1), F32),
                            pltpu.VMEM((rows, 2 * HD), F32),
                            pltpu.VMEM((rows, pps * page * H), F32),
                            pltpu.VMEM((pps, page * H, 2 * HD), BF16)],
        ),
        out_shape=jax.ShapeDtypeStruct((nb * NQ, d), F32),
        compiler_params=_cparams(("parallel", "arbitrary")),
        name="fox_decode_attn",
    )(page_table_flat, q_s, *([cache_k] * pps), *([cache_v] * pps), kn, vn, cum, cnew, gate_src)


def _pad_lanes(v, start=0):
    out = jnp.zeros((1, LANES), F32)
    return lax.dynamic_update_slice(out, v.reshape(1, -1).astype(F32), (0, start))


def kernel(x_prompt, x_sample, state_delta, state_conv, cache_k, cache_v, cache_logf, page_table,
           norm_mix, norm_mlp, a_w_in, a_conv, a_A_log, a_dt_bias, a_out_norm, a_w_out,
           b_w_in, b_f_bias, b_q_norm, b_k_norm, b_w_out, w_up, w_down):
    bp, t, d = x_prompt.shape
    bs, ts, _ = x_sample.shape
    depth = norm_mix.shape[0]
    n_mixers = 2
    H = state_delta.shape[2]
    DK = state_delta.shape[3]
    nqkv = state_conv.shape[-1]
    HB = cache_k.shape[3]
    HD = cache_k.shape[4]
    n_pool, page = cache_k.shape[1], cache_k.shape[2]
    npages = page_table.shape[1]
    d_ff = w_up.shape[2]
    mp, ms = bp * t, bs * ts
    assert H * DK == d and HB * HD == d and nqkv == 3 * d and DK == LANES and HD == LANES
    assert 2 * H <= LANES and HB <= LANES and 3 <= ts <= page and t >= 3

    x = jnp.concatenate([x_prompt.reshape(mp, d), x_sample.reshape(ms, d)], axis=0)
    pt_flat = page_table.reshape(-1).astype(jnp.int32)
    a_w_in_t = jnp.swapaxes(a_w_in, 1, 2)
    b_w_in_t = jnp.swapaxes(b_w_in, 1, 2)
    pad_rows = lambda w: jnp.pad(w, ((0, 0), (0, LANES - w.shape[1]), (0, 0)))
    a_w_gate_t = pad_rows(a_w_in_t[:, nqkv + d:, :])
    b_w_gate_t = pad_rows(b_w_in_t[:, 4 * d:, :])
    w_down_bf = w_down.astype(BF16)
    logf_hp = jnp.swapaxes(cache_logf, 2, 3).reshape(-1, HB, page)

    p_delta, p_conv, s_delta, s_conv = [], [], [], []
    p_k, p_v, p_lf, s_k, s_v, s_lf = [], [], [], [], [], []
    for i in range(depth):
        j = i // n_mixers
        hn = rmsnorm_bf16(x, norm_mix[i])
        if i % n_mixers == 0:
            proj = matmul_wcast(hn, a_w_in_t, j, n_out=nqkv + d, w_transposed=True)
            gates = matmul_wcast(hn, a_w_gate_t, j, n_out=LANES, w_transposed=True)
            alog_pad = _pad_lanes(a_A_log[j], H)
            dtb_pad = _pad_lanes(a_dt_bias[j], H)
            common = dict(H=H, DK=DK)
            cp = 64 if t % 64 == 0 else t
            yp, stp = gdn_core(proj, gates, a_conv[j], alog_pad, dtb_pad, a_out_norm[j], None, None,
                               row0=0, nb=bp, t=t, C=cp, G=max(1, min(H, MXU_DIM // cp)),
                               out_dtype=BF16, **common)
            ys, sts = gdn_core(proj, gates, a_conv[j], alog_pad, dtb_pad, a_out_norm[j],
                               state_conv[j], state_delta[j],
                               row0=mp, nb=bs, t=ts, C=ts, G=H, out_dtype=F32, **common)
            x = matmul_wcast(yp, a_w_out, j, n_out=d, epilogue="residual", residual=x, row0=0)
            x = matmul_wcast(ys.astype(BF16), a_w_out, j, n_out=d, epilogue="residual", residual=x, row0=mp)
            p_conv.append(jnp.stack([proj[b * t + t - 3:(b + 1) * t, :nqkv] for b in range(bp)]))
            s_conv.append(proj[mp:].reshape(bs, ts, -1)[:, ts - 3:, :nqkv])
            p_delta.append(stp)
            s_delta.append(sts)
        else:
            proj = matmul_wcast(hn, b_w_in_t, j, n_out=4 * d, w_transposed=True)
            fproj = matmul_wcast(hn, b_w_gate_t, j, n_out=LANES, w_transposed=True)
            qb, kf, kb, vb, lf = fox_prep(proj, fproj, b_q_norm[j], b_k_norm[j],
                                          _pad_lanes(b_f_bias[j]), H=HB, HD=HD)
            c_p = cumsum_rows(lf[:mp].reshape(bp, t, LANES))
            c_row = jnp.transpose(c_p[:, :, :HB], (0, 2, 1)).reshape(bp, HB, 1, t)
            op = fox_prompt_attention(qb, kb, vb, c_row, proj, nb=bp, t=t, H=HB, HD=HD)
            lf_new_hp = jnp.swapaxes(lf[mp:, :HB].reshape(bs, ts, HB), 1, 2)
            lf_new_hp = jnp.pad(lf_new_hp, ((0, 0), (0, 0), (0, page - ts)))
            cum_hp, cnew_hp = paged_cumsum(pt_flat, logf_hp, lf_new_hp, layer_off=j * n_pool,
                                           nb=bs, npages=npages, page=page, H=HB)
            cum = jnp.swapaxes(cum_hp, 2, 3).reshape(bs, npages, 1, page * HB)
            cnew = jnp.swapaxes(cnew_hp[:, :, :ts], 1, 2).reshape(bs, 1, ts * HB)
            os_ = fox_decode_attention(
                pt_flat, qb[mp:].reshape(bs, ts, d),
                cache_k.reshape(-1, page * HB, HD), cache_v.reshape(-1, page * HB, HD),
                kb[mp:].reshape(bs, ts * HB, HD), vb[mp:].reshape(bs, ts * HB, HD),
                cum, cnew, proj,
                layer_off=j * n_pool, nb=bs, npages=npages, page=page, H=HB, HD=HD, NQ=ts,
                gate_rb0=mp // ts)
            x = matmul_wcast(op, b_w_out, j, n_out=d, epilogue="residual", residual=x, row0=0)
            x = matmul_wcast(os_.astype(BF16), b_w_out, j, n_out=d, epilogue="residual", residual=x, row0=mp)
            p_k.append(kf[:mp].reshape(bp, t, HB, HD))
            p_v.append(proj[:mp, 2 * d:3 * d].reshape(bp, t, HB, HD))
            p_lf.append(lf[:mp, :HB].reshape(bp, t, HB))
            s_k.append(kf[mp:].reshape(bs, ts, HB, HD))
            s_v.append(proj[mp:, 2 * d:3 * d].reshape(bs, ts, HB, HD))
            s_lf.append(lf[mp:, :HB].reshape(bs, ts, HB))
        hm = rmsnorm_bf16(x, norm_mlp[i])
        a = matmul_wcast(hm, w_up, i, n_out=d_ff, epilogue="relu2", out_dtype=BF16)
        x = matmul_ktiled(a, w_down_bf, i, epilogue="residual", residual=x)
    return (x[:mp].reshape(bp, t, d), x[mp:].reshape(bs, ts, d),
            jnp.stack(p_delta), jnp.stack(p_conv), jnp.stack(p_k), jnp.stack(p_v), jnp.stack(p_lf),
            jnp.stack(s_delta), jnp.stack(s_conv), jnp.stack(s_k), jnp.stack(s_v), jnp.stack(s_lf))
```

```python
import functools

import jax
import jax.numpy as jnp
from jax import lax
from jax.experimental import pallas as pl
from jax.experimental.pallas import tpu as pltpu

F32 = jnp.float32
BF16 = jnp.bfloat16
EPS = 1e-6
NEG = -1e30
LOG2E = 1.4426950408889634
LANES = 128
SUBLANES = 8
MXU_DIM = 256
VMEM_LIMIT = 48 * 1024 * 1024

ROW_TILES = (768, 512, 528, 384, 256, 128, 64, 32, 16, 8)
COL_TILE = 1024
FULLK_COL_TILE = 256
ATTN_TQ = (512, 256, 128)
DECODE_PAGES_PER_STEP = (4, 2, 1)
DECODE_KEY_CHUNK = (256, 128)
CUMSUM_PAGES_PER_STEP = (32, 16, 8, 4, 2, 1)


def _pick(n, cands):
    for c in cands:
        if n % c == 0:
            return c
    return n


def _cparams(sem):
    return pltpu.CompilerParams(dimension_semantics=sem, vmem_limit_bytes=VMEM_LIMIT)


def _dot(a, b):
    return jnp.dot(a.astype(BF16), b.astype(BF16), preferred_element_type=F32)


def _dot_nt(a, b):
    return lax.dot_general(a.astype(BF16), b.astype(BF16), (((1,), (1,)), ((), ())),
                           preferred_element_type=F32)


def _split3(x):
    hi = x.astype(BF16)
    r = x - hi.astype(F32)
    mid = r.astype(BF16)
    lo = (r - mid.astype(F32)).astype(BF16)
    return hi, mid, lo


def _dot_exact_lhs01(a01, b):
    hi, mid, lo = _split3(b)
    a = a01.astype(BF16)
    d = functools.partial(jnp.dot, preferred_element_type=F32)
    return d(a, hi) + (d(a, mid) + d(a, lo))


def _sigmoid(x):
    return 1.0 / (1.0 + jnp.exp(-x))


def _softplus(x):
    return jnp.maximum(x, 0.0) + jnp.log(1.0 + jnp.exp(-jnp.abs(x)))


def _rmsnorm_kernel(x_ref, w_ref, o_ref):
    x = x_ref[...]
    ms = jnp.mean(x * x, axis=-1, keepdims=True)
    o_ref[...] = (x * lax.rsqrt(ms + EPS) * w_ref[...]).astype(o_ref.dtype)


def rmsnorm_bf16(x, w):
    m, d = x.shape
    tm = _pick(m, (256, 128, 64, 32, 16))
    return pl.pallas_call(
        _rmsnorm_kernel,
        grid=(m // tm,),
        in_specs=[pl.BlockSpec((tm, d), lambda i: (i, 0)),
                  pl.BlockSpec((1, d), lambda i: (0, 0))],
        out_specs=pl.BlockSpec((tm, d), lambda i: (i, 0)),
        out_shape=jax.ShapeDtypeStruct((m, d), BF16),
        compiler_params=_cparams(("parallel",)),
        name="rmsnorm",
    )(x, w.reshape(1, d))


def _epilogue(acc, kind, r_ref):
    if kind == "relu2":
        acc = jnp.maximum(acc, 0.0)
        acc = acc * acc
    elif kind == "residual":
        acc = acc + r_ref[...]
    return acc


def _mm_wcast_kernel(*refs, epilogue, w_transposed):
    if epilogue == "residual":
        a_ref, w_ref, r_ref, o_ref, wb_ref = refs
    else:
        a_ref, w_ref, o_ref, wb_ref = refs
        r_ref = None

    @pl.when(pl.program_id(1) == 0)
    def _():
        wb_ref[...] = w_ref[...].astype(wb_ref.dtype)

    if w_transposed:
        acc = _dot_nt(a_ref[...], wb_ref[...])
    else:
        acc = jnp.dot(a_ref[...], wb_ref[...], preferred_element_type=F32)
    o_ref[...] = _epilogue(acc, epilogue, r_ref).astype(o_ref.dtype)


def matmul_wcast(a, w, layer, *, n_out, w_transposed=False, epilogue="none", residual=None,
                 out_dtype=F32, row0=0):
    ma, kdim = a.shape
    tm = _pick(ma, ROW_TILES if epilogue != "residual" else (512, 256, 128, 64, 32, 16, 8))
    tn = min(COL_TILE, n_out)
    assert n_out % tn == 0 and row0 % tm == 0 and w.shape[2 if w_transposed else 1] == kdim
    r0 = row0 // tm
    if w_transposed:
        w_spec = pl.BlockSpec((None, tn, kdim), lambda j, i: (layer, j, 0))
        wb_shape = (tn, kdim)
    else:
        w_spec = pl.BlockSpec((None, kdim, tn), lambda j, i: (layer, 0, j))
        wb_shape = (kdim, tn)
    in_specs = [pl.BlockSpec((tm, kdim), lambda j, i: (i, 0)), w_spec]
    args = [a, w]
    aliases = {}
    if epilogue == "residual":
        in_specs.append(pl.BlockSpec((tm, tn), lambda j, i: (r0 + i, j)))
        args.append(residual)
        aliases = {2: 0}
        out_rows = residual.shape[0]
    else:
        out_rows = ma
    return pl.pallas_call(
        functools.partial(_mm_wcast_kernel, epilogue=epilogue, w_transposed=w_transposed),
        grid=(n_out // tn, ma // tm),
        in_specs=in_specs,
        out_specs=pl.BlockSpec((tm, tn), lambda j, i: (r0 + i, j)),
        out_shape=jax.ShapeDtypeStruct((out_rows, n_out), out_dtype),
        scratch_shapes=[pltpu.VMEM(wb_shape, BF16)],
        input_output_aliases=aliases,
        compiler_params=_cparams(("arbitrary", "arbitrary")),
        name="matmul_wcast_" + epilogue,
    )(*args)


def _mm_fullk_kernel(a_ref, b_ref, r_ref, o_ref):
    acc = jnp.dot(a_ref[...], b_ref[...], preferred_element_type=F32)
    o_ref[...] = (acc + r_ref[...]).astype(o_ref.dtype)


def matmul_fullk_residual(a, b, layer, residual, *, row0, rows):
    _, kdim = a.shape
    n = b.shape[2]
    tm = _pick(rows, ROW_TILES)
    tn = min(FULLK_COL_TILE, n)
    assert n % tn == 0 and row0 % tm == 0
    r0 = row0 // tm
    return pl.pallas_call(
        _mm_fullk_kernel,
        grid=(rows // tm, n // tn),
        in_specs=[pl.BlockSpec((tm, kdim), lambda i, j: (r0 + i, 0)),
                  pl.BlockSpec((None, kdim, tn), lambda i, j: (layer, 0, j)),
                  pl.BlockSpec((tm, tn), lambda i, j: (r0 + i, j))],
        out_specs=pl.BlockSpec((tm, tn), lambda i, j: (i, j)),
        out_shape=jax.ShapeDtypeStruct((rows, n), F32),
        compiler_params=_cparams(("parallel", "arbitrary")),
        name="matmul_fullk_residual",
    )(a, b, residual)


def _tri_inv_all(lmats, blk, eye, c):
    base = SUBLANES
    same8 = blk(0, base) == blk(1, base)
    lds = [jnp.where(same8, lm, 0.0) for lm in lmats]
    ld2 = [_dot(ld, ld) for ld in lds]
    ld4 = [_dot(a, a) for a in ld2]
    xs = [jnp.where(eye, 1.0, 0.0) - ld for ld in lds]
    xs = [x + _dot(x, a) for x, a in zip(xs, ld2)]
    xs = [x + _dot(x, a) for x, a in zip(xs, ld4)]
    m = base
    while m < c:
        sel = (blk(0, 2 * m) == blk(1, 2 * m)) & (blk(0, m) != blk(1, m))
        offx = [_dot(jnp.where(sel, lm, 0.0), x) for lm, x in zip(lmats, xs)]
        xs = [x - _dot(x, ox) for x, ox in zip(xs, offx)]
        m *= 2
    return xs


def _gdn_kernel(*refs, C, G, H, DK, has_cbuf, has_s0):
    it = iter(refs)
    qkv_ref = next(it)
    z_ref = next(it)
    gates_ref = next(it)
    cw_ref = next(it)
    alog_ref = next(it)
    dtb_ref = next(it)
    onw_ref = next(it)
    cbuf_ref = next(it) if has_cbuf else None
    s0_ref = next(it) if has_s0 else None
    y_ref = next(it)
    sfin_ref = next(it)
    ext_ref = next(it)
    scat_ref = next(it)

    c_idx = pl.program_id(1)
    nc = pl.num_programs(1)
    R = G * C
    NG = H // G
    NQK = H * DK
    pad = SUBLANES

    @pl.when(c_idx == 0)
    def _init():
        ext_ref[0:pad, :] = jnp.zeros((pad, ext_ref.shape[1]), F32)
        if has_cbuf:
            ext_ref[pad - 3:pad, :] = cbuf_ref[0]
        for h in range(H):
            if has_s0:
                scat_ref[:, h * DK:(h + 1) * DK] = s0_ref[0, h]
            else:
                scat_ref[:, h * DK:(h + 1) * DK] = jnp.zeros((DK, DK), F32)

    ext_ref[pad:pad + C, :] = qkv_ref[...]

    def conv_act(col):
        sl = slice(col, col + DK)
        acc = ext_ref[pad:pad + C, sl] * cw_ref[3:4, sl]
        for i in range(3):
            acc = acc + ext_ref[pad - 3 + i:pad - 3 + i + C, sl] * cw_ref[i:i + 1, sl]
        return acc * _sigmoid(acc)

    gt = gates_ref[...]
    beta_full = _sigmoid(gt)
    g_full = -jnp.exp(alog_ref[...]) * _softplus(gt + dtb_ref[...])

    iotas = (lax.broadcasted_iota(jnp.int32, (R, R), 0), lax.broadcasted_iota(jnp.int32, (R, R), 1))

    def blk(axis, m):
        return lax.shift_right_logical(iotas[axis], m.bit_length() - 1)

    same = blk(0, C) == blk(1, C)
    eye = iotas[0] == iotas[1]
    incl = same & (iotas[1] <= iotas[0])
    strict = same & (iotas[1] < iotas[0])
    incl_t = same & (iotas[0] <= iotas[1])
    rowblk = lax.shift_right_logical(lax.broadcasted_iota(jnp.int32, (R, DK), 0), C.bit_length() - 1)
    cat = (lambda xs: xs[0]) if G == 1 else (lambda xs: jnp.concatenate(xs, axis=0))
    groups = [list(range(gi * G, (gi + 1) * G)) for gi in range(NG)]

    Qs, Ks, Vs, betas, gcols = [], [], [], [], []
    for heads in groups:
        qs, ks, vs = [], [], []
        for h in heads:
            q = conv_act(h * DK)
            k = conv_act(NQK + h * DK)
            qs.append(q * lax.rsqrt(jnp.sum(q * q, axis=-1, keepdims=True) + EPS) * (DK ** -0.5))
            ks.append(k * lax.rsqrt(jnp.sum(k * k, axis=-1, keepdims=True) + EPS))
            vs.append(conv_act(2 * NQK + h * DK))
        Qs.append(cat(qs))
        Ks.append(cat(ks))
        Vs.append(cat(vs))
        betas.append(cat([beta_full[:, h:h + 1] for h in heads]))
        gcols.append(cat([g_full[:, H + h:H + h + 1] for h in heads]))

    gcs, gls, decays = [], [], []
    for g_c in gcols:
        g_r = jnp.sum(jnp.where(eye, g_c, 0.0), axis=0, keepdims=True)
        gc_c = jnp.sum(jnp.where(incl, g_r, 0.0), axis=1, keepdims=True)
        gc_r = jnp.sum(jnp.where(incl_t, g_c, 0.0), axis=0, keepdims=True)
        gls.append(jnp.sum(jnp.where(same, g_r, 0.0), axis=1, keepdims=True))
        gcs.append(gc_c)
        decays.append(jnp.exp(jnp.where(incl, gc_c - gc_r, NEG)))

    KBs = [k * b for k, b in zip(Ks, betas)]
    lmats = [jnp.where(strict, _dot_nt(kb, k) * d, 0.0) for kb, k, d in zip(KBs, Ks, decays)]
    amats = [jnp.where(incl, _dot_nt(q, k) * d, 0.0) for q, k, d in zip(Qs, Ks, decays)]
    tmats = _tri_inv_all(lmats, blk, eye, C)
    egs = [jnp.exp(gc) for gc in gcs]
    UWs = [_dot(t, jnp.concatenate([v * b, kb * eg], axis=1))
           for t, v, b, kb, eg in zip(tmats, Vs, betas, KBs, egs)]
    scats = [scat_ref[:, heads[0] * DK:(heads[-1] + 1) * DK] for heads in groups]
    WQSs = [_dot(jnp.concatenate([uw[:, DK:], q * eg], axis=0), sc)
            for uw, q, eg, sc in zip(UWs, Qs, egs, scats)]
    Vns, Ois = [], []
    for uw, wqs in zip(UWs, WQSs):
        vn, oi = [], []
        for j in range(G):
            rs = slice(j * C, (j + 1) * C)
            cs = slice(j * DK, (j + 1) * DK)
            vn.append(uw[rs, :DK] - wqs[rs, cs])
            oi.append(wqs[R + j * C:R + (j + 1) * C, cs])
        Vns.append(cat(vn))
        Ois.append(cat(oi))
    Os = [oi + _dot(a, vn) for oi, a, vn in zip(Ois, amats, Vns)]
    dSs = []
    for k, gl, gc, vn in zip(Ks, gls, gcs, Vns):
        ke = k * jnp.exp(gl - gc)
        vwide = jnp.concatenate([jnp.where(rowblk == j, vn, 0.0) for j in range(G)], axis=1)
        dSs.append(_dot(ke.T, vwide))
    for heads, sc, ds, gl, o_all in zip(groups, scats, dSs, gls, Os):
        for j, h in enumerate(heads):
            cs = slice(j * DK, (j + 1) * DK)
            e_last = jnp.exp(gl[j * C:j * C + 1, :])
            scat_ref[:, h * DK:(h + 1) * DK] = sc[:, cs] * e_last + ds[:, cs]
            o = o_all[j * C:(j + 1) * C]
            o = o * lax.rsqrt(jnp.mean(o * o, axis=-1, keepdims=True) + EPS) * onw_ref[...]
            zg = z_ref[:, h * DK:(h + 1) * DK]
            y_ref[:, h * DK:(h + 1) * DK] = (o * (zg * _sigmoid(zg))).astype(y_ref.dtype)

    carry = ext_ref[C:C + pad, :]
    ext_ref[0:pad, :] = carry

    @pl.when(c_idx == nc - 1)
    def _fin():
        for h in range(H):
            sfin_ref[0, h] = scat_ref[:, h * DK:(h + 1) * DK]


def gdn_core(proj, gates, conv_w, alog_pad, dtb_pad, out_norm, conv_buf, s0, *,
             row0, nb, t, C, G, H, DK, out_dtype):
    nqkv = 3 * H * DK
    nc = t // C
    assert t % C == 0 and row0 % C == 0 and H % G == 0 and C & (C - 1) == 0
    rb0 = row0 // C
    row_map = lambda b, c: (rb0 + b * nc + c, 0)
    has_cbuf = conv_buf is not None
    has_s0 = s0 is not None
    in_specs = [
        pl.BlockSpec((C, nqkv), row_map),
        pl.BlockSpec((C, H * DK), lambda b, c: (rb0 + b * nc + c, 3)),
        pl.BlockSpec((C, LANES), row_map),
        pl.BlockSpec((4, nqkv), lambda b, c: (0, 0)),
        pl.BlockSpec((1, LANES), lambda b, c: (0, 0)),
        pl.BlockSpec((1, LANES), lambda b, c: (0, 0)),
        pl.BlockSpec((1, DK), lambda b, c: (0, 0)),
    ]
    args = [proj, proj, gates, conv_w, alog_pad, dtb_pad, out_norm.reshape(1, DK)]
    if has_cbuf:
        in_specs.append(pl.BlockSpec((1, 3, nqkv), lambda b, c: (b, 0, 0)))
        args.append(conv_buf)
    if has_s0:
        in_specs.append(pl.BlockSpec((1, H, DK, DK), lambda b, c: (b, 0, 0, 0)))
        args.append(s0)
    y, sfin = pl.pallas_call(
        functools.partial(_gdn_kernel, C=C, G=G, H=H, DK=DK, has_cbuf=has_cbuf, has_s0=has_s0),
        grid=(nb, nc),
        in_specs=in_specs,
        out_specs=[pl.BlockSpec((C, H * DK), lambda b, c: (b * nc + c, 0)),
                   pl.BlockSpec((1, H, DK, DK), lambda b, c: (b, 0, 0, 0))],
        out_shape=[jax.ShapeDtypeStruct((nb * t, H * DK), out_dtype),
                   jax.ShapeDtypeStruct((nb, H, DK, DK), F32)],
        scratch_shapes=[pltpu.VMEM((C + SUBLANES, nqkv), F32),
                        pltpu.VMEM((DK, H * DK), F32)],
        compiler_params=_cparams(("parallel", "arbitrary")),
        name="gdn_core_c%d" % C,
    )(*args)
    return y, sfin


def _fox_prep_kernel(q_ref, k_ref, v_ref, f_ref, qn_ref, kn_ref, fb_ref,
                     qo_ref, ko_ref, kb_ref, vo_ref, vb_ref, lf_ref, *, H, HD):
    q_scale = (HD ** -0.5) * LOG2E
    for h in range(H):
        sl = slice(h * HD, (h + 1) * HD)
        q = q_ref[:, sl]
        q = q * lax.rsqrt(jnp.mean(q * q, axis=-1, keepdims=True) + EPS) * qn_ref[...]
        qo_ref[:, sl] = (q * q_scale).astype(qo_ref.dtype)
        k = k_ref[:, sl]
        k = k * lax.rsqrt(jnp.mean(k * k, axis=-1, keepdims=True) + EPS) * kn_ref[...]
        ko_ref[:, sl] = k
        kb_ref[:, sl] = k.astype(kb_ref.dtype)
    v = v_ref[...]
    vo_ref[...] = v
    vb_ref[...] = v.astype(vb_ref.dtype)
    x = f_ref[...] + fb_ref[...]
    lf_ref[...] = jnp.minimum(x, 0.0) - jnp.log1p(jnp.exp(-jnp.abs(x)))


def fox_prep(proj, fproj, q_norm, k_norm, fb_pad, *, row0, rows, H, HD):
    d = H * HD
    tm = _pick(rows, (256, 128, 64, 32, 16, 8))
    assert row0 % tm == 0
    r0 = row0 // tm
    col = lambda j: (lambda i: (r0 + i, j))
    out = lambda i: (i, 0)
    return pl.pallas_call(
        functools.partial(_fox_prep_kernel, H=H, HD=HD),
        grid=(rows // tm,),
        in_specs=[pl.BlockSpec((tm, d), col(0)), pl.BlockSpec((tm, d), col(1)),
                  pl.BlockSpec((tm, d), col(2)), pl.BlockSpec((tm, LANES), col(0)),
                  pl.BlockSpec((1, HD), lambda i: (0, 0)), pl.BlockSpec((1, HD), lambda i: (0, 0)),
                  pl.BlockSpec((1, LANES), lambda i: (0, 0))],
        out_specs=[pl.BlockSpec((tm, d), out)] * 5 + [pl.BlockSpec((tm, LANES), out)],
        out_shape=[jax.ShapeDtypeStruct((rows, d), BF16), jax.ShapeDtypeStruct((rows, d), F32),
                   jax.ShapeDtypeStruct((rows, d), BF16), jax.ShapeDtypeStruct((rows, d), F32),
                   jax.ShapeDtypeStruct((rows, d), BF16), jax.ShapeDtypeStruct((rows, LANES), F32)],
        compiler_params=_cparams(("parallel",)),
        name="fox_prep",
    )(proj, proj, proj, fproj, q_norm.reshape(1, HD), k_norm.reshape(1, HD), fb_pad)


def _cumsum_block(x, carry):
    n = x.shape[0]
    ri = lax.broadcasted_iota(jnp.int32, (n, n), 0)
    ci = lax.broadcasted_iota(jnp.int32, (n, n), 1)
    tril = jnp.where(ci <= ri, 1.0, 0.0)
    return _dot_exact_lhs01(tril, x) + carry


def _cumsum_kernel(x_ref, o_ref, carry_ref):
    @pl.when(pl.program_id(1) == 0)
    def _():
        carry_ref[...] = jnp.zeros_like(carry_ref)

    cs = _cumsum_block(x_ref[0], carry_ref[...])
    o_ref[0] = cs
    carry_ref[...] = cs[cs.shape[0] - 1:, :]


def cumsum_rows(x):
    b, t, w = x.shape
    tb = _pick(t, (256, 128, 64, 32, 16, 8))
    return pl.pallas_call(
        _cumsum_kernel,
        grid=(b, t // tb),
        in_specs=[pl.BlockSpec((1, tb, w), lambda i, j: (i, j, 0))],
        out_specs=pl.BlockSpec((1, tb, w), lambda i, j: (i, j, 0)),
        out_shape=jax.ShapeDtypeStruct((b, t, w), F32),
        scratch_shapes=[pltpu.VMEM((1, w), F32)],
        compiler_params=_cparams(("parallel", "arbitrary")),
        name="cumsum_rows",
    )(x)


def _cumsum_lanes(x, carry):
    n = x.shape[1]
    ri = lax.broadcasted_iota(jnp.int32, (n, n), 0)
    ci = lax.broadcasted_iota(jnp.int32, (n, n), 1)
    triu = jnp.where(ri <= ci, 1.0, 0.0).astype(BF16)
    hi, mid, lo = _split3(x)
    d = functools.partial(jnp.dot, preferred_element_type=F32)
    return d(hi, triu) + (d(mid, triu) + d(lo, triu)) + carry


def _paged_cumsum_kernel(pt_ref, *refs, pps):
    lp_refs = refs[:pps]
    ln_ref, o_ref, on_ref, carry_ref = refs[pps:]
    p = pl.program_id(1)

    @pl.when(p == 0)
    def _():
        carry_ref[...] = jnp.zeros_like(carry_ref)

    carry = carry_ref[...]
    for j in range(pps):
        cs = _cumsum_lanes(lp_refs[j][0], carry)
        o_ref[0, j] = cs
        carry = cs[:, cs.shape[1] - 1:]
    carry_ref[...] = carry

    @pl.when(p == pl.num_programs(1) - 1)
    def _():
        on_ref[0] = _cumsum_lanes(ln_ref[0], carry)


def paged_cumsum(page_table_flat, logf_hp, lf_new_hp, *, layer_off, nb, npages, page, H):
    pps = _pick(npages, CUMSUM_PAGES_PER_STEP)
    ts = lf_new_hp.shape[2]
    assert ts == page

    def pool_map(j):
        return lambda b, p, pt: (layer_off + pt[b * npages + p * pps + j], 0, 0)

    return pl.pallas_call(
        functools.partial(_paged_cumsum_kernel, pps=pps),
        grid_spec=pltpu.PrefetchScalarGridSpec(
            num_scalar_prefetch=1,
            grid=(nb, npages // pps),
            in_specs=[pl.BlockSpec((1, H, page), pool_map(j)) for j in range(pps)]
            + [pl.BlockSpec((1, H, ts), lambda b, p, pt: (b, 0, 0))],
            out_specs=[pl.BlockSpec((1, pps, H, page), lambda b, p, pt: (b, p, 0, 0)),
                       pl.BlockSpec((1, H, ts), lambda b, p, pt: (b, 0, 0))],
            scratch_shapes=[pltpu.VMEM((H, 1), F32)],
        ),
        out_shape=[jax.ShapeDtypeStruct((nb, npages, H, page), F32),
                   jax.ShapeDtypeStruct((nb, H, ts), F32)],
        compiler_params=_cparams(("parallel", "arbitrary")),
        name="paged_cumsum",
    )(page_table_flat, *([logf_hp] * pps), lf_new_hp)


def _fox_prompt_kernel(q_ref, k_ref, v_ref, c_ref, g_ref, o_ref, *, tq):
    qi = pl.program_id(2)
    q = q_ref[...]
    hd = q.shape[1]

    def block(j, carry, diag):
        m, l, acc = carry
        start = pl.multiple_of(j * tq, tq)
        k = k_ref[pl.ds(start, tq), :]
        v = v_ref[pl.ds(start, tq), :]
        s = _dot_nt(q, k) - c_ref[0, 0, :, pl.ds(start, tq)]
        if diag:
            rpos = lax.broadcasted_iota(jnp.int32, (tq, tq), 0)
            cpos = lax.broadcasted_iota(jnp.int32, (tq, tq), 1)
            s = jnp.where(cpos <= rpos, s, NEG)
        m_new = jnp.maximum(m, jnp.max(s, axis=1, keepdims=True))
        alpha = jnp.exp2(m - m_new)
        p = jnp.exp2(s - m_new)
        l = alpha * l + jnp.sum(p, axis=1, keepdims=True)
        acc = alpha * acc + _dot(p, v)
        return m_new, l, acc

    init = (jnp.full((tq, 1), NEG, F32), jnp.zeros((tq, 1), F32), jnp.zeros((tq, hd), F32))
    carry = lax.fori_loop(0, qi, lambda j, c: block(j, c, False), init)
    m, l, acc = block(qi, carry, True)
    g = g_ref[...]
    o_ref[...] = ((acc / l) * _sigmoid(g)).astype(o_ref.dtype)


def fox_prompt_attention(qb, kb, vb, c_row, proj, *, nb, t, H, HD):
    tq = _pick(t, ATTN_TQ)
    nq = t // tq
    return pl.pallas_call(
        functools.partial(_fox_prompt_kernel, tq=tq),
        grid=(nb, H, nq),
        in_specs=[pl.BlockSpec((tq, HD), lambda b, h, i: (b * nq + i, h)),
                  pl.BlockSpec((t, HD), lambda b, h, i: (b, h)),
                  pl.BlockSpec((t, HD), lambda b, h, i: (b, h)),
                  pl.BlockSpec((1, 1, 1, t), lambda b, h, i: (b, h, 0, 0)),
                  pl.BlockSpec((tq, HD), lambda b, h, i: (b * nq + i, 3 * H + h))],
        out_specs=pl.BlockSpec((tq, HD), lambda b, h, i: (b * nq + i, h)),
        out_shape=jax.ShapeDtypeStruct((nb * t, H * HD), BF16),
        compiler_params=_cparams(("parallel", "parallel", "arbitrary")),
        name="fox_prompt_attn",
    )(qb, kb, vb, c_row, proj)


def _fox_decode_kernel(pt_ref, *refs, pps, H, HD, NQ):
    q_ref = refs[0]
    kp_refs = refs[1:1 + pps]
    vp_refs = refs[1 + pps:1 + 2 * pps]
    (kn_ref, vn_ref, cum_ref, cn_ref, g_ref,
     o_ref, qcat_ref, mask_ref, m_ref, acc_ref, s_ref, vx_ref) = refs[1 + 2 * pps:]
    p = pl.program_id(1)
    rows = H * NQ
    kc = mask_ref.shape[1]
    width = kp_refs[0].shape[1]

    @pl.when(p == 0)
    def _():
        q = q_ref[0].astype(F32)
        for h in range(H):
            qcat_ref[h * NQ:(h + 1) * NQ, :] = q[:, h * HD:(h + 1) * HD]
        rh = lax.broadcasted_iota(jnp.int32, (rows, kc), 0) // NQ
        ch = lax.broadcasted_iota(jnp.int32, (rows, kc), 1) % H
        mask_ref[...] = jnp.where(rh == ch, 0.0, NEG)
        m_ref[...] = jnp.full_like(m_ref, NEG)
        acc_ref[...] = jnp.zeros_like(acc_ref)
        vx_ref[:, :, HD:] = jnp.ones((pps, width, HD), vx_ref.dtype)

    qc = qcat_ref[...].astype(BF16)

    m, acc = m_ref[...], acc_ref[...]
    for j in range(pps):
        vx_ref[j, :, :HD] = vp_refs[j][0].astype(vx_ref.dtype)
        mx = jnp.full((rows, kc), NEG, F32)
        for c0 in range(0, width, kc):
            s = _dot_nt(qc, kp_refs[j][0, c0:c0 + kc, :]) + (mask_ref[...] - cum_ref[0, j, :, c0:c0 + kc])
            s_ref[:, j * width + c0:j * width + c0 + kc] = s
            mx = jnp.maximum(mx, s)
        m_new = jnp.maximum(m, jnp.max(mx, axis=1, keepdims=True))
        pv = jnp.zeros((rows, 2 * HD), F32)
        for c0 in range(0, width, kc):
            pr = jnp.exp2(s_ref[:, j * width + c0:j * width + c0 + kc] - m_new)
            pv = pv + _dot(pr, vx_ref[j, c0:c0 + kc, :])
        acc = jnp.exp2(m - m_new) * acc + pv
        m = m_new
    m_ref[...], acc_ref[...] = m, acc

    @pl.when(p == pl.num_programs(1) - 1)
    def _():
        wn = kn_ref.shape[1]
        r = lax.broadcasted_iota(jnp.int32, (rows, wn), 0)
        c = lax.broadcasted_iota(jnp.int32, (rows, wn), 1)
        ok = (r // NQ == c % H) & (c // H <= r % NQ)
        s = jnp.where(ok, _dot_nt(qc, kn_ref[0]) - cn_ref[0], NEG)
        m_fin = jnp.maximum(m, jnp.max(s, axis=1, keepdims=True))
        vn = vn_ref[0]
        fin = (jnp.exp2(m - m_fin) * acc
               + _dot(jnp.exp2(s - m_fin), jnp.concatenate([vn, jnp.ones_like(vn)], axis=1)))
        out = fin[:, :HD] / fin[:, HD:]
        for h in range(H):
            cs = slice(h * HD, (h + 1) * HD)
            o_ref[:, cs] = (out[h * NQ:(h + 1) * NQ, :] * _sigmoid(g_ref[:, cs])).astype(o_ref.dtype)


def fox_decode_attention(page_table_flat, q_s, cache_k, cache_v, kn, vn, cum, cnew, gate_src, *,
                         layer_off, nb, npages, page, H, HD, NQ, gate_rb0):
    d = H * HD
    pps = _pick(npages, DECODE_PAGES_PER_STEP)
    rows = H * NQ
    kc = _pick(page * H, DECODE_KEY_CHUNK)
    assert kc % H == 0 and (page * H) % kc == 0

    def pool_map(j):
        return lambda b, p, pt: (layer_off + pt[b * npages + p * pps + j], 0, 0)

    per_b = lambda b, p, pt: (b, 0, 0)
    pool_specs = [pl.BlockSpec((1, page * H, HD), pool_map(j)) for j in range(pps)]
    return pl.pallas_call(
        functools.partial(_fox_decode_kernel, pps=pps, H=H, HD=HD, NQ=NQ),
        grid_spec=pltpu.PrefetchScalarGridSpec(
            num_scalar_prefetch=1,
            grid=(nb, npages // pps),
            in_specs=[pl.BlockSpec((1, NQ, d), per_b)] + pool_specs + pool_specs
            + [pl.BlockSpec((1, NQ * H, HD), per_b),
               pl.BlockSpec((1, NQ * H, HD), per_b),
               pl.BlockSpec((1, pps, 1, page * H), lambda b, p, pt: (b, p, 0, 0)),
               pl.BlockSpec((1, 1, NQ * H), per_b),
               pl.BlockSpec((NQ, d), lambda b, p, pt: (gate_rb0 + b, 3))],
            out_specs=pl.BlockSpec((NQ, d), lambda b, p, pt: (b, 0)),
            scratch_shapes=[pltpu.VMEM((rows, HD), F32),
                            pltpu.VMEM((rows, kc), F32),
                            pltpu.VMEM((rows, 1), F32),
                            pltpu.VMEM((rows, 2 * HD), F32),
                            pltpu.VMEM((rows, pps * page * H), F32),
                            pltpu.VMEM((pps, page * H, 2 * HD), BF16)],
        ),
        out_shape=jax.ShapeDtypeStruct((nb * NQ, d), F32),
        compiler_params=_cparams(("parallel", "arbitrary")),
        name="fox_decode_attn",
    )(page_table_flat, q_s, *([cache_k] * pps), *([cache_v] * pps), kn, vn, cum, cnew, gate_src)


def _pad_lanes(v, start=0):
    out = jnp.zeros((1, LANES), F32)
    return lax.dynamic_update_slice(out, v.reshape(1, -1).astype(F32), (0, start))


def kernel(x_prompt, x_sample, state_delta, state_conv, cache_k, cache_v, cache_logf, page_table,
           norm_mix, norm_mlp, a_w_in, a_conv, a_A_log, a_dt_bias, a_out_norm, a_w_out,
           b_w_in, b_f_bias, b_q_norm, b_k_norm, b_w_out, w_up, w_down):
    bp, t, d = x_prompt.shape
    bs, ts, _ = x_sample.shape
    depth = norm_mix.shape[0]
    n_mixers = 2
    H = state_delta.shape[2]
    DK = state_delta.shape[3]
    nqkv = state_conv.shape[-1]
    HB = cache_k.shape[3]
    HD = cache_k.shape[4]
    n_pool, page = cache_k.shape[1], cache_k.shape[2]
    npages = page_table.shape[1]
    d_ff = w_up.shape[2]
    mp, ms = bp * t, bs * ts
    assert H * DK == d and HB * HD == d and nqkv == 3 * d and DK == LANES and HD == LANES
    assert 2 * H <= LANES and HB <= LANES and 3 <= ts <= page and t >= 3

    x = jnp.concatenate([x_prompt.reshape(mp, d), x_sample.reshape(ms, d)], axis=0)
    pt_flat = page_table.reshape(-1).astype(jnp.int32)
    a_w_in_t = jnp.swapaxes(a_w_in, 1, 2)
    b_w_in_t = jnp.swapaxes(b_w_in, 1, 2)
    pad_rows = lambda w: jnp.pad(w, ((0, 0), (0, LANES - w.shape[1]), (0, 0)))
    a_w_gate_t = pad_rows(a_w_in_t[:, nqkv + d:, :])
    b_w_gate_t = pad_rows(b_w_in_t[:, 4 * d:, :])
    w_down_bf = w_down.astype(BF16)
    logf_hp = jnp.swapaxes(cache_logf, 2, 3).reshape(-1, HB, page)

    p_delta, p_conv, s_delta, s_conv = [], [], [], []
    p_k, p_v, p_lf, s_k, s_v, s_lf = [], [], [], [], [], []
    for i in range(depth):
        j = i // n_mixers
        hn = rmsnorm_bf16(x, norm_mix[i])
        if i % n_mixers == 0:
            proj = matmul_wcast(hn, a_w_in_t, j, n_out=nqkv + d, w_transposed=True)
            gates = matmul_wcast(hn, a_w_gate_t, j, n_out=LANES, w_transposed=True)
            alog_pad = _pad_lanes(a_A_log[j], H)
            dtb_pad = _pad_lanes(a_dt_bias[j], H)
            common = dict(H=H, DK=DK)
            cp = 64 if t % 64 == 0 else t
            yp, stp = gdn_core(proj, gates, a_conv[j], alog_pad, dtb_pad, a_out_norm[j], None, None,
                               row0=0, nb=bp, t=t, C=cp, G=max(1, min(H, MXU_DIM // cp)),
                               out_dtype=BF16, **common)
            ys, sts = gdn_core(proj, gates, a_conv[j], alog_pad, dtb_pad, a_out_norm[j],
                               state_conv[j], state_delta[j],
                               row0=mp, nb=bs, t=ts, C=ts, G=H, out_dtype=F32, **common)
            x = matmul_wcast(yp, a_w_out, j, n_out=d, epilogue="residual", residual=x, row0=0)
            x = matmul_wcast(ys.astype(BF16), a_w_out, j, n_out=d, epilogue="residual", residual=x, row0=mp)
            p_conv.append(jnp.stack([proj[b * t + t - 3:(b + 1) * t, :nqkv] for b in range(bp)]))
            s_conv.append(proj[mp:].reshape(bs, ts, -1)[:, ts - 3:, :nqkv])
            p_delta.append(stp)
            s_delta.append(sts)
        else:
            proj = matmul_wcast(hn, b_w_in_t, j, n_out=4 * d, w_transposed=True)
            fproj = matmul_wcast(hn, b_w_gate_t, j, n_out=LANES, w_transposed=True)
            prep = functools.partial(fox_prep, proj, fproj, b_q_norm[j], b_k_norm[j],
                                     _pad_lanes(b_f_bias[j]), H=HB, HD=HD)
            qb_p, kf_p, kb_p, vf_p, vb_p, lf_p = prep(row0=0, rows=mp)
            qb_s, kf_s, kb_s, vf_s, vb_s, lf_s = prep(row0=mp, rows=ms)
            c_p = cumsum_rows(lf_p.reshape(bp, t, LANES))
            c_row = jnp.transpose(c_p[:, :, :HB], (0, 2, 1)).reshape(bp, HB, 1, t) * LOG2E
            op = fox_prompt_attention(qb_p, kb_p, vb_p, c_row, proj, nb=bp, t=t, H=HB, HD=HD)
            lf_new_hp = jnp.swapaxes(lf_s[:, :HB].reshape(bs, ts, HB), 1, 2)
            lf_new_hp = jnp.pad(lf_new_hp, ((0, 0), (0, 0), (0, page - ts)))
            cum_hp, cnew_hp = paged_cumsum(pt_flat, logf_hp, lf_new_hp, layer_off=j * n_pool,
                                           nb=bs, npages=npages, page=page, H=HB)
            cum = jnp.swapaxes(cum_hp, 2, 3).reshape(bs, npages, 1, page * HB) * LOG2E
            cnew = jnp.swapaxes(cnew_hp[:, :, :ts], 1, 2).reshape(bs, 1, ts * HB) * LOG2E
            os_ = fox_decode_attention(
                pt_flat, qb_s.reshape(bs, ts, d),
                cache_k.reshape(-1, page * HB, HD), cache_v.reshape(-1, page * HB, HD),
                kb_s.reshape(bs, ts * HB, HD), vb_s.reshape(bs, ts * HB, HD),
                cum, cnew, proj,
                layer_off=j * n_pool, nb=bs, npages=npages, page=page, H=HB, HD=HD, NQ=ts,
                gate_rb0=mp // ts)
            x = matmul_wcast(op, b_w_out, j, n_out=d, epilogue="residual", residual=x, row0=0)
            x = matmul_wcast(os_.astype(BF16), b_w_out, j, n_out=d, epilogue="residual", residual=x, row0=mp)
            p_k.append(kf_p.reshape(bp, t, HB, HD))
            p_v.append(vf_p.reshape(bp, t, HB, HD))
            p_lf.append(lf_p[:, :HB].reshape(bp, t, HB))
            s_k.append(kf_s.reshape(bs, ts, HB, HD))
            s_v.append(vf_s.reshape(bs, ts, HB, HD))
            s_lf.append(lf_s[:, :HB].reshape(bs, ts, HB))
        hm = rmsnorm_bf16(x, norm_mlp[i])
        a = matmul_wcast(hm, w_up, i, n_out=d_ff, epilogue="relu2", out_dtype=BF16)
        down = functools.partial(matmul_fullk_residual, a, w_down_bf, i, x)
        if i + 1 < depth:
            x = down(row0=0, rows=mp + ms)
    y_p, y_s = down(row0=0, rows=mp), down(row0=mp, rows=ms)
    return (y_p.reshape(bp, t, d), y_s.reshape(bs, ts, d),
            jnp.stack(p_delta), jnp.stack(p_conv), jnp.stack(p_k), jnp.stack(p_v), jnp.stack(p_lf),
            jnp.stack(s_delta), jnp.stack(s_conv), jnp.stack(s_k), jnp.stack(s_v), jnp.stack(s_lf))
```

```python
import functools

import jax
import jax.numpy as jnp
from jax import lax
from jax.experimental import pallas as pl
from jax.experimental.pallas import tpu as pltpu

F32 = jnp.float32
BF16 = jnp.bfloat16
EPS = 1e-6
NEG = -1e30
LOG2E = 1.4426950408889634
LANES = 128
SUBLANES = 8
MXU_DIM = 256
VMEM_LIMIT = 48 * 1024 * 1024

ROW_TILES = (768, 512, 528, 384, 256, 128, 64, 32, 16, 8)
COL_TILE = 1024
FULLK_COL_TILE = 256
ATTN_TQ = (512, 256, 128)
DECODE_PAGES_PER_STEP = (4, 2, 1)
DECODE_KEY_CHUNK = (256, 128)
CUMSUM_PAGES_PER_STEP = (32, 16, 8, 4, 2, 1)

def _pick(n, cands):
    for c in cands:
        if n % c == 0:
            return c
    return n


def _cparams(sem):
    return pltpu.CompilerParams(dimension_semantics=sem, vmem_limit_bytes=VMEM_LIMIT)


def _dot(a, b):
    return jnp.dot(a.astype(BF16), b.astype(BF16), preferred_element_type=F32)


def _dot_nt(a, b):
    return lax.dot_general(a.astype(BF16), b.astype(BF16), (((1,), (1,)), ((), ())),
                           preferred_element_type=F32)


def _split3(x):
    hi = x.astype(BF16)
    r = x - hi.astype(F32)
    mid = r.astype(BF16)
    lo = (r - mid.astype(F32)).astype(BF16)
    return hi, mid, lo


def _dot_exact_lhs01(a01, b):
    hi, mid, lo = _split3(b)
    a = a01.astype(BF16)
    d = functools.partial(jnp.dot, preferred_element_type=F32)
    return d(a, hi) + (d(a, mid) + d(a, lo))


def _sigmoid(x):
    return 1.0 / (1.0 + jnp.exp(-x))


def _softplus(x):
    return jnp.maximum(x, 0.0) + jnp.log(1.0 + jnp.exp(-jnp.abs(x)))


def _rmsnorm_kernel(x_ref, w_ref, o_ref):
    x = x_ref[...]
    ms = jnp.mean(x * x, axis=-1, keepdims=True)
    o_ref[...] = (x * lax.rsqrt(ms + EPS) * w_ref[...]).astype(o_ref.dtype)


def rmsnorm_bf16(x, w):
    m, d = x.shape
    tm = _pick(m, (256, 128, 64, 32, 16))
    return pl.pallas_call(
        _rmsnorm_kernel,
        grid=(m // tm,),
        in_specs=[pl.BlockSpec((tm, d), lambda i: (i, 0)),
                  pl.BlockSpec((1, d), lambda i: (0, 0))],
        out_specs=pl.BlockSpec((tm, d), lambda i: (i, 0)),
        out_shape=jax.ShapeDtypeStruct((m, d), BF16),
        compiler_params=_cparams(("parallel",)),
        name="rmsnorm",
    )(x, w.reshape(1, d))


def _epilogue(acc, kind, r_ref):
    if kind == "relu2":
        acc = jnp.maximum(acc, 0.0)
        acc = acc * acc
    elif kind == "residual":
        acc = acc + r_ref[...]
    return acc


def _mm_wcast_kernel(*refs, epilogue, w_transposed):
    if epilogue == "residual":
        a_ref, w_ref, r_ref, o_ref, wb_ref = refs
    else:
        a_ref, w_ref, o_ref, wb_ref = refs
        r_ref = None

    @pl.when(pl.program_id(1) == 0)
    def _():
        wb_ref[...] = w_ref[...].astype(wb_ref.dtype)

    if w_transposed:
        acc = _dot_nt(a_ref[...], wb_ref[...])
    else:
        acc = jnp.dot(a_ref[...], wb_ref[...], preferred_element_type=F32)
    o_ref[...] = _epilogue(acc, epilogue, r_ref).astype(o_ref.dtype)


def matmul_wcast(a, w, layer, *, n_out, w_transposed=False, epilogue="none", residual=None,
                 out_dtype=F32, row0=0):
    ma, kdim = a.shape
    tm = _pick(ma, ROW_TILES if epilogue != "residual" else (512, 256, 128, 64, 32, 16, 8))
    tn = min(COL_TILE, n_out)
    assert n_out % tn == 0 and row0 % tm == 0 and w.shape[2 if w_transposed else 1] == kdim
    r0 = row0 // tm
    if w_transposed:
        w_spec = pl.BlockSpec((None, tn, kdim), lambda j, i: (layer, j, 0))
        wb_shape = (tn, kdim)
    else:
        w_spec = pl.BlockSpec((None, kdim, tn), lambda j, i: (layer, 0, j))
        wb_shape = (kdim, tn)
    in_specs = [pl.BlockSpec((tm, kdim), lambda j, i: (i, 0)), w_spec]
    args = [a, w]
    aliases = {}
    if epilogue == "residual":
        in_specs.append(pl.BlockSpec((tm, tn), lambda j, i: (r0 + i, j)))
        args.append(residual)
        aliases = {2: 0}
        out_rows = residual.shape[0]
    else:
        out_rows = ma
    return pl.pallas_call(
        functools.partial(_mm_wcast_kernel, epilogue=epilogue, w_transposed=w_transposed),
        grid=(n_out // tn, ma // tm),
        in_specs=in_specs,
        out_specs=pl.BlockSpec((tm, tn), lambda j, i: (r0 + i, j)),
        out_shape=jax.ShapeDtypeStruct((out_rows, n_out), out_dtype),
        scratch_shapes=[pltpu.VMEM(wb_shape, BF16)],
        input_output_aliases=aliases,
        compiler_params=_cparams(("arbitrary", "arbitrary")),
        name="matmul_wcast_" + epilogue,
    )(*args)


def _mm_fullk_kernel(a_ref, b_ref, r_ref, o_ref):
    acc = jnp.dot(a_ref[...], b_ref[...], preferred_element_type=F32)
    o_ref[...] = (acc + r_ref[...]).astype(o_ref.dtype)


def matmul_fullk_residual(a, b, layer, residual, *, row0, rows):
    _, kdim = a.shape
    n = b.shape[2]
    tm = _pick(rows, ROW_TILES)
    tn = min(FULLK_COL_TILE, n)
    assert n % tn == 0 and row0 % tm == 0
    r0 = row0 // tm
    return pl.pallas_call(
        _mm_fullk_kernel,
        grid=(rows // tm, n // tn),
        in_specs=[pl.BlockSpec((tm, kdim), lambda i, j: (r0 + i, 0)),
                  pl.BlockSpec((None, kdim, tn), lambda i, j: (layer, 0, j)),
                  pl.BlockSpec((tm, tn), lambda i, j: (r0 + i, j))],
        out_specs=pl.BlockSpec((tm, tn), lambda i, j: (i, j)),
        out_shape=jax.ShapeDtypeStruct((rows, n), F32),
        compiler_params=_cparams(("parallel", "arbitrary")),
        name="matmul_fullk_residual",
    )(a, b, residual)


def _gdn_kernel(*refs, C, G, H, DK, has_cbuf, has_s0):
    it = iter(refs)
    qkv_ref = next(it)
    z_ref = next(it)
    gates_ref = next(it)
    cw_ref = next(it)
    alog_ref = next(it)
    dtb_ref = next(it)
    onw_ref = next(it)
    cbuf_ref = next(it) if has_cbuf else None
    s0_ref = next(it) if has_s0 else None
    y_ref = next(it)
    sfin_ref = next(it)
    ext_ref = next(it)
    scat_ref = next(it)

    c_idx = pl.program_id(1)
    nc = pl.num_programs(1)
    R = G * C
    NG = H // G
    NQK = H * DK
    pad = SUBLANES

    @pl.when(c_idx == 0)
    def _init():
        ext_ref[0:pad, :] = jnp.zeros((pad, ext_ref.shape[1]), F32)
        if has_cbuf:
            ext_ref[pad - 3:pad, :] = cbuf_ref[0]
        for h in range(H):
            if has_s0:
                scat_ref[:, h * DK:(h + 1) * DK] = s0_ref[0, h]
            else:
                scat_ref[:, h * DK:(h + 1) * DK] = jnp.zeros((DK, DK), F32)

    ext_ref[pad:pad + C, :] = qkv_ref[...]

    def conv_act(col):
        sl = slice(col, col + DK)
        acc = ext_ref[pad:pad + C, sl] * cw_ref[3:4, sl]
        for i in range(3):
            acc = acc + ext_ref[pad - 3 + i:pad - 3 + i + C, sl] * cw_ref[i:i + 1, sl]
        return acc * _sigmoid(acc)

    gt = gates_ref[...]
    beta_full = _sigmoid(gt)
    g_full = -jnp.exp(alog_ref[...]) * _softplus(gt + dtb_ref[...])

    iotas = (lax.broadcasted_iota(jnp.int32, (R, R), 0), lax.broadcasted_iota(jnp.int32, (R, R), 1))

    def blk(axis, m):
        return lax.shift_right_logical(iotas[axis], m.bit_length() - 1)

    same = blk(0, C) == blk(1, C)
    eye = iotas[0] == iotas[1]
    incl = same & (iotas[1] <= iotas[0])
    strict = same & (iotas[1] < iotas[0])
    incl_t = same & (iotas[0] <= iotas[1])
    rowblk = lax.shift_right_logical(lax.broadcasted_iota(jnp.int32, (R, DK), 0), C.bit_length() - 1)
    cat = (lambda xs: xs[0]) if G == 1 else (lambda xs: jnp.concatenate(xs, axis=0))
    groups = [list(range(gi * G, (gi + 1) * G)) for gi in range(NG)]

    def group_program(heads):
        qs, ks, vs = [], [], []
        for h in heads:
            q = conv_act(h * DK)
            k = conv_act(NQK + h * DK)
            qs.append(q * lax.rsqrt(jnp.sum(q * q, axis=-1, keepdims=True) + EPS) * (DK ** -0.5))
            ks.append(k * lax.rsqrt(jnp.sum(k * k, axis=-1, keepdims=True) + EPS))
            vs.append(conv_act(2 * NQK + h * DK))
        Q, K, V = cat(qs), cat(ks), cat(vs)
        beta = cat([beta_full[:, h:h + 1] for h in heads])
        g_c = cat([g_full[:, H + h:H + h + 1] for h in heads])
        yield
        g_r = jnp.sum(jnp.where(eye, g_c, 0.0), axis=0, keepdims=True)
        gc = jnp.sum(jnp.where(incl, g_r, 0.0), axis=1, keepdims=True)
        gc_r = jnp.sum(jnp.where(incl_t, g_c, 0.0), axis=0, keepdims=True)
        gl = jnp.sum(jnp.where(same, g_r, 0.0), axis=1, keepdims=True)
        decay = jnp.exp(jnp.where(incl, gc - gc_r, NEG))
        KB = K * beta
        lmat = jnp.where(strict, _dot_nt(KB, K) * decay, 0.0)
        amat = jnp.where(incl, _dot_nt(Q, K) * decay, 0.0)
        yield
        ld = jnp.where(blk(0, SUBLANES) == blk(1, SUBLANES), lmat, 0.0)
        ld2 = _dot(ld, ld)
        yield
        ld4 = _dot(ld2, ld2)
        x = jnp.where(eye, 1.0, 0.0) - ld
        x = x + _dot(x, ld2)
        yield
        x = x + _dot(x, ld4)
        yield
        m = SUBLANES
        while m < C:
            sel = (blk(0, 2 * m) == blk(1, 2 * m)) & (blk(0, m) != blk(1, m))
            ox = _dot(jnp.where(sel, lmat, 0.0), x)
            yield
            x = x - _dot(x, ox)
            yield
            m *= 2
        eg = jnp.exp(gc)
        UW = _dot(x, jnp.concatenate([V * beta, KB * eg], axis=1))
        yield
        sc = scat_ref[:, heads[0] * DK:(heads[-1] + 1) * DK]
        WQS = _dot(jnp.concatenate([UW[:, DK:], Q * eg], axis=0), sc)
        yield
        vn, oi = [], []
        for j in range(G):
            rs = slice(j * C, (j + 1) * C)
            cs = slice(j * DK, (j + 1) * DK)
            vn.append(UW[rs, :DK] - WQS[rs, cs])
            oi.append(WQS[R + j * C:R + (j + 1) * C, cs])
        Vn = cat(vn)
        O = cat(oi) + _dot(amat, Vn)
        ke = K * jnp.exp(gl - gc)
        vwide = jnp.concatenate([jnp.where(rowblk == j, Vn, 0.0) for j in range(G)], axis=1)
        dS = _dot(ke.T, vwide)
        yield
        for j, h in enumerate(heads):
            cs = slice(j * DK, (j + 1) * DK)
            e_last = jnp.exp(gl[j * C:j * C + 1, :])
            scat_ref[:, h * DK:(h + 1) * DK] = sc[:, cs] * e_last + dS[:, cs]
            o = O[j * C:(j + 1) * C]
            o = o * lax.rsqrt(jnp.mean(o * o, axis=-1, keepdims=True) + EPS) * onw_ref[...]
            zg = z_ref[:, h * DK:(h + 1) * DK]
            y_ref[:, h * DK:(h + 1) * DK] = (o * (zg * _sigmoid(zg))).astype(y_ref.dtype)

    programs = [group_program(heads) for heads in groups]
    while programs:
        running = []
        for prog in programs:
            try:
                next(prog)
                running.append(prog)
            except StopIteration:
                pass
        programs = running

    carry = ext_ref[C:C + pad, :]
    ext_ref[0:pad, :] = carry

    @pl.when(c_idx == nc - 1)
    def _fin():
        for h in range(H):
            sfin_ref[0, h] = scat_ref[:, h * DK:(h + 1) * DK]


def gdn_core(proj, gates, conv_w, alog_pad, dtb_pad, out_norm, conv_buf, s0, *,
             row0, nb, t, C, G, H, DK, out_dtype):
    nqkv = 3 * H * DK
    nc = t // C
    assert t % C == 0 and row0 % C == 0 and H % G == 0 and C & (C - 1) == 0
    rb0 = row0 // C
    row_map = lambda b, c: (rb0 + b * nc + c, 0)
    has_cbuf = conv_buf is not None
    has_s0 = s0 is not None
    in_specs = [
        pl.BlockSpec((C, nqkv), row_map),
        pl.BlockSpec((C, H * DK), lambda b, c: (rb0 + b * nc + c, 3)),
        pl.BlockSpec((C, LANES), row_map),
        pl.BlockSpec((4, nqkv), lambda b, c: (0, 0)),
        pl.BlockSpec((1, LANES), lambda b, c: (0, 0)),
        pl.BlockSpec((1, LANES), lambda b, c: (0, 0)),
        pl.BlockSpec((1, DK), lambda b, c: (0, 0)),
    ]
    args = [proj, proj, gates, conv_w, alog_pad, dtb_pad, out_norm.reshape(1, DK)]
    if has_cbuf:
        in_specs.append(pl.BlockSpec((1, 3, nqkv), lambda b, c: (b, 0, 0)))
        args.append(conv_buf)
    if has_s0:
        in_specs.append(pl.BlockSpec((1, H, DK, DK), lambda b, c: (b, 0, 0, 0)))
        args.append(s0)
    y, sfin = pl.pallas_call(
        functools.partial(_gdn_kernel, C=C, G=G, H=H, DK=DK, has_cbuf=has_cbuf, has_s0=has_s0),
        grid=(nb, nc),
        in_specs=in_specs,
        out_specs=[pl.BlockSpec((C, H * DK), lambda b, c: (b * nc + c, 0)),
                   pl.BlockSpec((1, H, DK, DK), lambda b, c: (b, 0, 0, 0))],
        out_shape=[jax.ShapeDtypeStruct((nb * t, H * DK), out_dtype),
                   jax.ShapeDtypeStruct((nb, H, DK, DK), F32)],
        scratch_shapes=[pltpu.VMEM((C + SUBLANES, nqkv), F32),
                        pltpu.VMEM((DK, H * DK), F32)],
        compiler_params=_cparams(("parallel", "arbitrary")),
        name="gdn_core_c%d" % C,
    )(*args)
    return y, sfin


def _fox_prep_kernel(q_ref, k_ref, v_ref, f_ref, qn_ref, kn_ref, fb_ref,
                     qo_ref, ko_ref, kb_ref, vo_ref, vb_ref, lf_ref, *, H, HD):
    q_scale = (HD ** -0.5) * LOG2E
    for h in range(H):
        sl = slice(h * HD, (h + 1) * HD)
        q = q_ref[:, sl]
        q = q * lax.rsqrt(jnp.mean(q * q, axis=-1, keepdims=True) + EPS) * qn_ref[...]
        qo_ref[:, sl] = (q * q_scale).astype(qo_ref.dtype)
        k = k_ref[:, sl]
        k = k * lax.rsqrt(jnp.mean(k * k, axis=-1, keepdims=True) + EPS) * kn_ref[...]
        ko_ref[:, sl] = k
        kb_ref[:, sl] = k.astype(kb_ref.dtype)
    v = v_ref[...]
    vo_ref[...] = v
    vb_ref[...] = v.astype(vb_ref.dtype)
    x = f_ref[...] + fb_ref[...]
    lf_ref[...] = jnp.minimum(x, 0.0) - jnp.log1p(jnp.exp(-jnp.abs(x)))


def fox_prep(proj, fproj, q_norm, k_norm, fb_pad, *, row0, rows, H, HD):
    d = H * HD
    tm = _pick(rows, (256, 128, 64, 32, 16, 8))
    assert row0 % tm == 0
    r0 = row0 // tm
    col = lambda j: (lambda i: (r0 + i, j))
    out = lambda i: (i, 0)
    return pl.pallas_call(
        functools.partial(_fox_prep_kernel, H=H, HD=HD),
        grid=(rows // tm,),
        in_specs=[pl.BlockSpec((tm, d), col(0)), pl.BlockSpec((tm, d), col(1)),
                  pl.BlockSpec((tm, d), col(2)), pl.BlockSpec((tm, LANES), col(0)),
                  pl.BlockSpec((1, HD), lambda i: (0, 0)), pl.BlockSpec((1, HD), lambda i: (0, 0)),
                  pl.BlockSpec((1, LANES), lambda i: (0, 0))],
        out_specs=[pl.BlockSpec((tm, d), out)] * 5 + [pl.BlockSpec((tm, LANES), out)],
        out_shape=[jax.ShapeDtypeStruct((rows, d), BF16), jax.ShapeDtypeStruct((rows, d), F32),
                   jax.ShapeDtypeStruct((rows, d), BF16), jax.ShapeDtypeStruct((rows, d), F32),
                   jax.ShapeDtypeStruct((rows, d), BF16), jax.ShapeDtypeStruct((rows, LANES), F32)],
        compiler_params=_cparams(("parallel",)),
        name="fox_prep",
    )(proj, proj, proj, fproj, q_norm.reshape(1, HD), k_norm.reshape(1, HD), fb_pad)


def _cumsum_block(x, carry):
    n = x.shape[0]
    ri = lax.broadcasted_iota(jnp.int32, (n, n), 0)
    ci = lax.broadcasted_iota(jnp.int32, (n, n), 1)
    tril = jnp.where(ci <= ri, 1.0, 0.0)
    return _dot_exact_lhs01(tril, x) + carry


def _cumsum_kernel(x_ref, o_ref, carry_ref):
    @pl.when(pl.program_id(1) == 0)
    def _():
        carry_ref[...] = jnp.zeros_like(carry_ref)

    cs = _cumsum_block(x_ref[0], carry_ref[...])
    o_ref[0] = cs
    carry_ref[...] = cs[cs.shape[0] - 1:, :]


def cumsum_rows(x):
    b, t, w = x.shape
    tb = _pick(t, (256, 128, 64, 32, 16, 8))
    return pl.pallas_call(
        _cumsum_kernel,
        grid=(b, t // tb),
        in_specs=[pl.BlockSpec((1, tb, w), lambda i, j: (i, j, 0))],
        out_specs=pl.BlockSpec((1, tb, w), lambda i, j: (i, j, 0)),
        out_shape=jax.ShapeDtypeStruct((b, t, w), F32),
        scratch_shapes=[pltpu.VMEM((1, w), F32)],
        compiler_params=_cparams(("parallel", "arbitrary")),
        name="cumsum_rows",
    )(x)


def _cumsum_lanes(x, carry):
    n = x.shape[1]
    ri = lax.broadcasted_iota(jnp.int32, (n, n), 0)
    ci = lax.broadcasted_iota(jnp.int32, (n, n), 1)
    triu = jnp.where(ri <= ci, 1.0, 0.0).astype(BF16)
    hi, mid, lo = _split3(x)
    d = functools.partial(jnp.dot, preferred_element_type=F32)
    return d(hi, triu) + (d(mid, triu) + d(lo, triu)) + carry


def _paged_cumsum_kernel(pt_ref, *refs, pps):
    lp_refs = refs[:pps]
    ln_ref, o_ref, on_ref, carry_ref = refs[pps:]
    p = pl.program_id(1)

    @pl.when(p == 0)
    def _():
        carry_ref[...] = jnp.zeros_like(carry_ref)

    h = lp_refs[0].shape[1]
    local = _cumsum_lanes(jnp.concatenate([r[0] for r in lp_refs], axis=0) if pps > 1 else lp_refs[0][0], 0.0)
    carry = carry_ref[...]
    for j in range(pps):
        lj = local[j * h:(j + 1) * h]
        o_ref[0, j] = lj + carry
        carry = carry + lj[:, lj.shape[1] - 1:]
    carry_ref[...] = carry

    @pl.when(p == pl.num_programs(1) - 1)
    def _():
        on_ref[0] = _cumsum_lanes(ln_ref[0], carry)


def paged_cumsum(page_table_flat, logf_hp, lf_new_hp, *, layer_off, nb, npages, page, H):
    pps = _pick(npages, CUMSUM_PAGES_PER_STEP)
    ts = lf_new_hp.shape[2]
    assert ts == page

    def pool_map(j):
        return lambda b, p, pt: (layer_off + pt[b * npages + p * pps + j], 0, 0)

    return pl.pallas_call(
        functools.partial(_paged_cumsum_kernel, pps=pps),
        grid_spec=pltpu.PrefetchScalarGridSpec(
            num_scalar_prefetch=1,
            grid=(nb, npages // pps),
            in_specs=[pl.BlockSpec((1, H, page), pool_map(j)) for j in range(pps)]
            + [pl.BlockSpec((1, H, ts), lambda b, p, pt: (b, 0, 0))],
            out_specs=[pl.BlockSpec((1, pps, H, page), lambda b, p, pt: (b, p, 0, 0)),
                       pl.BlockSpec((1, H, ts), lambda b, p, pt: (b, 0, 0))],
            scratch_shapes=[pltpu.VMEM((H, 1), F32)],
        ),
        out_shape=[jax.ShapeDtypeStruct((nb, npages, H, page), F32),
                   jax.ShapeDtypeStruct((nb, H, ts), F32)],
        compiler_params=_cparams(("parallel", "arbitrary")),
        name="paged_cumsum",
    )(page_table_flat, *([logf_hp] * pps), lf_new_hp)


def _fox_prompt_kernel(q_ref, k_ref, v_ref, c_ref, g_ref, o_ref, *, tq):
    qi = pl.program_id(2)
    q = q_ref[...]
    hd = q.shape[1]

    def block(j, carry, diag):
        m, l, acc = carry
        start = pl.multiple_of(j * tq, tq)
        k = k_ref[pl.ds(start, tq), :]
        v = v_ref[pl.ds(start, tq), :]
        s = _dot_nt(q, k) - c_ref[0, 0, :, pl.ds(start, tq)]
        if diag:
            rpos = lax.broadcasted_iota(jnp.int32, (tq, tq), 0)
            cpos = lax.broadcasted_iota(jnp.int32, (tq, tq), 1)
            s = jnp.where(cpos <= rpos, s, NEG)
        m_new = jnp.maximum(m, jnp.max(s, axis=1, keepdims=True))
        alpha = jnp.exp2(m - m_new)
        p = jnp.exp2(s - m_new)
        l = alpha * l + jnp.sum(p, axis=1, keepdims=True)
        acc = alpha * acc + _dot(p, v)
        return m_new, l, acc

    init = (jnp.full((tq, 1), NEG, F32), jnp.zeros((tq, 1), F32), jnp.zeros((tq, hd), F32))
    carry = lax.fori_loop(0, qi, lambda j, c: block(j, c, False), init)
    m, l, acc = block(qi, carry, True)
    g = g_ref[...]
    o_ref[...] = ((acc / l) * _sigmoid(g)).astype(o_ref.dtype)


def fox_prompt_attention(qb, kb, vb, c_row, proj, *, nb, t, H, HD):
    tq = _pick(t, ATTN_TQ)
    nq = t // tq
    return pl.pallas_call(
        functools.partial(_fox_prompt_kernel, tq=tq),
        grid=(nb, H, nq),
        in_specs=[pl.BlockSpec((tq, HD), lambda b, h, i: (b * nq + i, h)),
                  pl.BlockSpec((t, HD), lambda b, h, i: (b, h)),
                  pl.BlockSpec((t, HD), lambda b, h, i: (b, h)),
                  pl.BlockSpec((1, 1, 1, t), lambda b, h, i: (b, h, 0, 0)),
                  pl.BlockSpec((tq, HD), lambda b, h, i: (b * nq + i, 3 * H + h))],
        out_specs=pl.BlockSpec((tq, HD), lambda b, h, i: (b * nq + i, h)),
        out_shape=jax.ShapeDtypeStruct((nb * t, H * HD), BF16),
        compiler_params=_cparams(("parallel", "parallel", "arbitrary")),
        name="fox_prompt_attn",
    )(qb, kb, vb, c_row, proj)


def _fox_decode_kernel(pt_ref, *refs, pps, H, HD, NQ):
    q_ref = refs[0]
    kp_refs = refs[1:1 + pps]
    vp_refs = refs[1 + pps:1 + 2 * pps]
    (kn_ref, vn_ref, cum_ref, cn_ref, g_ref,
     o_ref, qcat_ref, mask_ref, m_ref, acc_ref, s_ref, vx_ref) = refs[1 + 2 * pps:]
    p = pl.program_id(1)
    rows = H * NQ
    kc = mask_ref.shape[1]
    width = kp_refs[0].shape[1]

    @pl.when(p == 0)
    def _():
        q = q_ref[0].astype(F32)
        for h in range(H):
            qcat_ref[h * NQ:(h + 1) * NQ, :] = q[:, h * HD:(h + 1) * HD]
        rh = lax.broadcasted_iota(jnp.int32, (rows, kc), 0) // NQ
        ch = lax.broadcasted_iota(jnp.int32, (rows, kc), 1) % H
        mask_ref[...] = jnp.where(rh == ch, 0.0, NEG)
        m_ref[...] = jnp.full_like(m_ref, NEG)
        acc_ref[...] = jnp.zeros_like(acc_ref)
        vx_ref[:, :, HD:] = jnp.ones((pps, width, HD), vx_ref.dtype)

    qc = qcat_ref[...].astype(BF16)

    chunks = range(0, width, kc)

    def scores(j, c0, mx):
        s = _dot_nt(qc, kp_refs[j][0, c0:c0 + kc, :]) + (mask_ref[...] - cum_ref[0, j, :, c0:c0 + kc])
        s_ref[:, j * width + c0:j * width + c0 + kc] = s
        return jnp.maximum(mx, s)

    def weighted(j, c0, m_new, pv):
        pr = jnp.exp2(s_ref[:, j * width + c0:j * width + c0 + kc] - m_new)
        return pv + _dot(pr, vx_ref[j, c0:c0 + kc, :])

    m, acc = m_ref[...], acc_ref[...]
    neg = jnp.full((rows, kc), NEG, F32)
    vx_ref[0, :, :HD] = vp_refs[0][0].astype(vx_ref.dtype)
    mx = neg
    for c0 in chunks:
        mx = scores(0, c0, mx)
    for j in range(pps):
        m_new = jnp.maximum(m, jnp.max(mx, axis=1, keepdims=True))
        pv = jnp.zeros((rows, 2 * HD), F32)
        nxt = j + 1 < pps
        if nxt:
            vx_ref[j + 1, :, :HD] = vp_refs[j + 1][0].astype(vx_ref.dtype)
        mx = neg
        for c0 in chunks:
            pv = weighted(j, c0, m_new, pv)
            if nxt:
                mx = scores(j + 1, c0, mx)
        acc = jnp.exp2(m - m_new) * acc + pv
        m = m_new
    m_ref[...], acc_ref[...] = m, acc

    @pl.when(p == pl.num_programs(1) - 1)
    def _():
        wn = kn_ref.shape[1]
        r = lax.broadcasted_iota(jnp.int32, (rows, wn), 0)
        c = lax.broadcasted_iota(jnp.int32, (rows, wn), 1)
        ok = (r // NQ == c % H) & (c // H <= r % NQ)
        s = jnp.where(ok, _dot_nt(qc, kn_ref[0]) - cn_ref[0], NEG)
        m_fin = jnp.maximum(m, jnp.max(s, axis=1, keepdims=True))
        vn = vn_ref[0]
        fin = (jnp.exp2(m - m_fin) * acc
               + _dot(jnp.exp2(s - m_fin), jnp.concatenate([vn, jnp.ones_like(vn)], axis=1)))
        out = fin[:, :HD] / fin[:, HD:]
        for h in range(H):
            cs = slice(h * HD, (h + 1) * HD)
            o_ref[:, cs] = (out[h * NQ:(h + 1) * NQ, :] * _sigmoid(g_ref[:, cs])).astype(o_ref.dtype)


def fox_decode_attention(page_table_flat, q_s, cache_k, cache_v, kn, vn, cum, cnew, gate_src, *,
                         layer_off, nb, npages, page, H, HD, NQ, gate_rb0):
    d = H * HD
    pps = _pick(npages, DECODE_PAGES_PER_STEP)
    rows = H * NQ
    kc = _pick(page * H, DECODE_KEY_CHUNK)
    assert kc % H == 0 and (page * H) % kc == 0

    def pool_map(j):
        return lambda b, p, pt: (layer_off + pt[b * npages + p * pps + j], 0, 0)

    per_b = lambda b, p, pt: (b, 0, 0)
    pool_specs = [pl.BlockSpec((1, page * H, HD), pool_map(j)) for j in range(pps)]
    return pl.pallas_call(
        functools.partial(_fox_decode_kernel, pps=pps, H=H, HD=HD, NQ=NQ),
        grid_spec=pltpu.PrefetchScalarGridSpec(
            num_scalar_prefetch=1,
            grid=(nb, npages // pps),
            in_specs=[pl.BlockSpec((1, NQ, d), per_b)] + pool_specs + pool_specs
            + [pl.BlockSpec((1, NQ * H, HD), per_b),
               pl.BlockSpec((1, NQ * H, HD), per_b),
               pl.BlockSpec((1, pps, 1, page * H), lambda b, p, pt: (b, p, 0, 0)),
               pl.BlockSpec((1, 1, NQ * H), per_b),
               pl.BlockSpec((NQ, d), lambda b, p, pt: (gate_rb0 + b, 3))],
            out_specs=pl.BlockSpec((NQ, d), lambda b, p, pt: (b, 0)),
            scratch_shapes=[pltpu.VMEM((rows, HD), F32),
                            pltpu.VMEM((rows, kc), F32),
                            pltpu.VMEM((rows, 1), F32),
                            pltpu.VMEM((rows, 2 * HD), F32),
                            pltpu.VMEM((rows, pps * page * H), F32),
                            pltpu.VMEM((pps, page * H, 2 * HD), BF16)],
        ),
        out_shape=jax.ShapeDtypeStruct((nb * NQ, d), F32),
        compiler_params=_cparams(("parallel", "arbitrary")),
        name="fox_decode_attn",
    )(page_table_flat, q_s, *([cache_k] * pps), *([cache_v] * pps), kn, vn, cum, cnew, gate_src)


def _pad_lanes(v, start=0):
    out = jnp.zeros((1, LANES), F32)
    return lax.dynamic_update_slice(out, v.reshape(1, -1).astype(F32), (0, start))


def kernel(x_prompt, x_sample, state_delta, state_conv, cache_k, cache_v, cache_logf, page_table,
           norm_mix, norm_mlp, a_w_in, a_conv, a_A_log, a_dt_bias, a_out_norm, a_w_out,
           b_w_in, b_f_bias, b_q_norm, b_k_norm, b_w_out, w_up, w_down):
    bp, t, d = x_prompt.shape
    bs, ts, _ = x_sample.shape
    depth = norm_mix.shape[0]
    n_mixers = 2
    H = state_delta.shape[2]
    DK = state_delta.shape[3]
    nqkv = state_conv.shape[-1]
    HB = cache_k.shape[3]
    HD = cache_k.shape[4]
    n_pool, page = cache_k.shape[1], cache_k.shape[2]
    npages = page_table.shape[1]
    d_ff = w_up.shape[2]
    mp, ms = bp * t, bs * ts
    assert H * DK == d and HB * HD == d and nqkv == 3 * d and DK == LANES and HD == LANES
    assert 2 * H <= LANES and HB <= LANES and 3 <= ts <= page and t >= 3

    x = jnp.concatenate([x_prompt.reshape(mp, d), x_sample.reshape(ms, d)], axis=0)
    pt_flat = page_table.reshape(-1).astype(jnp.int32)
    a_w_in_t = jnp.swapaxes(a_w_in, 1, 2)
    b_w_in_t = jnp.swapaxes(b_w_in, 1, 2)
    pad_rows = lambda w: jnp.pad(w, ((0, 0), (0, LANES - w.shape[1]), (0, 0)))
    a_w_gate_t = pad_rows(a_w_in_t[:, nqkv + d:, :])
    b_w_gate_t = pad_rows(b_w_in_t[:, 4 * d:, :])
    w_down_bf = w_down.astype(BF16)
    logf_hp = jnp.swapaxes(cache_logf, 2, 3).reshape(-1, HB, page)

    p_delta, p_conv, s_delta, s_conv = [], [], [], []
    p_k, p_v, p_lf, s_k, s_v, s_lf = [], [], [], [], [], []
    for i in range(depth):
        j = i // n_mixers
        hn = rmsnorm_bf16(x, norm_mix[i])
        if i % n_mixers == 0:
            proj = matmul_wcast(hn, a_w_in_t, j, n_out=nqkv + d, w_transposed=True)
            gates = matmul_wcast(hn, a_w_gate_t, j, n_out=LANES, w_transposed=True)
            alog_pad = _pad_lanes(a_A_log[j], H)
            dtb_pad = _pad_lanes(a_dt_bias[j], H)
            common = dict(H=H, DK=DK)
            cp = 64 if t % 64 == 0 else t
            yp, stp = gdn_core(proj, gates, a_conv[j], alog_pad, dtb_pad, a_out_norm[j], None, None,
                               row0=0, nb=bp, t=t, C=cp, G=max(1, min(H, MXU_DIM // cp)),
                               out_dtype=BF16, **common)
            ys, sts = gdn_core(proj, gates, a_conv[j], alog_pad, dtb_pad, a_out_norm[j],
                               state_conv[j], state_delta[j],
                               row0=mp, nb=bs, t=ts, C=ts, G=H, out_dtype=F32, **common)
            x = matmul_wcast(yp, a_w_out, j, n_out=d, epilogue="residual", residual=x, row0=0)
            x = matmul_wcast(ys.astype(BF16), a_w_out, j, n_out=d, epilogue="residual", residual=x, row0=mp)
            p_conv.append(jnp.stack([proj[b * t + t - 3:(b + 1) * t, :nqkv] for b in range(bp)]))
            s_conv.append(proj[mp:].reshape(bs, ts, -1)[:, ts - 3:, :nqkv])
            p_delta.append(stp)
            s_delta.append(sts)
        else:
            proj = matmul_wcast(hn, b_w_in_t, j, n_out=4 * d, w_transposed=True)
            fproj = matmul_wcast(hn, b_w_gate_t, j, n_out=LANES, w_transposed=True)
            prep = functools.partial(fox_prep, proj, fproj, b_q_norm[j], b_k_norm[j],
                                     _pad_lanes(b_f_bias[j]), H=HB, HD=HD)
            qb_p, kf_p, kb_p, vf_p, vb_p, lf_p = prep(row0=0, rows=mp)
            qb_s, kf_s, kb_s, vf_s, vb_s, lf_s = prep(row0=mp, rows=ms)
            c_p = cumsum_rows(lf_p.reshape(bp, t, LANES))
            c_row = jnp.transpose(c_p[:, :, :HB], (0, 2, 1)).reshape(bp, HB, 1, t) * LOG2E
            op = fox_prompt_attention(qb_p, kb_p, vb_p, c_row, proj, nb=bp, t=t, H=HB, HD=HD)
            lf_new_hp = jnp.swapaxes(lf_s[:, :HB].reshape(bs, ts, HB), 1, 2)
            lf_new_hp = jnp.pad(lf_new_hp, ((0, 0), (0, 0), (0, page - ts)))
            cum_hp, cnew_hp = paged_cumsum(pt_flat, logf_hp, lf_new_hp, layer_off=j * n_pool,
                                           nb=bs, npages=npages, page=page, H=HB)
            cum = jnp.swapaxes(cum_hp, 2, 3).reshape(bs, npages, 1, page * HB) * LOG2E
            cnew = jnp.swapaxes(cnew_hp[:, :, :ts], 1, 2).reshape(bs, 1, ts * HB) * LOG2E
            os_ = fox_decode_attention(
                pt_flat, qb_s.reshape(bs, ts, d),
                cache_k.reshape(-1, page * HB, HD), cache_v.reshape(-1, page * HB, HD),
                kb_s.reshape(bs, ts * HB, HD), vb_s.reshape(bs, ts * HB, HD),
                cum, cnew, proj,
                layer_off=j * n_pool, nb=bs, npages=npages, page=page, H=HB, HD=HD, NQ=ts,
                gate_rb0=mp // ts)
            x = matmul_wcast(op, b_w_out, j, n_out=d, epilogue="residual", residual=x, row0=0)
            x = matmul_wcast(os_.astype(BF16), b_w_out, j, n_out=d, epilogue="residual", residual=x, row0=mp)
            p_k.append(kf_p.reshape(bp, t, HB, HD))
            p_v.append(vf_p.reshape(bp, t, HB, HD))
            p_lf.append(lf_p[:, :HB].reshape(bp, t, HB))
            s_k.append(kf_s.reshape(bs, ts, HB, HD))
            s_v.append(vf_s.reshape(bs, ts, HB, HD))
            s_lf.append(lf_s[:, :HB].reshape(bs, ts, HB))
        hm = rmsnorm_bf16(x, norm_mlp[i])
        a = matmul_wcast(hm, w_up, i, n_out=d_ff, epilogue="relu2", out_dtype=BF16)
        down = functools.partial(matmul_fullk_residual, a, w_down_bf, i, x)
        if i + 1 < depth:
            x = down(row0=0, rows=mp + ms)
    y_p, y_s = down(row0=0, rows=mp), down(row0=mp, rows=ms)
    return (y_p.reshape(bp, t, d), y_s.reshape(bs, ts, d),
            jnp.stack(p_delta), jnp.stack(p_conv), jnp.stack(p_k), jnp.stack(p_v), jnp.stack(p_lf),
            jnp.stack(s_delta), jnp.stack(s_conv), jnp.stack(s_k), jnp.stack(s_v), jnp.stack(s_lf))
```

```python
import functools

import jax
import jax.numpy as jnp
from jax import lax
from jax.experimental import pallas as pl
from jax.experimental.pallas import tpu as pltpu

F32 = jnp.float32
BF16 = jnp.bfloat16
EPS = 1e-6
NEG = -1e30
LOG2E = 1.4426950408889634
LANES = 128
SUBLANES = 8
MXU_DIM = 256
VMEM_LIMIT = 56 * 1024 * 1024

ROW_TILES = (768, 512, 528, 384, 256, 128, 64, 32, 16, 8)
WCAST_ROW_TILES = (1056,) + ROW_TILES
COL_TILE = 1024
FULLK_COL_TILE = 512
ATTN_TQ = (512, 256, 128)
DECODE_PAGES_PER_STEP = (8, 4, 2, 1)
DECODE_KEY_CHUNK = (256, 128)
CUMSUM_PAGES_PER_STEP = (32, 16, 8, 4, 2, 1)

def _pick(n, cands):
    for c in cands:
        if n % c == 0:
            return c
    return n


def _cparams(sem):
    return pltpu.CompilerParams(dimension_semantics=sem, vmem_limit_bytes=VMEM_LIMIT)


def _dot(a, b):
    return jnp.dot(a.astype(BF16), b.astype(BF16), preferred_element_type=F32)


def _dot_nt(a, b):
    return lax.dot_general(a.astype(BF16), b.astype(BF16), (((1,), (1,)), ((), ())),
                           preferred_element_type=F32)


def _split3(x):
    hi = x.astype(BF16)
    r = x - hi.astype(F32)
    mid = r.astype(BF16)
    lo = (r - mid.astype(F32)).astype(BF16)
    return hi, mid, lo


def _dot_exact_lhs01(a01, b):
    hi, mid, lo = _split3(b)
    a = a01.astype(BF16)
    d = functools.partial(jnp.dot, preferred_element_type=F32)
    return d(a, hi) + (d(a, mid) + d(a, lo))


def _sigmoid(x):
    return 1.0 / (1.0 + jnp.exp(-x))


def _softplus(x):
    return jnp.maximum(x, 0.0) + jnp.log(1.0 + jnp.exp(-jnp.abs(x)))


def _rmsnorm_kernel(x_ref, w_ref, o_ref):
    x = x_ref[...]
    ms = jnp.mean(x * x, axis=-1, keepdims=True)
    o_ref[...] = (x * lax.rsqrt(ms + EPS) * w_ref[...]).astype(o_ref.dtype)


def rmsnorm_bf16(x, w):
    m, d = x.shape
    tm = _pick(m, ROW_TILES)
    return pl.pallas_call(
        _rmsnorm_kernel,
        grid=(m // tm,),
        in_specs=[pl.BlockSpec((tm, d), lambda i: (i, 0)),
                  pl.BlockSpec((1, d), lambda i: (0, 0))],
        out_specs=pl.BlockSpec((tm, d), lambda i: (i, 0)),
        out_shape=jax.ShapeDtypeStruct((m, d), BF16),
        compiler_params=_cparams(("parallel",)),
        name="rmsnorm",
    )(x, w.reshape(1, d))


def _epilogue(acc, kind, r_ref):
    if kind == "relu2":
        acc = jnp.maximum(acc, 0.0)
        acc = acc * acc
    elif kind == "residual":
        acc = acc + r_ref[...]
    return acc


def _mm_wcast_kernel(*refs, epilogue, w_transposed):
    if epilogue == "residual":
        a_ref, w_ref, r_ref, o_ref, wb_ref = refs
    else:
        a_ref, w_ref, o_ref, wb_ref = refs
        r_ref = None

    @pl.when(pl.program_id(1) == 0)
    def _():
        wb_ref[...] = w_ref[...].astype(wb_ref.dtype)

    if w_transposed:
        acc = _dot_nt(a_ref[...], wb_ref[...])
    else:
        acc = jnp.dot(a_ref[...], wb_ref[...], preferred_element_type=F32)
    o_ref[...] = _epilogue(acc, epilogue, r_ref).astype(o_ref.dtype)


def matmul_wcast(a, w, layer, *, n_out, w_transposed=False, epilogue="none", residual=None,
                 out_dtype=F32, row0=0):
    ma, kdim = a.shape
    tm = _pick(ma, WCAST_ROW_TILES if epilogue != "residual" else (512, 256, 128, 64, 32, 16, 8))
    tn = min(COL_TILE, n_out)
    assert n_out % tn == 0 and row0 % tm == 0 and w.shape[2 if w_transposed else 1] == kdim
    r0 = row0 // tm
    if w_transposed:
        w_spec = pl.BlockSpec((None, tn, kdim), lambda j, i: (layer, j, 0))
        wb_shape = (tn, kdim)
    else:
        w_spec = pl.BlockSpec((None, kdim, tn), lambda j, i: (layer, 0, j))
        wb_shape = (kdim, tn)
    in_specs = [pl.BlockSpec((tm, kdim), lambda j, i: (i, 0)), w_spec]
    args = [a, w]
    aliases = {}
    if epilogue == "residual":
        in_specs.append(pl.BlockSpec((tm, tn), lambda j, i: (r0 + i, j)))
        args.append(residual)
        aliases = {2: 0}
        out_rows = residual.shape[0]
    else:
        out_rows = ma
    return pl.pallas_call(
        functools.partial(_mm_wcast_kernel, epilogue=epilogue, w_transposed=w_transposed),
        grid=(n_out // tn, ma // tm),
        in_specs=in_specs,
        out_specs=pl.BlockSpec((tm, tn), lambda j, i: (r0 + i, j)),
        out_shape=jax.ShapeDtypeStruct((out_rows, n_out), out_dtype),
        scratch_shapes=[pltpu.VMEM(wb_shape, BF16)],
        input_output_aliases=aliases,
        compiler_params=_cparams(("arbitrary", "arbitrary")),
        name="matmul_wcast_" + epilogue,
    )(*args)


def _mm_fullk_kernel(a_ref, b_ref, r_ref, o_ref):
    acc = jnp.dot(a_ref[...], b_ref[...], preferred_element_type=F32)
    o_ref[...] = (acc + r_ref[...]).astype(o_ref.dtype)


def matmul_fullk_residual(a, b, layer, residual, *, row0, rows):
    _, kdim = a.shape
    n = b.shape[2]
    tm = _pick(rows, ROW_TILES)
    tn = min(FULLK_COL_TILE, n)
    assert n % tn == 0 and row0 % tm == 0
    r0 = row0 // tm
    return pl.pallas_call(
        _mm_fullk_kernel,
        grid=(rows // tm, n // tn),
        in_specs=[pl.BlockSpec((tm, kdim), lambda i, j: (r0 + i, 0)),
                  pl.BlockSpec((None, kdim, tn), lambda i, j: (layer, 0, j)),
                  pl.BlockSpec((tm, tn), lambda i, j: (r0 + i, j))],
        out_specs=pl.BlockSpec((tm, tn), lambda i, j: (i, j)),
        out_shape=jax.ShapeDtypeStruct((rows, n), F32),
        compiler_params=_cparams(("parallel", "arbitrary")),
        name="matmul_fullk_residual",
    )(a, b, residual)


def _gdn_kernel(*refs, C, G, H, DK, has_cbuf, has_s0):
    it = iter(refs)
    qkv_ref = next(it)
    z_ref = next(it)
    gates_ref = next(it)
    cw_ref = next(it)
    alog_ref = next(it)
    dtb_ref = next(it)
    onw_ref = next(it)
    cbuf_ref = next(it) if has_cbuf else None
    s0_ref = next(it) if has_s0 else None
    y_ref = next(it)
    sfin_ref = next(it)
    ext_ref = next(it)
    scat_ref = next(it)

    c_idx = pl.program_id(1)
    nc = pl.num_programs(1)
    R = G * C
    NG = H // G
    NQK = H * DK
    pad = SUBLANES

    @pl.when(c_idx == 0)
    def _init():
        ext_ref[0:pad, :] = jnp.zeros((pad, ext_ref.shape[1]), F32)
        if has_cbuf:
            ext_ref[pad - 3:pad, :] = cbuf_ref[0]
        for h in range(H):
            if has_s0:
                scat_ref[:, h * DK:(h + 1) * DK] = s0_ref[0, h]
            else:
                scat_ref[:, h * DK:(h + 1) * DK] = jnp.zeros((DK, DK), F32)

    ext_ref[pad:pad + C, :] = qkv_ref[...]

    def conv_act(col):
        sl = slice(col, col + DK)
        acc = ext_ref[pad:pad + C, sl] * cw_ref[3:4, sl]
        for i in range(3):
            acc = acc + ext_ref[pad - 3 + i:pad - 3 + i + C, sl] * cw_ref[i:i + 1, sl]
        return acc * _sigmoid(acc)

    gt = gates_ref[...]
    beta_full = _sigmoid(gt)
    g_full = -jnp.exp(alog_ref[...]) * _softplus(gt + dtb_ref[...])

    iotas = (lax.broadcasted_iota(jnp.int32, (R, R), 0), lax.broadcasted_iota(jnp.int32, (R, R), 1))

    def blk(axis, m):
        return lax.shift_right_logical(iotas[axis], m.bit_length() - 1)

    same = blk(0, C) == blk(1, C)
    eye = iotas[0] == iotas[1]
    incl = same & (iotas[1] <= iotas[0])
    strict = same & (iotas[1] < iotas[0])
    incl_t = same & (iotas[0] <= iotas[1])
    rowblk = lax.shift_right_logical(lax.broadcasted_iota(jnp.int32, (R, DK), 0), C.bit_length() - 1)
    cat = (lambda xs: xs[0]) if G == 1 else (lambda xs: jnp.concatenate(xs, axis=0))
    groups = [list(range(gi * G, (gi + 1) * G)) for gi in range(NG)]

    def group_program(heads):
        qs, ks, vs = [], [], []
        for h in heads:
            q = conv_act(h * DK)
            k = conv_act(NQK + h * DK)
            qs.append(q * lax.rsqrt(jnp.sum(q * q, axis=-1, keepdims=True) + EPS) * (DK ** -0.5))
            ks.append(k * lax.rsqrt(jnp.sum(k * k, axis=-1, keepdims=True) + EPS))
            vs.append(conv_act(2 * NQK + h * DK))
        Q, K, V = cat(qs), cat(ks), cat(vs)
        beta = cat([beta_full[:, h:h + 1] for h in heads])
        g_c = cat([g_full[:, H + h:H + h + 1] for h in heads])
        yield
        g_r = jnp.sum(jnp.where(eye, g_c, 0.0), axis=0, keepdims=True)
        gc = jnp.sum(jnp.where(incl, g_r, 0.0), axis=1, keepdims=True)
        gc_r = jnp.sum(jnp.where(incl_t, g_c, 0.0), axis=0, keepdims=True)
        gl = jnp.sum(jnp.where(same, g_r, 0.0), axis=1, keepdims=True)
        decay = jnp.exp(jnp.where(incl, gc - gc_r, NEG))
        KB = K * beta
        lmat = jnp.where(strict, _dot_nt(KB, K) * decay, 0.0)
        amat = jnp.where(incl, _dot_nt(Q, K) * decay, 0.0)
        yield
        ld = jnp.where(blk(0, SUBLANES) == blk(1, SUBLANES), lmat, 0.0)
        ld2 = _dot(ld, ld)
        yield
        ld4 = _dot(ld2, ld2)
        x = jnp.where(eye, 1.0, 0.0) - ld
        x = x + _dot(x, ld2)
        yield
        x = x + _dot(x, ld4)
        yield
        m = SUBLANES
        while m < C:
            sel = (blk(0, 2 * m) == blk(1, 2 * m)) & (blk(0, m) != blk(1, m))
            ox = _dot(jnp.where(sel, lmat, 0.0), x)
            yield
            x = x - _dot(x, ox)
            yield
            m *= 2
        eg = jnp.exp(gc)
        UW = _dot(x, jnp.concatenate([V * beta, KB * eg], axis=1))
        yield
        sc = scat_ref[:, heads[0] * DK:(heads[-1] + 1) * DK]
        WQS = _dot(jnp.concatenate([UW[:, DK:], Q * eg], axis=0), sc)
        yield
        vn, oi = [], []
        for j in range(G):
            rs = slice(j * C, (j + 1) * C)
            cs = slice(j * DK, (j + 1) * DK)
            vn.append(UW[rs, :DK] - WQS[rs, cs])
            oi.append(WQS[R + j * C:R + (j + 1) * C, cs])
        Vn = cat(vn)
        O = cat(oi) + _dot(amat, Vn)
        ke = K * jnp.exp(gl - gc)
        vwide = jnp.concatenate([jnp.where(rowblk == j, Vn, 0.0) for j in range(G)], axis=1)
        dS = _dot(ke.T, vwide)
        yield
        for j, h in enumerate(heads):
            cs = slice(j * DK, (j + 1) * DK)
            e_last = jnp.exp(gl[j * C:j * C + 1, :])
            scat_ref[:, h * DK:(h + 1) * DK] = sc[:, cs] * e_last + dS[:, cs]
            o = O[j * C:(j + 1) * C]
            o = o * lax.rsqrt(jnp.mean(o * o, axis=-1, keepdims=True) + EPS) * onw_ref[...]
            zg = z_ref[:, h * DK:(h + 1) * DK]
            y_ref[:, h * DK:(h + 1) * DK] = (o * (zg * _sigmoid(zg))).astype(y_ref.dtype)

    programs = [group_program(heads) for heads in groups]
    while programs:
        running = []
        for prog in programs:
            try:
                next(prog)
                running.append(prog)
            except StopIteration:
                pass
        programs = running

    carry = ext_ref[C:C + pad, :]
    ext_ref[0:pad, :] = carry

    @pl.when(c_idx == nc - 1)
    def _fin():
        for h in range(H):
            sfin_ref[0, h] = scat_ref[:, h * DK:(h + 1) * DK]


def gdn_core(proj, gates, conv_w, alog_pad, dtb_pad, out_norm, conv_buf, s0, *,
             row0, nb, t, C, G, H, DK, out_dtype):
    nqkv = 3 * H * DK
    nc = t // C
    assert t % C == 0 and row0 % C == 0 and H % G == 0 and C & (C - 1) == 0
    rb0 = row0 // C
    row_map = lambda b, c: (rb0 + b * nc + c, 0)
    has_cbuf = conv_buf is not None
    has_s0 = s0 is not None
    in_specs = [
        pl.BlockSpec((C, nqkv), row_map),
        pl.BlockSpec((C, H * DK), lambda b, c: (rb0 + b * nc + c, 3)),
        pl.BlockSpec((C, LANES), row_map),
        pl.BlockSpec((4, nqkv), lambda b, c: (0, 0)),
        pl.BlockSpec((1, LANES), lambda b, c: (0, 0)),
        pl.BlockSpec((1, LANES), lambda b, c: (0, 0)),
        pl.BlockSpec((1, DK), lambda b, c: (0, 0)),
    ]
    args = [proj, proj, gates, conv_w, alog_pad, dtb_pad, out_norm.reshape(1, DK)]
    if has_cbuf:
        in_specs.append(pl.BlockSpec((1, 3, nqkv), lambda b, c: (b, 0, 0)))
        args.append(conv_buf)
    if has_s0:
        in_specs.append(pl.BlockSpec((1, H, DK, DK), lambda b, c: (b, 0, 0, 0)))
        args.append(s0)
    y, sfin = pl.pallas_call(
        functools.partial(_gdn_kernel, C=C, G=G, H=H, DK=DK, has_cbuf=has_cbuf, has_s0=has_s0),
        grid=(nb, nc),
        in_specs=in_specs,
        out_specs=[pl.BlockSpec((C, H * DK), lambda b, c: (b * nc + c, 0)),
                   pl.BlockSpec((1, H, DK, DK), lambda b, c: (b, 0, 0, 0))],
        out_shape=[jax.ShapeDtypeStruct((nb * t, H * DK), out_dtype),
                   jax.ShapeDtypeStruct((nb, H, DK, DK), F32)],
        scratch_shapes=[pltpu.VMEM((C + SUBLANES, nqkv), F32),
                        pltpu.VMEM((DK, H * DK), F32)],
        compiler_params=_cparams(("parallel", "arbitrary")),
        name="gdn_core_c%d" % C,
    )(*args)
    return y, sfin


def _fox_prep_kernel(q_ref, k_ref, v_ref, f_ref, qn_ref, kn_ref, fb_ref,
                     qo_ref, ko_ref, kb_ref, vo_ref, vb_ref, lf_ref, *, H, HD):
    q_scale = (HD ** -0.5) * LOG2E
    for h in range(H):
        sl = slice(h * HD, (h + 1) * HD)
        q = q_ref[:, sl]
        q = q * lax.rsqrt(jnp.mean(q * q, axis=-1, keepdims=True) + EPS) * qn_ref[...]
        qo_ref[:, sl] = (q * q_scale).astype(qo_ref.dtype)
        k = k_ref[:, sl]
        k = k * lax.rsqrt(jnp.mean(k * k, axis=-1, keepdims=True) + EPS) * kn_ref[...]
        ko_ref[:, sl] = k
        kb_ref[:, sl] = k.astype(kb_ref.dtype)
    v = v_ref[...]
    vo_ref[...] = v
    vb_ref[...] = v.astype(vb_ref.dtype)
    x = f_ref[...] + fb_ref[...]
    lf_ref[...] = jnp.minimum(x, 0.0) - jnp.log1p(jnp.exp(-jnp.abs(x)))


def fox_prep(proj, fproj, q_norm, k_norm, fb_pad, *, row0, rows, H, HD):
    d = H * HD
    tm = _pick(rows, (256, 128, 64, 32, 16, 8))
    assert row0 % tm == 0
    r0 = row0 // tm
    col = lambda j: (lambda i: (r0 + i, j))
    out = lambda i: (i, 0)
    return pl.pallas_call(
        functools.partial(_fox_prep_kernel, H=H, HD=HD),
        grid=(rows // tm,),
        in_specs=[pl.BlockSpec((tm, d), col(0)), pl.BlockSpec((tm, d), col(1)),
                  pl.BlockSpec((tm, d), col(2)), pl.BlockSpec((tm, LANES), col(0)),
                  pl.BlockSpec((1, HD), lambda i: (0, 0)), pl.BlockSpec((1, HD), lambda i: (0, 0)),
                  pl.BlockSpec((1, LANES), lambda i: (0, 0))],
        out_specs=[pl.BlockSpec((tm, d), out)] * 5 + [pl.BlockSpec((tm, LANES), out)],
        out_shape=[jax.ShapeDtypeStruct((rows, d), BF16), jax.ShapeDtypeStruct((rows, d), F32),
                   jax.ShapeDtypeStruct((rows, d), BF16), jax.ShapeDtypeStruct((rows, d), F32),
                   jax.ShapeDtypeStruct((rows, d), BF16), jax.ShapeDtypeStruct((rows, LANES), F32)],
        compiler_params=_cparams(("parallel",)),
        name="fox_prep",
    )(proj, proj, proj, fproj, q_norm.reshape(1, HD), k_norm.reshape(1, HD), fb_pad)


def _cumsum_block(x, carry):
    n = x.shape[0]
    ri = lax.broadcasted_iota(jnp.int32, (n, n), 0)
    ci = lax.broadcasted_iota(jnp.int32, (n, n), 1)
    tril = jnp.where(ci <= ri, 1.0, 0.0)
    return _dot_exact_lhs01(tril, x) + carry


def _cumsum_kernel(x_ref, o_ref, carry_ref):
    @pl.when(pl.program_id(1) == 0)
    def _():
        carry_ref[...] = jnp.zeros_like(carry_ref)

    cs = _cumsum_block(x_ref[0], carry_ref[...])
    o_ref[0] = cs
    carry_ref[...] = cs[cs.shape[0] - 1:, :]


def cumsum_rows(x):
    b, t, w = x.shape
    tb = _pick(t, (256, 128, 64, 32, 16, 8))
    return pl.pallas_call(
        _cumsum_kernel,
        grid=(b, t // tb),
        in_specs=[pl.BlockSpec((1, tb, w), lambda i, j: (i, j, 0))],
        out_specs=pl.BlockSpec((1, tb, w), lambda i, j: (i, j, 0)),
        out_shape=jax.ShapeDtypeStruct((b, t, w), F32),
        scratch_shapes=[pltpu.VMEM((1, w), F32)],
        compiler_params=_cparams(("parallel", "arbitrary")),
        name="cumsum_rows",
    )(x)


def _cumsum_lanes(x, carry):
    n = x.shape[1]
    ri = lax.broadcasted_iota(jnp.int32, (n, n), 0)
    ci = lax.broadcasted_iota(jnp.int32, (n, n), 1)
    triu = jnp.where(ri <= ci, 1.0, 0.0).astype(BF16)
    hi, mid, lo = _split3(x)
    d = functools.partial(jnp.dot, preferred_element_type=F32)
    return d(hi, triu) + (d(mid, triu) + d(lo, triu)) + carry


def _paged_cumsum_kernel(pt_ref, *refs, pps):
    lp_refs = refs[:pps]
    ln_ref, o_ref, on_ref, carry_ref = refs[pps:]
    p = pl.program_id(1)

    @pl.when(p == 0)
    def _():
        carry_ref[...] = jnp.zeros_like(carry_ref)

    h = lp_refs[0].shape[1]
    local = _cumsum_lanes(jnp.concatenate([r[0] for r in lp_refs], axis=0) if pps > 1 else lp_refs[0][0], 0.0)
    carry = carry_ref[...]
    for j in range(pps):
        lj = local[j * h:(j + 1) * h]
        o_ref[0, j] = lj + carry
        carry = carry + lj[:, lj.shape[1] - 1:]
    carry_ref[...] = carry

    @pl.when(p == pl.num_programs(1) - 1)
    def _():
        on_ref[0] = _cumsum_lanes(ln_ref[0], carry)


def paged_cumsum(page_table_flat, logf_hp, lf_new_hp, *, layer_off, nb, npages, page, H):
    pps = _pick(npages, CUMSUM_PAGES_PER_STEP)
    ts = lf_new_hp.shape[2]
    assert ts == page

    def pool_map(j):
        return lambda b, p, pt: (layer_off + pt[b * npages + p * pps + j], 0, 0)

    return pl.pallas_call(
        functools.partial(_paged_cumsum_kernel, pps=pps),
        grid_spec=pltpu.PrefetchScalarGridSpec(
            num_scalar_prefetch=1,
            grid=(nb, npages // pps),
            in_specs=[pl.BlockSpec((1, H, page), pool_map(j)) for j in range(pps)]
            + [pl.BlockSpec((1, H, ts), lambda b, p, pt: (b, 0, 0))],
            out_specs=[pl.BlockSpec((1, pps, H, page), lambda b, p, pt: (b, p, 0, 0)),
                       pl.BlockSpec((1, H, ts), lambda b, p, pt: (b, 0, 0))],
            scratch_shapes=[pltpu.VMEM((H, 1), F32)],
        ),
        out_shape=[jax.ShapeDtypeStruct((nb, npages, H, page), F32),
                   jax.ShapeDtypeStruct((nb, H, ts), F32)],
        compiler_params=_cparams(("parallel", "arbitrary")),
        name="paged_cumsum",
    )(page_table_flat, *([logf_hp] * pps), lf_new_hp)


def _fox_prompt_kernel(q_ref, k_ref, v_ref, c_ref, g_ref, o_ref, *, tq):
    qi = pl.program_id(2)
    q = q_ref[...]
    hd = q.shape[1]

    def block(j, carry, diag):
        m, l, acc = carry
        start = pl.multiple_of(j * tq, tq)
        k = k_ref[pl.ds(start, tq), :]
        v = v_ref[pl.ds(start, tq), :]
        s = _dot_nt(q, k) - c_ref[0, 0, :, pl.ds(start, tq)]
        if diag:
            rpos = lax.broadcasted_iota(jnp.int32, (tq, tq), 0)
            cpos = lax.broadcasted_iota(jnp.int32, (tq, tq), 1)
            s = jnp.where(cpos <= rpos, s, NEG)
        m_new = jnp.maximum(m, jnp.max(s, axis=1, keepdims=True))
        alpha = jnp.exp2(m - m_new)
        p = jnp.exp2(s - m_new)
        l = alpha * l + jnp.sum(p, axis=1, keepdims=True)
        acc = alpha * acc + _dot(p, v)
        return m_new, l, acc

    init = (jnp.full((tq, 1), NEG, F32), jnp.zeros((tq, 1), F32), jnp.zeros((tq, hd), F32))
    carry = lax.fori_loop(0, qi, lambda j, c: block(j, c, False), init)
    m, l, acc = block(qi, carry, True)
    g = g_ref[...]
    o_ref[...] = ((acc / l) * _sigmoid(g)).astype(o_ref.dtype)


def fox_prompt_attention(qb, kb, vb, c_row, proj, *, nb, t, H, HD):
    tq = _pick(t, ATTN_TQ)
    nq = t // tq
    return pl.pallas_call(
        functools.partial(_fox_prompt_kernel, tq=tq),
        grid=(nb, H, nq),
        in_specs=[pl.BlockSpec((tq, HD), lambda b, h, i: (b * nq + i, h)),
                  pl.BlockSpec((t, HD), lambda b, h, i: (b, h)),
                  pl.BlockSpec((t, HD), lambda b, h, i: (b, h)),
                  pl.BlockSpec((1, 1, 1, t), lambda b, h, i: (b, h, 0, 0)),
                  pl.BlockSpec((tq, HD), lambda b, h, i: (b * nq + i, 3 * H + h))],
        out_specs=pl.BlockSpec((tq, HD), lambda b, h, i: (b * nq + i, h)),
        out_shape=jax.ShapeDtypeStruct((nb * t, H * HD), BF16),
        compiler_params=_cparams(("parallel", "parallel", "arbitrary")),
        name="fox_prompt_attn",
    )(qb, kb, vb, c_row, proj)


def _fox_decode_kernel(pt_ref, *refs, pps, H, HD, NQ):
    q_ref = refs[0]
    kp_refs = refs[1:1 + pps]
    vp_refs = refs[1 + pps:1 + 2 * pps]
    (kn_ref, vn_ref, cum_ref, cn_ref, g_ref,
     o_ref, qcat_ref, mask_ref, m_ref, acc_ref, s_ref, vx_ref) = refs[1 + 2 * pps:]
    p = pl.program_id(1)
    rows = H * NQ
    kc = mask_ref.shape[1]
    width = kp_refs[0].shape[1]

    @pl.when(p == 0)
    def _():
        q = q_ref[0].astype(F32)
        for h in range(H):
            qcat_ref[h * NQ:(h + 1) * NQ, :] = q[:, h * HD:(h + 1) * HD]
        rh = lax.broadcasted_iota(jnp.int32, (rows, kc), 0) // NQ
        ch = lax.broadcasted_iota(jnp.int32, (rows, kc), 1) % H
        mask_ref[...] = jnp.where(rh == ch, 0.0, NEG)
        m_ref[...] = jnp.full_like(m_ref, NEG)
        acc_ref[...] = jnp.zeros_like(acc_ref)
        vx_ref[:, :, HD:] = jnp.ones((vx_ref.shape[0], width, HD), vx_ref.dtype)

    qc = qcat_ref[...].astype(BF16)

    chunks = range(0, width, kc)
    slots = vx_ref.shape[0]

    def scores(j, c0, mx):
        s = _dot_nt(qc, kp_refs[j][0, c0:c0 + kc, :]) + (mask_ref[...] - cum_ref[0, j, :, c0:c0 + kc])
        s_ref[:, (j % slots) * width + c0:(j % slots) * width + c0 + kc] = s
        return jnp.maximum(mx, s)

    def weighted(j, c0, m_new, pv):
        pr = jnp.exp2(s_ref[:, (j % slots) * width + c0:(j % slots) * width + c0 + kc] - m_new)
        return pv + _dot(pr, vx_ref[j % slots, c0:c0 + kc, :])

    def stage_values(j):
        vx_ref[j % slots, :, :HD] = vp_refs[j][0].astype(vx_ref.dtype)

    m, acc = m_ref[...], acc_ref[...]
    neg = jnp.full((rows, kc), NEG, F32)
    stage_values(0)
    mx = neg
    for c0 in chunks:
        mx = scores(0, c0, mx)
    for j in range(pps):
        m_new = jnp.maximum(m, jnp.max(mx, axis=1, keepdims=True))
        pv = jnp.zeros((rows, 2 * HD), F32)
        nxt = j + 1 < pps
        if nxt:
            stage_values(j + 1)
        mx = neg
        for c0 in chunks:
            pv = weighted(j, c0, m_new, pv)
            if nxt:
                mx = scores(j + 1, c0, mx)
        acc = jnp.exp2(m - m_new) * acc + pv
        m = m_new
    m_ref[...], acc_ref[...] = m, acc

    @pl.when(p == pl.num_programs(1) - 1)
    def _():
        wn = kn_ref.shape[1]
        r = lax.broadcasted_iota(jnp.int32, (rows, wn), 0)
        c = lax.broadcasted_iota(jnp.int32, (rows, wn), 1)
        ok = (r // NQ == c % H) & (c // H <= r % NQ)
        s = jnp.where(ok, _dot_nt(qc, kn_ref[0]) - cn_ref[0], NEG)
        m_fin = jnp.maximum(m, jnp.max(s, axis=1, keepdims=True))
        vn = vn_ref[0]
        fin = (jnp.exp2(m - m_fin) * acc
               + _dot(jnp.exp2(s - m_fin), jnp.concatenate([vn, jnp.ones_like(vn)], axis=1)))
        out = fin[:, :HD] / fin[:, HD:]
        for h in range(H):
            cs = slice(h * HD, (h + 1) * HD)
            o_ref[:, cs] = (out[h * NQ:(h + 1) * NQ, :] * _sigmoid(g_ref[:, cs])).astype(o_ref.dtype)


def fox_decode_attention(page_table_flat, q_s, cache_k, cache_v, kn, vn, cum, cnew, gate_src, *,
                         layer_off, nb, npages, page, H, HD, NQ, gate_rb0):
    d = H * HD
    pps = _pick(npages, DECODE_PAGES_PER_STEP)
    rows = H * NQ
    kc = _pick(page * H, DECODE_KEY_CHUNK)
    slots = min(2, pps)
    assert kc % H == 0 and (page * H) % kc == 0

    def pool_map(j):
        return lambda b, p, pt: (layer_off + pt[b * npages + p * pps + j], 0, 0)

    per_b = lambda b, p, pt: (b, 0, 0)
    pool_specs = [pl.BlockSpec((1, page * H, HD), pool_map(j)) for j in range(pps)]
    return pl.pallas_call(
        functools.partial(_fox_decode_kernel, pps=pps, H=H, HD=HD, NQ=NQ),
        grid_spec=pltpu.PrefetchScalarGridSpec(
            num_scalar_prefetch=1,
            grid=(nb, npages // pps),
            in_specs=[pl.BlockSpec((1, NQ, d), per_b)] + pool_specs + pool_specs
            + [pl.BlockSpec((1, NQ * H, HD), per_b),
               pl.BlockSpec((1, NQ * H, HD), per_b),
               pl.BlockSpec((1, pps, 1, page * H), lambda b, p, pt: (b, p, 0, 0)),
               pl.BlockSpec((1, 1, NQ * H), per_b),
               pl.BlockSpec((NQ, d), lambda b, p, pt: (gate_rb0 + b, 3))],
            out_specs=pl.BlockSpec((NQ, d), lambda b, p, pt: (b, 0)),
            scratch_shapes=[pltpu.VMEM((rows, HD), F32),
                            pltpu.VMEM((rows, kc), F32),
                            pltpu.VMEM((rows, 1), F32),
                            pltpu.VMEM((rows, 2 * HD), F32),
                            pltpu.VMEM((rows, slots * page * H), F32),
                            pltpu.VMEM((slots, page * H, 2 * HD), BF16)],
        ),
        out_shape=jax.ShapeDtypeStruct((nb * NQ, d), F32),
        compiler_params=_cparams(("parallel", "arbitrary")),
        name="fox_decode_attn",
    )(page_table_flat, q_s, *([cache_k] * pps), *([cache_v] * pps), kn, vn, cum, cnew, gate_src)


def _pad_lanes(v, start=0):
    out = jnp.zeros((1, LANES), F32)
    return lax.dynamic_update_slice(out, v.reshape(1, -1).astype(F32), (0, start))


def kernel(x_prompt, x_sample, state_delta, state_conv, cache_k, cache_v, cache_logf, page_table,
           norm_mix, norm_mlp, a_w_in, a_conv, a_A_log, a_dt_bias, a_out_norm, a_w_out,
           b_w_in, b_f_bias, b_q_norm, b_k_norm, b_w_out, w_up, w_down):
    bp, t, d = x_prompt.shape
    bs, ts, _ = x_sample.shape
    depth = norm_mix.shape[0]
    n_mixers = 2
    H = state_delta.shape[2]
    DK = state_delta.shape[3]
    nqkv = state_conv.shape[-1]
    HB = cache_k.shape[3]
    HD = cache_k.shape[4]
    n_pool, page = cache_k.shape[1], cache_k.shape[2]
    npages = page_table.shape[1]
    d_ff = w_up.shape[2]
    mp, ms = bp * t, bs * ts
    assert H * DK == d and HB * HD == d and nqkv == 3 * d and DK == LANES and HD == LANES
    assert 2 * H <= LANES and HB <= LANES and 3 <= ts <= page and t >= 3

    x = jnp.concatenate([x_prompt.reshape(mp, d), x_sample.reshape(ms, d)], axis=0)
    pt_flat = page_table.reshape(-1).astype(jnp.int32)
    a_w_in_t = jnp.swapaxes(a_w_in, 1, 2)
    b_w_in_t = jnp.swapaxes(b_w_in, 1, 2)
    pad_rows = lambda w: jnp.pad(w, ((0, 0), (0, LANES - w.shape[1]), (0, 0)))
    a_w_gate_t = pad_rows(a_w_in_t[:, nqkv + d:, :])
    b_w_gate_t = pad_rows(b_w_in_t[:, 4 * d:, :])
    w_down_bf = w_down.astype(BF16)
    logf_hp = jnp.swapaxes(cache_logf, 2, 3).reshape(-1, HB, page)

    p_delta, p_conv, s_delta, s_conv = [], [], [], []
    p_k, p_v, p_lf, s_k, s_v, s_lf = [], [], [], [], [], []
    for i in range(depth):
        j = i // n_mixers
        hn = rmsnorm_bf16(x, norm_mix[i])
        if i % n_mixers == 0:
            proj = matmul_wcast(hn, a_w_in_t, j, n_out=nqkv + d, w_transposed=True)
            gates = matmul_wcast(hn, a_w_gate_t, j, n_out=LANES, w_transposed=True)
            alog_pad = _pad_lanes(a_A_log[j], H)
            dtb_pad = _pad_lanes(a_dt_bias[j], H)
            common = dict(H=H, DK=DK)
            cp = 64 if t % 64 == 0 else t
            yp, stp = gdn_core(proj, gates, a_conv[j], alog_pad, dtb_pad, a_out_norm[j], None, None,
                               row0=0, nb=bp, t=t, C=cp, G=max(1, min(H, MXU_DIM // cp)),
                               out_dtype=BF16, **common)
            ys, sts = gdn_core(proj, gates, a_conv[j], alog_pad, dtb_pad, a_out_norm[j],
                               state_conv[j], state_delta[j],
                               row0=mp, nb=bs, t=ts, C=ts, G=H, out_dtype=F32, **common)
            x = matmul_wcast(yp, a_w_out, j, n_out=d, epilogue="residual", residual=x, row0=0)
            x = matmul_wcast(ys.astype(BF16), a_w_out, j, n_out=d, epilogue="residual", residual=x, row0=mp)
            p_conv.append(jnp.stack([proj[b * t + t - 3:(b + 1) * t, :nqkv] for b in range(bp)]))
            s_conv.append(proj[mp:].reshape(bs, ts, -1)[:, ts - 3:, :nqkv])
            p_delta.append(stp)
            s_delta.append(sts)
        else:
            proj = matmul_wcast(hn, b_w_in_t, j, n_out=4 * d, w_transposed=True)
            fproj = matmul_wcast(hn, b_w_gate_t, j, n_out=LANES, w_transposed=True)
            prep = functools.partial(fox_prep, proj, fproj, b_q_norm[j], b_k_norm[j],
                                     _pad_lanes(b_f_bias[j]), H=HB, HD=HD)
            qb_p, kf_p, kb_p, vf_p, vb_p, lf_p = prep(row0=0, rows=mp)
            qb_s, kf_s, kb_s, vf_s, vb_s, lf_s = prep(row0=mp, rows=ms)
            c_p = cumsum_rows(lf_p.reshape(bp, t, LANES))
            c_row = jnp.transpose(c_p[:, :, :HB], (0, 2, 1)).reshape(bp, HB, 1, t) * LOG2E
            op = fox_prompt_attention(qb_p, kb_p, vb_p, c_row, proj, nb=bp, t=t, H=HB, HD=HD)
            lf_new_hp = jnp.swapaxes(lf_s[:, :HB].reshape(bs, ts, HB), 1, 2)
            lf_new_hp = jnp.pad(lf_new_hp, ((0, 0), (0, 0), (0, page - ts)))
            cum_hp, cnew_hp = paged_cumsum(pt_flat, logf_hp, lf_new_hp, layer_off=j * n_pool,
                                           nb=bs, npages=npages, page=page, H=HB)
            cum = jnp.swapaxes(cum_hp, 2, 3).reshape(bs, npages, 1, page * HB) * LOG2E
            cnew = jnp.swapaxes(cnew_hp[:, :, :ts], 1, 2).reshape(bs, 1, ts * HB) * LOG2E
            os_ = fox_decode_attention(
                pt_flat, qb_s.reshape(bs, ts, d),
                cache_k.reshape(-1, page * HB, HD), cache_v.reshape(-1, page * HB, HD),
                kb_s.reshape(bs, ts * HB, HD), vb_s.reshape(bs, ts * HB, HD),
                cum, cnew, proj,
                layer_off=j * n_pool, nb=bs, npages=npages, page=page, H=HB, HD=HD, NQ=ts,
                gate_rb0=mp // ts)
            x = matmul_wcast(op, b_w_out, j, n_out=d, epilogue="residual", residual=x, row0=0)
            x = matmul_wcast(os_.astype(BF16), b_w_out, j, n_out=d, epilogue="residual", residual=x, row0=mp)
            p_k.append(kf_p.reshape(bp, t, HB, HD))
            p_v.append(vf_p.reshape(bp, t, HB, HD))
            p_lf.append(lf_p[:, :HB].reshape(bp, t, HB))
            s_k.append(kf_s.reshape(bs, ts, HB, HD))
            s_v.append(vf_s.reshape(bs, ts, HB, HD))
            s_lf.append(lf_s[:, :HB].reshape(bs, ts, HB))
        hm = rmsnorm_bf16(x, norm_mlp[i])
        a = matmul_wcast(hm, w_up, i, n_out=d_ff, epilogue="relu2", out_dtype=BF16)
        down = functools.partial(matmul_fullk_residual, a, w_down_bf, i, x)
        if i + 1 < depth:
            x = down(row0=0, rows=mp + ms)
    y_p, y_s = down(row0=0, rows=mp), down(row0=mp, rows=ms)
    return (y_p.reshape(bp, t, d), y_s.reshape(bs, ts, d),
            jnp.stack(p_delta), jnp.stack(p_conv), jnp.stack(p_k), jnp.stack(p_v), jnp.stack(p_lf),
            jnp.stack(s_delta), jnp.stack(s_conv), jnp.stack(s_k), jnp.stack(s_v), jnp.stack(s_lf))
```

```python
import functools

import jax
import jax.numpy as jnp
from jax import lax
from jax.experimental import pallas as pl
from jax.experimental.pallas import tpu as pltpu

F32 = jnp.float32
BF16 = jnp.bfloat16
EPS = 1e-6
NEG = -1e30
LOG2E = 1.4426950408889634
LANES = 128
SUBLANES = 8
MXU_DIM = 256
VMEM_LIMIT = 56 * 1024 * 1024

ROW_TILES = (768, 512, 528, 384, 256, 128, 64, 32, 16, 8)
WCAST_ROW_TILES = (1056,) + ROW_TILES
COL_TILE = 1024
FULLK_COL_TILE = 512
ATTN_TQ = (512, 256, 128)
DECODE_PAGES_PER_STEP = (8, 4, 2, 1)
DECODE_KEY_CHUNK = (256, 128)
CUMSUM_PAGES_PER_STEP = (32, 16, 8, 4, 2, 1)
GDN_TOKENS_PER_STEP = 128
GDN_STACK_ROWS = 128

def _pick(n, cands):
    for c in cands:
        if n % c == 0:
            return c
    return n


def _cparams(sem):
    return pltpu.CompilerParams(dimension_semantics=sem, vmem_limit_bytes=VMEM_LIMIT)


def _dot(a, b):
    return jnp.dot(a.astype(BF16), b.astype(BF16), preferred_element_type=F32)


def _dot_nt(a, b):
    return lax.dot_general(a.astype(BF16), b.astype(BF16), (((1,), (1,)), ((), ())),
                           preferred_element_type=F32)


def _split3(x):
    hi = x.astype(BF16)
    r = x - hi.astype(F32)
    mid = r.astype(BF16)
    lo = (r - mid.astype(F32)).astype(BF16)
    return hi, mid, lo


def _dot_exact_lhs01(a01, b):
    hi, mid, lo = _split3(b)
    a = a01.astype(BF16)
    d = functools.partial(jnp.dot, preferred_element_type=F32)
    return d(a, hi) + (d(a, mid) + d(a, lo))


def _sigmoid(x):
    return 1.0 / (1.0 + jnp.exp(-x))


def _softplus(x):
    return jnp.maximum(x, 0.0) + jnp.log(1.0 + jnp.exp(-jnp.abs(x)))


def _rmsnorm_kernel(x_ref, w_ref, o_ref):
    x = x_ref[...]
    ms = jnp.mean(x * x, axis=-1, keepdims=True)
    o_ref[...] = (x * lax.rsqrt(ms + EPS) * w_ref[...]).astype(o_ref.dtype)


def rmsnorm_bf16(x, w):
    m, d = x.shape
    tm = _pick(m, ROW_TILES)
    return pl.pallas_call(
        _rmsnorm_kernel,
        grid=(m // tm,),
        in_specs=[pl.BlockSpec((tm, d), lambda i: (i, 0)),
                  pl.BlockSpec((1, d), lambda i: (0, 0))],
        out_specs=pl.BlockSpec((tm, d), lambda i: (i, 0)),
        out_shape=jax.ShapeDtypeStruct((m, d), BF16),
        compiler_params=_cparams(("parallel",)),
        name="rmsnorm",
    )(x, w.reshape(1, d))


def _epilogue(acc, kind, r_ref):
    if kind == "relu2":
        acc = jnp.maximum(acc, 0.0)
        acc = acc * acc
    elif kind == "residual":
        acc = acc + r_ref[...]
    return acc


def _mm_wcast_kernel(*refs, epilogue, w_transposed):
    if epilogue == "residual":
        a_ref, w_ref, r_ref, o_ref, wb_ref = refs
    else:
        a_ref, w_ref, o_ref, wb_ref = refs
        r_ref = None

    @pl.when(pl.program_id(1) == 0)
    def _():
        wb_ref[...] = w_ref[...].astype(wb_ref.dtype)

    if w_transposed:
        acc = _dot_nt(a_ref[...], wb_ref[...])
    else:
        acc = jnp.dot(a_ref[...], wb_ref[...], preferred_element_type=F32)
    o_ref[...] = _epilogue(acc, epilogue, r_ref).astype(o_ref.dtype)


def matmul_wcast(a, w, layer, *, n_out, w_transposed=False, epilogue="none", residual=None,
                 out_dtype=F32, row0=0):
    ma, kdim = a.shape
    tm = _pick(ma, WCAST_ROW_TILES if epilogue != "residual" else (1024, 512, 256, 128, 64, 32, 16, 8))
    tn = min(COL_TILE, n_out)
    assert n_out % tn == 0 and row0 % tm == 0 and w.shape[2 if w_transposed else 1] == kdim
    r0 = row0 // tm
    if w_transposed:
        w_spec = pl.BlockSpec((None, tn, kdim), lambda j, i: (layer, j, 0))
        wb_shape = (tn, kdim)
    else:
        w_spec = pl.BlockSpec((None, kdim, tn), lambda j, i: (layer, 0, j))
        wb_shape = (kdim, tn)
    in_specs = [pl.BlockSpec((tm, kdim), lambda j, i: (i, 0)), w_spec]
    args = [a, w]
    aliases = {}
    if epilogue == "residual":
        in_specs.append(pl.BlockSpec((tm, tn), lambda j, i: (r0 + i, j)))
        args.append(residual)
        aliases = {2: 0}
        out_rows = residual.shape[0]
    else:
        out_rows = ma
    return pl.pallas_call(
        functools.partial(_mm_wcast_kernel, epilogue=epilogue, w_transposed=w_transposed),
        grid=(n_out // tn, ma // tm),
        in_specs=in_specs,
        out_specs=pl.BlockSpec((tm, tn), lambda j, i: (r0 + i, j)),
        out_shape=jax.ShapeDtypeStruct((out_rows, n_out), out_dtype),
        scratch_shapes=[pltpu.VMEM(wb_shape, BF16)],
        input_output_aliases=aliases,
        compiler_params=_cparams(("arbitrary", "arbitrary")),
        name="matmul_wcast_" + epilogue,
    )(*args)


def _mm_fullk_kernel(a_ref, b_ref, r_ref, o_ref):
    acc = jnp.dot(a_ref[...], b_ref[...], preferred_element_type=F32)
    o_ref[...] = (acc + r_ref[...]).astype(o_ref.dtype)


def matmul_fullk_residual(a, b, layer, residual, *, row0, rows):
    _, kdim = a.shape
    n = b.shape[2]
    tm = _pick(rows, ROW_TILES)
    tn = min(FULLK_COL_TILE, n)
    assert n % tn == 0 and row0 % tm == 0
    r0 = row0 // tm
    return pl.pallas_call(
        _mm_fullk_kernel,
        grid=(rows // tm, n // tn),
        in_specs=[pl.BlockSpec((tm, kdim), lambda i, j: (r0 + i, 0)),
                  pl.BlockSpec((None, kdim, tn), lambda i, j: (layer, 0, j)),
                  pl.BlockSpec((tm, tn), lambda i, j: (r0 + i, j))],
        out_specs=pl.BlockSpec((tm, tn), lambda i, j: (i, j)),
        out_shape=jax.ShapeDtypeStruct((rows, n), F32),
        compiler_params=_cparams(("parallel", "arbitrary")),
        name="matmul_fullk_residual",
    )(a, b, residual)


def _gdn_kernel(*refs, C, G, H, DK, NS, has_cbuf, has_s0):
    it = iter(refs)
    seq_refs = [(next(it), next(it), next(it)) for _ in range(NS)]
    cw_ref = next(it)
    alog_ref = next(it)
    dtb_ref = next(it)
    onw_ref = next(it)
    cbuf_refs = [next(it) for _ in range(NS)] if has_cbuf else None
    s0_refs = [next(it) for _ in range(NS)] if has_s0 else None
    y_ref = next(it)
    sfin_ref = next(it)
    ext_ref = next(it)
    scat_ref = next(it)

    c_idx = pl.program_id(1)
    nc = pl.num_programs(1)
    R = G * C
    NG = H // G
    NQK = H * DK
    pad = SUBLANES

    @pl.when(c_idx == 0)
    def _init():
        for s in range(NS):
            ext_ref[s, 0:pad, :] = jnp.zeros((pad, ext_ref.shape[2]), F32)
            if has_cbuf:
                ext_ref[s, pad - 3:pad, :] = cbuf_refs[s][0]
            for h in range(H):
                if has_s0:
                    scat_ref[s, :, h * DK:(h + 1) * DK] = s0_refs[s][0, h]
                else:
                    scat_ref[s, :, h * DK:(h + 1) * DK] = jnp.zeros((DK, DK), F32)

    for s in range(NS):
        ext_ref[s, pad:pad + C, :] = seq_refs[s][0][...]

    def conv_act(s, col):
        sl = slice(col, col + DK)
        acc = ext_ref[s, pad:pad + C, sl] * cw_ref[3:4, sl]
        for i in range(3):
            acc = acc + ext_ref[s, pad - 3 + i:pad - 3 + i + C, sl] * cw_ref[i:i + 1, sl]
        return acc * _sigmoid(acc)

    beta_full = [_sigmoid(seq_refs[s][2][...]) for s in range(NS)]
    g_full = [-jnp.exp(alog_ref[...]) * _softplus(seq_refs[s][2][...] + dtb_ref[...]) for s in range(NS)]

    iotas = (lax.broadcasted_iota(jnp.int32, (R, R), 0), lax.broadcasted_iota(jnp.int32, (R, R), 1))

    def blk(axis, m):
        return lax.shift_right_logical(iotas[axis], m.bit_length() - 1)

    same = blk(0, C) == blk(1, C)
    eye = iotas[0] == iotas[1]
    incl = same & (iotas[1] <= iotas[0])
    strict = same & (iotas[1] < iotas[0])
    incl_t = same & (iotas[0] <= iotas[1])
    rowblk = lax.shift_right_logical(lax.broadcasted_iota(jnp.int32, (R, DK), 0), C.bit_length() - 1)
    cat = (lambda xs: xs[0]) if G == 1 else (lambda xs: jnp.concatenate(xs, axis=0))
    groups = [list(range(gi * G, (gi + 1) * G)) for gi in range(NG)]

    def group_program(s, heads):
        z_ref = seq_refs[s][1]
        qs, ks, vs = [], [], []
        for h in heads:
            q = conv_act(s, h * DK)
            k = conv_act(s, NQK + h * DK)
            qs.append(q * lax.rsqrt(jnp.sum(q * q, axis=-1, keepdims=True) + EPS) * (DK ** -0.5))
            ks.append(k * lax.rsqrt(jnp.sum(k * k, axis=-1, keepdims=True) + EPS))
            vs.append(conv_act(s, 2 * NQK + h * DK))
        Q, K, V = cat(qs), cat(ks), cat(vs)
        beta = cat([beta_full[s][:, h:h + 1] for h in heads])
        g_c = cat([g_full[s][:, H + h:H + h + 1] for h in heads])
        yield
        g_r = jnp.sum(jnp.where(eye, g_c, 0.0), axis=0, keepdims=True)
        gc = jnp.sum(jnp.where(incl, g_r, 0.0), axis=1, keepdims=True)
        gc_r = jnp.sum(jnp.where(incl_t, g_c, 0.0), axis=0, keepdims=True)
        gl = jnp.sum(jnp.where(same, g_r, 0.0), axis=1, keepdims=True)
        decay = jnp.exp(jnp.where(incl, gc - gc_r, NEG))
        KB = K * beta
        lmat = jnp.where(strict, _dot_nt(KB, K) * decay, 0.0)
        amat = jnp.where(incl, _dot_nt(Q, K) * decay, 0.0)
        yield
        ld = jnp.where(blk(0, SUBLANES) == blk(1, SUBLANES), lmat, 0.0)
        ld2 = _dot(ld, ld)
        yield
        ld4 = _dot(ld2, ld2)
        x = jnp.where(eye, 1.0, 0.0) - ld
        x = x + _dot(x, ld2)
        yield
        x = x + _dot(x, ld4)
        yield
        m = SUBLANES
        while m < C:
            sel = (blk(0, 2 * m) == blk(1, 2 * m)) & (blk(0, m) != blk(1, m))
            ox = _dot(jnp.where(sel, lmat, 0.0), x)
            yield
            x = x - _dot(x, ox)
            yield
            m *= 2
        eg = jnp.exp(gc)
        UW = _dot(x, jnp.concatenate([V * beta, KB * eg], axis=1))
        yield
        sc = scat_ref[s, :, heads[0] * DK:(heads[-1] + 1) * DK]
        WQS = _dot(jnp.concatenate([UW[:, DK:], Q * eg], axis=0), sc)
        yield
        vn, oi = [], []
        for j in range(G):
            rs = slice(j * C, (j + 1) * C)
            cs = slice(j * DK, (j + 1) * DK)
            vn.append(UW[rs, :DK] - WQS[rs, cs])
            oi.append(WQS[R + j * C:R + (j + 1) * C, cs])
        Vn = cat(vn)
        O = cat(oi) + _dot(amat, Vn)
        ke = K * jnp.exp(gl - gc)
        vwide = jnp.concatenate([jnp.where(rowblk == j, Vn, 0.0) for j in range(G)], axis=1)
        dS = _dot(ke.T, vwide)
        yield
        for j, h in enumerate(heads):
            cs = slice(j * DK, (j + 1) * DK)
            e_last = jnp.exp(gl[j * C:j * C + 1, :])
            scat_ref[s, :, h * DK:(h + 1) * DK] = sc[:, cs] * e_last + dS[:, cs]
            o = O[j * C:(j + 1) * C]
            o = o * lax.rsqrt(jnp.mean(o * o, axis=-1, keepdims=True) + EPS) * onw_ref[...]
            zg = z_ref[:, h * DK:(h + 1) * DK]
            y_ref[s, :, h * DK:(h + 1) * DK] = (o * (zg * _sigmoid(zg))).astype(y_ref.dtype)

    programs = [group_program(s, heads) for s in range(NS) for heads in groups]
    while programs:
        running = []
        for prog in programs:
            try:
                next(prog)
                running.append(prog)
            except StopIteration:
                pass
        programs = running

    for s in range(NS):
        carry = ext_ref[s, C:C + pad, :]
        ext_ref[s, 0:pad, :] = carry

    @pl.when(c_idx == nc - 1)
    def _fin():
        for s in range(NS):
            for h in range(H):
                sfin_ref[s, h] = scat_ref[s, :, h * DK:(h + 1) * DK]


def gdn_core(proj, gates, conv_w, alog_pad, dtb_pad, out_norm, conv_buf, s0, *,
             row0, nb, t, C, G, H, DK, out_dtype):
    nqkv = 3 * H * DK
    nc = t // C
    assert t % C == 0 and row0 % C == 0 and H % G == 0 and C & (C - 1) == 0
    NS = _pick(nb, tuple(n for n in (4, 2, 1) if n * C <= GDN_TOKENS_PER_STEP))
    rb0 = row0 // C
    has_cbuf = conv_buf is not None
    has_s0 = s0 is not None

    def rows(s, col):
        return lambda bb, c: (rb0 + (bb * NS + s) * nc + c, col)

    const = lambda bb, c: (0, 0)
    in_specs, args = [], []
    for s in range(NS):
        in_specs += [pl.BlockSpec((C, nqkv), rows(s, 0)),
                     pl.BlockSpec((C, H * DK), rows(s, 3)),
                     pl.BlockSpec((C, LANES), rows(s, 0))]
        args += [proj, proj, gates]
    in_specs += [pl.BlockSpec((4, nqkv), const), pl.BlockSpec((1, LANES), const),
                 pl.BlockSpec((1, LANES), const), pl.BlockSpec((1, DK), const)]
    args += [conv_w, alog_pad, dtb_pad, out_norm.reshape(1, DK)]
    if has_cbuf:
        for s in range(NS):
            in_specs.append(pl.BlockSpec((1, 3, nqkv), lambda bb, c, s=s: (bb * NS + s, 0, 0)))
            args.append(conv_buf)
    if has_s0:
        for s in range(NS):
            in_specs.append(pl.BlockSpec((1, H, DK, DK), lambda bb, c, s=s: (bb * NS + s, 0, 0, 0)))
            args.append(s0)
    y, sfin = pl.pallas_call(
        functools.partial(_gdn_kernel, C=C, G=G, H=H, DK=DK, NS=NS, has_cbuf=has_cbuf, has_s0=has_s0),
        grid=(nb // NS, nc),
        in_specs=in_specs,
        out_specs=[pl.BlockSpec((NS, C, H * DK), lambda bb, c: (bb, c, 0)),
                   pl.BlockSpec((NS, H, DK, DK), lambda bb, c: (bb, 0, 0, 0))],
        out_shape=[jax.ShapeDtypeStruct((nb, t, H * DK), out_dtype),
                   jax.ShapeDtypeStruct((nb, H, DK, DK), F32)],
        scratch_shapes=[pltpu.VMEM((NS, C + SUBLANES, nqkv), F32),
                        pltpu.VMEM((NS, DK, H * DK), F32)],
        compiler_params=_cparams(("parallel", "arbitrary")),
        name="gdn_core_c%d" % C,
    )(*args)
    return y.reshape(nb * t, H * DK), sfin


def _fox_prep_kernel(q_ref, k_ref, v_ref, f_ref, qn_ref, kn_ref, fb_ref,
                     qo_ref, ko_ref, kb_ref, vo_ref, vb_ref, lf_ref, *, H, HD):
    q_scale = (HD ** -0.5) * LOG2E
    for h in range(H):
        sl = slice(h * HD, (h + 1) * HD)
        q = q_ref[:, sl]
        q = q * lax.rsqrt(jnp.mean(q * q, axis=-1, keepdims=True) + EPS) * qn_ref[...]
        qo_ref[:, sl] = (q * q_scale).astype(qo_ref.dtype)
        k = k_ref[:, sl]
        k = k * lax.rsqrt(jnp.mean(k * k, axis=-1, keepdims=True) + EPS) * kn_ref[...]
        ko_ref[:, sl] = k
        kb_ref[:, sl] = k.astype(kb_ref.dtype)
    v = v_ref[...]
    vo_ref[...] = v
    vb_ref[...] = v.astype(vb_ref.dtype)
    x = f_ref[...] + fb_ref[...]
    lf_ref[...] = jnp.minimum(x, 0.0) - jnp.log1p(jnp.exp(-jnp.abs(x)))


def fox_prep(proj, fproj, q_norm, k_norm, fb_pad, *, row0, rows, H, HD):
    d = H * HD
    tm = _pick(rows, (256, 128, 64, 32, 16, 8))
    assert row0 % tm == 0
    r0 = row0 // tm
    col = lambda j: (lambda i: (r0 + i, j))
    out = lambda i: (i, 0)
    return pl.pallas_call(
        functools.partial(_fox_prep_kernel, H=H, HD=HD),
        grid=(rows // tm,),
        in_specs=[pl.BlockSpec((tm, d), col(0)), pl.BlockSpec((tm, d), col(1)),
                  pl.BlockSpec((tm, d), col(2)), pl.BlockSpec((tm, LANES), col(0)),
                  pl.BlockSpec((1, HD), lambda i: (0, 0)), pl.BlockSpec((1, HD), lambda i: (0, 0)),
                  pl.BlockSpec((1, LANES), lambda i: (0, 0))],
        out_specs=[pl.BlockSpec((tm, d), out)] * 5 + [pl.BlockSpec((tm, LANES), out)],
        out_shape=[jax.ShapeDtypeStruct((rows, d), BF16), jax.ShapeDtypeStruct((rows, d), F32),
                   jax.ShapeDtypeStruct((rows, d), BF16), jax.ShapeDtypeStruct((rows, d), F32),
                   jax.ShapeDtypeStruct((rows, d), BF16), jax.ShapeDtypeStruct((rows, LANES), F32)],
        compiler_params=_cparams(("parallel",)),
        name="fox_prep",
    )(proj, proj, proj, fproj, q_norm.reshape(1, HD), k_norm.reshape(1, HD), fb_pad)


def _cumsum_block(x, carry):
    n = x.shape[0]
    ri = lax.broadcasted_iota(jnp.int32, (n, n), 0)
    ci = lax.broadcasted_iota(jnp.int32, (n, n), 1)
    tril = jnp.where(ci <= ri, 1.0, 0.0)
    return _dot_exact_lhs01(tril, x) + carry


def _cumsum_kernel(x_ref, o_ref, carry_ref):
    @pl.when(pl.program_id(1) == 0)
    def _():
        carry_ref[...] = jnp.zeros_like(carry_ref)

    cs = _cumsum_block(x_ref[0], carry_ref[...])
    o_ref[0] = cs
    carry_ref[...] = cs[cs.shape[0] - 1:, :]


def cumsum_rows(x):
    b, t, w = x.shape
    tb = _pick(t, (256, 128, 64, 32, 16, 8))
    return pl.pallas_call(
        _cumsum_kernel,
        grid=(b, t // tb),
        in_specs=[pl.BlockSpec((1, tb, w), lambda i, j: (i, j, 0))],
        out_specs=pl.BlockSpec((1, tb, w), lambda i, j: (i, j, 0)),
        out_shape=jax.ShapeDtypeStruct((b, t, w), F32),
        scratch_shapes=[pltpu.VMEM((1, w), F32)],
        compiler_params=_cparams(("parallel", "arbitrary")),
        name="cumsum_rows",
    )(x)


def _cumsum_lanes(x, carry):
    n = x.shape[1]
    ri = lax.broadcasted_iota(jnp.int32, (n, n), 0)
    ci = lax.broadcasted_iota(jnp.int32, (n, n), 1)
    triu = jnp.where(ri <= ci, 1.0, 0.0).astype(BF16)
    hi, mid, lo = _split3(x)
    d = functools.partial(jnp.dot, preferred_element_type=F32)
    return d(hi, triu) + (d(mid, triu) + d(lo, triu)) + carry


def _paged_cumsum_kernel(pt_ref, *refs, pps):
    lp_refs = refs[:pps]
    ln_ref, o_ref, on_ref, carry_ref = refs[pps:]
    p = pl.program_id(1)

    @pl.when(p == 0)
    def _():
        carry_ref[...] = jnp.zeros_like(carry_ref)

    h = lp_refs[0].shape[1]
    local = _cumsum_lanes(jnp.concatenate([r[0] for r in lp_refs], axis=0) if pps > 1 else lp_refs[0][0], 0.0)
    carry = carry_ref[...]
    for j in range(pps):
        lj = local[j * h:(j + 1) * h]
        o_ref[0, j] = lj + carry
        carry = carry + lj[:, lj.shape[1] - 1:]
    carry_ref[...] = carry

    @pl.when(p == pl.num_programs(1) - 1)
    def _():
        on_ref[0] = _cumsum_lanes(ln_ref[0], carry)


def paged_cumsum(page_table_flat, logf_hp, lf_new_hp, *, layer_off, nb, npages, page, H):
    pps = _pick(npages, CUMSUM_PAGES_PER_STEP)
    ts = lf_new_hp.shape[2]
    assert ts == page

    def pool_map(j):
        return lambda b, p, pt: (layer_off + pt[b * npages + p * pps + j], 0, 0)

    return pl.pallas_call(
        functools.partial(_paged_cumsum_kernel, pps=pps),
        grid_spec=pltpu.PrefetchScalarGridSpec(
            num_scalar_prefetch=1,
            grid=(nb, npages // pps),
            in_specs=[pl.BlockSpec((1, H, page), pool_map(j)) for j in range(pps)]
            + [pl.BlockSpec((1, H, ts), lambda b, p, pt: (b, 0, 0))],
            out_specs=[pl.BlockSpec((1, pps, H, page), lambda b, p, pt: (b, p, 0, 0)),
                       pl.BlockSpec((1, H, ts), lambda b, p, pt: (b, 0, 0))],
            scratch_shapes=[pltpu.VMEM((H, 1), F32)],
        ),
        out_shape=[jax.ShapeDtypeStruct((nb, npages, H, page), F32),
                   jax.ShapeDtypeStruct((nb, H, ts), F32)],
        compiler_params=_cparams(("parallel", "arbitrary")),
        name="paged_cumsum",
    )(page_table_flat, *([logf_hp] * pps), lf_new_hp)


def _fox_prompt_kernel(q_ref, k_ref, v_ref, c_ref, g_ref, o_ref, *, tq):
    qi = pl.program_id(2)
    q = q_ref[...]
    hd = q.shape[1]

    def block(j, carry, diag):
        m, l, acc = carry
        start = pl.multiple_of(j * tq, tq)
        k = k_ref[pl.ds(start, tq), :]
        v = v_ref[pl.ds(start, tq), :]
        s = _dot_nt(q, k) - c_ref[0, 0, :, pl.ds(start, tq)]
        if diag:
            rpos = lax.broadcasted_iota(jnp.int32, (tq, tq), 0)
            cpos = lax.broadcasted_iota(jnp.int32, (tq, tq), 1)
            s = jnp.where(cpos <= rpos, s, NEG)
        m_new = jnp.maximum(m, jnp.max(s, axis=1, keepdims=True))
        alpha = jnp.exp2(m - m_new)
        p = jnp.exp2(s - m_new)
        l = alpha * l + jnp.sum(p, axis=1, keepdims=True)
        acc = alpha * acc + _dot(p, v)
        return m_new, l, acc

    init = (jnp.full((tq, 1), NEG, F32), jnp.zeros((tq, 1), F32), jnp.zeros((tq, hd), F32))
    carry = lax.fori_loop(0, qi, lambda j, c: block(j, c, False), init)
    m, l, acc = block(qi, carry, True)
    g = g_ref[...]
    o_ref[...] = ((acc / l) * _sigmoid(g)).astype(o_ref.dtype)


def fox_prompt_attention(qb, kb, vb, c_row, proj, *, nb, t, H, HD):
    tq = _pick(t, ATTN_TQ)
    nq = t // tq
    return pl.pallas_call(
        functools.partial(_fox_prompt_kernel, tq=tq),
        grid=(nb, H, nq),
        in_specs=[pl.BlockSpec((tq, HD), lambda b, h, i: (b * nq + i, h)),
                  pl.BlockSpec((t, HD), lambda b, h, i: (b, h)),
                  pl.BlockSpec((t, HD), lambda b, h, i: (b, h)),
                  pl.BlockSpec((1, 1, 1, t), lambda b, h, i: (b, h, 0, 0)),
                  pl.BlockSpec((tq, HD), lambda b, h, i: (b * nq + i, 3 * H + h))],
        out_specs=pl.BlockSpec((tq, HD), lambda b, h, i: (b * nq + i, h)),
        out_shape=jax.ShapeDtypeStruct((nb * t, H * HD), BF16),
        compiler_params=_cparams(("parallel", "parallel", "arbitrary")),
        name="fox_prompt_attn",
    )(qb, kb, vb, c_row, proj)


def _fox_decode_kernel(pt_ref, *refs, pps, H, HD, NQ):
    q_ref = refs[0]
    kp_refs = refs[1:1 + pps]
    vp_refs = refs[1 + pps:1 + 2 * pps]
    (kn_ref, vn_ref, cum_ref, cn_ref, g_ref,
     o_ref, qcat_ref, mask_ref, m_ref, acc_ref, s_ref, vx_ref) = refs[1 + 2 * pps:]
    p = pl.program_id(1)
    rows = H * NQ
    kc = mask_ref.shape[1]
    width = kp_refs[0].shape[1]

    @pl.when(p == 0)
    def _():
        q = q_ref[0].astype(F32)
        for h in range(H):
            qcat_ref[h * NQ:(h + 1) * NQ, :] = q[:, h * HD:(h + 1) * HD]
        rh = lax.broadcasted_iota(jnp.int32, (rows, kc), 0) // NQ
        ch = lax.broadcasted_iota(jnp.int32, (rows, kc), 1) % H
        mask_ref[...] = jnp.where(rh == ch, 0.0, NEG)
        m_ref[...] = jnp.full_like(m_ref, NEG)
        acc_ref[...] = jnp.zeros_like(acc_ref)
        vx_ref[:, :, HD:] = jnp.ones((vx_ref.shape[0], width, HD), vx_ref.dtype)

    qc = qcat_ref[...].astype(BF16)

    chunks = range(0, width, kc)
    slots = vx_ref.shape[0]

    def scores(j, c0, mx):
        s = _dot_nt(qc, kp_refs[j][0, c0:c0 + kc, :]) + (mask_ref[...] - cum_ref[0, j, :, c0:c0 + kc])
        s_ref[:, (j % slots) * width + c0:(j % slots) * width + c0 + kc] = s
        return jnp.maximum(mx, s)

    def weighted(j, c0, m_new, pv):
        pr = jnp.exp2(s_ref[:, (j % slots) * width + c0:(j % slots) * width + c0 + kc] - m_new)
        return pv + _dot(pr, vx_ref[j % slots, c0:c0 + kc, :])

    def stage_values(j):
        vx_ref[j % slots, :, :HD] = vp_refs[j][0].astype(vx_ref.dtype)

    m, acc = m_ref[...], acc_ref[...]
    neg = jnp.full((rows, kc), NEG, F32)
    stage_values(0)
    mx = neg
    for c0 in chunks:
        mx = scores(0, c0, mx)
    for j in range(pps):
        m_new = jnp.maximum(m, jnp.max(mx, axis=1, keepdims=True))
        pv = jnp.zeros((rows, 2 * HD), F32)
        nxt = j + 1 < pps
        if nxt:
            stage_values(j + 1)
        mx = neg
        for c0 in chunks:
            pv = weighted(j, c0, m_new, pv)
            if nxt:
                mx = scores(j + 1, c0, mx)
        acc = jnp.exp2(m - m_new) * acc + pv
        m = m_new
    m_ref[...], acc_ref[...] = m, acc

    @pl.when(p == pl.num_programs(1) - 1)
    def _():
        wn = kn_ref.shape[1]
        r = lax.broadcasted_iota(jnp.int32, (rows, wn), 0)
        c = lax.broadcasted_iota(jnp.int32, (rows, wn), 1)
        ok = (r // NQ == c % H) & (c // H <= r % NQ)
        s = jnp.where(ok, _dot_nt(qc, kn_ref[0]) - cn_ref[0], NEG)
        m_fin = jnp.maximum(m, jnp.max(s, axis=1, keepdims=True))
        vn = vn_ref[0]
        fin = (jnp.exp2(m - m_fin) * acc
               + _dot(jnp.exp2(s - m_fin), jnp.concatenate([vn, jnp.ones_like(vn)], axis=1)))
        out = fin[:, :HD] / fin[:, HD:]
        for h in range(H):
            cs = slice(h * HD, (h + 1) * HD)
            o_ref[:, cs] = (out[h * NQ:(h + 1) * NQ, :] * _sigmoid(g_ref[:, cs])).astype(o_ref.dtype)


def fox_decode_attention(page_table_flat, q_s, cache_k, cache_v, kn, vn, cum, cnew, gate_src, *,
                         layer_off, nb, npages, page, H, HD, NQ, gate_rb0):
    d = H * HD
    pps = _pick(npages, DECODE_PAGES_PER_STEP)
    rows = H * NQ
    kc = _pick(page * H, DECODE_KEY_CHUNK)
    slots = min(2, pps)
    assert kc % H == 0 and (page * H) % kc == 0

    def pool_map(j):
        return lambda b, p, pt: (layer_off + pt[b * npages + p * pps + j], 0, 0)

    per_b = lambda b, p, pt: (b, 0, 0)
    pool_specs = [pl.BlockSpec((1, page * H, HD), pool_map(j)) for j in range(pps)]
    return pl.pallas_call(
        functools.partial(_fox_decode_kernel, pps=pps, H=H, HD=HD, NQ=NQ),
        grid_spec=pltpu.PrefetchScalarGridSpec(
            num_scalar_prefetch=1,
            grid=(nb, npages // pps),
            in_specs=[pl.BlockSpec((1, NQ, d), per_b)] + pool_specs + pool_specs
            + [pl.BlockSpec((1, NQ * H, HD), per_b),
               pl.BlockSpec((1, NQ * H, HD), per_b),
               pl.BlockSpec((1, pps, 1, page * H), lambda b, p, pt: (b, p, 0, 0)),
               pl.BlockSpec((1, 1, NQ * H), per_b),
               pl.BlockSpec((NQ, d), lambda b, p, pt: (gate_rb0 + b, 3))],
            out_specs=pl.BlockSpec((NQ, d), lambda b, p, pt: (b, 0)),
            scratch_shapes=[pltpu.VMEM((rows, HD), F32),
                            pltpu.VMEM((rows, kc), F32),
                            pltpu.VMEM((rows, 1), F32),
                            pltpu.VMEM((rows, 2 * HD), F32),
                            pltpu.VMEM((rows, slots * page * H), F32),
                            pltpu.VMEM((slots, page * H, 2 * HD), BF16)],
        ),
        out_shape=jax.ShapeDtypeStruct((nb * NQ, d), F32),
        compiler_params=_cparams(("parallel", "arbitrary")),
        name="fox_decode_attn",
    )(page_table_flat, q_s, *([cache_k] * pps), *([cache_v] * pps), kn, vn, cum, cnew, gate_src)


def _pad_lanes(v, start=0):
    out = jnp.zeros((1, LANES), F32)
    return lax.dynamic_update_slice(out, v.reshape(1, -1).astype(F32), (0, start))


def kernel(x_prompt, x_sample, state_delta, state_conv, cache_k, cache_v, cache_logf, page_table,
           norm_mix, norm_mlp, a_w_in, a_conv, a_A_log, a_dt_bias, a_out_norm, a_w_out,
           b_w_in, b_f_bias, b_q_norm, b_k_norm, b_w_out, w_up, w_down):
    bp, t, d = x_prompt.shape
    bs, ts, _ = x_sample.shape
    depth = norm_mix.shape[0]
    n_mixers = 2
    H = state_delta.shape[2]
    DK = state_delta.shape[3]
    nqkv = state_conv.shape[-1]
    HB = cache_k.shape[3]
    HD = cache_k.shape[4]
    n_pool, page = cache_k.shape[1], cache_k.shape[2]
    npages = page_table.shape[1]
    d_ff = w_up.shape[2]
    mp, ms = bp * t, bs * ts
    assert H * DK == d and HB * HD == d and nqkv == 3 * d and DK == LANES and HD == LANES
    assert 2 * H <= LANES and HB <= LANES and 3 <= ts <= page and t >= 3

    x = jnp.concatenate([x_prompt.reshape(mp, d), x_sample.reshape(ms, d)], axis=0)
    pt_flat = page_table.reshape(-1).astype(jnp.int32)
    a_w_in_t = jnp.swapaxes(a_w_in, 1, 2)
    b_w_in_t = jnp.swapaxes(b_w_in, 1, 2)
    pad_rows = lambda w: jnp.pad(w, ((0, 0), (0, LANES - w.shape[1]), (0, 0)))
    a_w_gate_t = pad_rows(a_w_in_t[:, nqkv + d:, :])
    b_w_gate_t = pad_rows(b_w_in_t[:, 4 * d:, :])
    w_down_bf = w_down.astype(BF16)
    logf_hp = jnp.swapaxes(cache_logf, 2, 3).reshape(-1, HB, page)

    p_delta, p_conv, s_delta, s_conv = [], [], [], []
    p_k, p_v, p_lf, s_k, s_v, s_lf = [], [], [], [], [], []
    for i in range(depth):
        j = i // n_mixers
        hn = rmsnorm_bf16(x, norm_mix[i])
        if i % n_mixers == 0:
            proj = matmul_wcast(hn, a_w_in_t, j, n_out=nqkv + d, w_transposed=True)
            gates = matmul_wcast(hn, a_w_gate_t, j, n_out=LANES, w_transposed=True)
            alog_pad = _pad_lanes(a_A_log[j], H)
            dtb_pad = _pad_lanes(a_dt_bias[j], H)
            common = dict(H=H, DK=DK)
            cp = 64 if t % 64 == 0 else t
            yp, stp = gdn_core(proj, gates, a_conv[j], alog_pad, dtb_pad, a_out_norm[j], None, None,
                               row0=0, nb=bp, t=t, C=cp, G=max(1, min(H, GDN_STACK_ROWS // cp)),
                               out_dtype=BF16, **common)
            ys, sts = gdn_core(proj, gates, a_conv[j], alog_pad, dtb_pad, a_out_norm[j],
                               state_conv[j], state_delta[j],
                               row0=mp, nb=bs, t=ts, C=ts, G=H, out_dtype=F32, **common)
            x = matmul_wcast(yp, a_w_out, j, n_out=d, epilogue="residual", residual=x, row0=0)
            x = matmul_wcast(ys.astype(BF16), a_w_out, j, n_out=d, epilogue="residual", residual=x, row0=mp)
            p_conv.append(jnp.stack([proj[b * t + t - 3:(b + 1) * t, :nqkv] for b in range(bp)]))
            s_conv.append(proj[mp:].reshape(bs, ts, -1)[:, ts - 3:, :nqkv])
            p_delta.append(stp)
            s_delta.append(sts)
        else:
            proj = matmul_wcast(hn, b_w_in_t, j, n_out=4 * d, w_transposed=True)
            fproj = matmul_wcast(hn, b_w_gate_t, j, n_out=LANES, w_transposed=True)
            prep = functools.partial(fox_prep, proj, fproj, b_q_norm[j], b_k_norm[j],
                                     _pad_lanes(b_f_bias[j]), H=HB, HD=HD)
            qb_p, kf_p, kb_p, vf_p, vb_p, lf_p = prep(row0=0, rows=mp)
            qb_s, kf_s, kb_s, vf_s, vb_s, lf_s = prep(row0=mp, rows=ms)
            c_p = cumsum_rows(lf_p.reshape(bp, t, LANES))
            c_row = jnp.transpose(c_p[:, :, :HB], (0, 2, 1)).reshape(bp, HB, 1, t) * LOG2E
            op = fox_prompt_attention(qb_p, kb_p, vb_p, c_row, proj, nb=bp, t=t, H=HB, HD=HD)
            lf_new_hp = jnp.swapaxes(lf_s[:, :HB].reshape(bs, ts, HB), 1, 2)
            lf_new_hp = jnp.pad(lf_new_hp, ((0, 0), (0, 0), (0, page - ts)))
            cum_hp, cnew_hp = paged_cumsum(pt_flat, logf_hp, lf_new_hp, layer_off=j * n_pool,
                                           nb=bs, npages=npages, page=page, H=HB)
            cum = jnp.swapaxes(cum_hp, 2, 3).reshape(bs, npages, 1, page * HB) * LOG2E
            cnew = jnp.swapaxes(cnew_hp[:, :, :ts], 1, 2).reshape(bs, 1, ts * HB) * LOG2E
            os_ = fox_decode_attention(
                pt_flat, qb_s.reshape(bs, ts, d),
                cache_k.reshape(-1, page * HB, HD), cache_v.reshape(-1, page * HB, HD),
                kb_s.reshape(bs, ts * HB, HD), vb_s.reshape(bs, ts * HB, HD),
                cum, cnew, proj,
                layer_off=j * n_pool, nb=bs, npages=npages, page=page, H=HB, HD=HD, NQ=ts,
                gate_rb0=mp // ts)
            x = matmul_wcast(op, b_w_out, j, n_out=d, epilogue="residual", residual=x, row0=0)
            x = matmul_wcast(os_.astype(BF16), b_w_out, j, n_out=d, epilogue="residual", residual=x, row0=mp)
            p_k.append(kf_p.reshape(bp, t, HB, HD))
            p_v.append(vf_p.reshape(bp, t, HB, HD))
            p_lf.append(lf_p[:, :HB].reshape(bp, t, HB))
            s_k.append(kf_s.reshape(bs, ts, HB, HD))
            s_v.append(vf_s.reshape(bs, ts, HB, HD))
            s_lf.append(lf_s[:, :HB].reshape(bs, ts, HB))
        hm = rmsnorm_bf16(x, norm_mlp[i])
        a = matmul_wcast(hm, w_up, i, n_out=d_ff, epilogue="relu2", out_dtype=BF16)
        down = functools.partial(matmul_fullk_residual, a, w_down_bf, i, x)
        if i + 1 < depth:
            x = down(row0=0, rows=mp + ms)
    y_p, y_s = down(row0=0, rows=mp), down(row0=mp, rows=ms)
    return (y_p.reshape(bp, t, d), y_s.reshape(bs, ts, d),
            jnp.stack(p_delta), jnp.stack(p_conv), jnp.stack(p_k), jnp.stack(p_v), jnp.stack(p_lf),
            jnp.stack(s_delta), jnp.stack(s_conv), jnp.stack(s_k), jnp.stack(s_v), jnp.stack(s_lf))
```

```python
import functools

import jax
import jax.numpy as jnp
from jax import lax
from jax.experimental import pallas as pl
from jax.experimental.pallas import tpu as pltpu

F32 = jnp.float32
BF16 = jnp.bfloat16
EPS = 1e-6
NEG = -1e30
LOG2E = 1.4426950408889634
LANES = 128
SUBLANES = 8
MXU_DIM = 256
VMEM_LIMIT = 56 * 1024 * 1024

ROW_TILES = (768, 512, 528, 384, 256, 128, 64, 32, 16, 8)
WCAST_ROW_TILES = (1056,) + ROW_TILES
COL_TILE = 1024
FULLK_COL_TILE = 512
ATTN_TQ = (512, 256, 128)
ATTN_TILES_PER_STEP = (4, 2, 1)
DECODE_PAGES_PER_STEP = (8, 4, 2, 1)
DECODE_KEY_CHUNK = (256, 128)
CUMSUM_PAGES_PER_STEP = (32, 16, 8, 4, 2, 1)
GDN_TOKENS_PER_STEP = 128
GDN_STACK_ROWS = 128

def _pick(n, cands):
    for c in cands:
        if n % c == 0:
            return c
    return n


def _cparams(sem):
    return pltpu.CompilerParams(dimension_semantics=sem, vmem_limit_bytes=VMEM_LIMIT)


def _dot(a, b):
    return jnp.dot(a.astype(BF16), b.astype(BF16), preferred_element_type=F32)


def _dot_nt(a, b):
    return lax.dot_general(a.astype(BF16), b.astype(BF16), (((1,), (1,)), ((), ())),
                           preferred_element_type=F32)


def _split3(x):
    hi = x.astype(BF16)
    r = x - hi.astype(F32)
    mid = r.astype(BF16)
    lo = (r - mid.astype(F32)).astype(BF16)
    return hi, mid, lo


def _dot_exact_lhs01(a01, b):
    hi, mid, lo = _split3(b)
    a = a01.astype(BF16)
    d = functools.partial(jnp.dot, preferred_element_type=F32)
    return d(a, hi) + (d(a, mid) + d(a, lo))


def _sigmoid(x):
    return 1.0 / (1.0 + jnp.exp(-x))


def _softplus(x):
    return jnp.maximum(x, 0.0) + jnp.log(1.0 + jnp.exp(-jnp.abs(x)))


def _rmsnorm_kernel(x_ref, w_ref, o_ref):
    x = x_ref[...]
    ms = jnp.mean(x * x, axis=-1, keepdims=True)
    o_ref[...] = (x * lax.rsqrt(ms + EPS) * w_ref[...]).astype(o_ref.dtype)


def rmsnorm_bf16(x, w):
    m, d = x.shape
    tm = _pick(m, ROW_TILES)
    return pl.pallas_call(
        _rmsnorm_kernel,
        grid=(m // tm,),
        in_specs=[pl.BlockSpec((tm, d), lambda i: (i, 0)),
                  pl.BlockSpec((1, d), lambda i: (0, 0))],
        out_specs=pl.BlockSpec((tm, d), lambda i: (i, 0)),
        out_shape=jax.ShapeDtypeStruct((m, d), BF16),
        compiler_params=_cparams(("parallel",)),
        name="rmsnorm",
    )(x, w.reshape(1, d))


def _epilogue(acc, kind, r_ref):
    if kind == "relu2":
        acc = jnp.maximum(acc, 0.0)
        acc = acc * acc
    elif kind == "residual":
        acc = acc + r_ref[...]
    return acc


def _mm_wcast_kernel(*refs, epilogue, w_transposed):
    if epilogue == "residual":
        a_ref, w_ref, r_ref, o_ref, wb_ref = refs
    else:
        a_ref, w_ref, o_ref, wb_ref = refs
        r_ref = None

    @pl.when(pl.program_id(1) == 0)
    def _():
        wb_ref[...] = w_ref[...].astype(wb_ref.dtype)

    if w_transposed:
        acc = _dot_nt(a_ref[...], wb_ref[...])
    else:
        acc = jnp.dot(a_ref[...], wb_ref[...], preferred_element_type=F32)
    o_ref[...] = _epilogue(acc, epilogue, r_ref).astype(o_ref.dtype)


def matmul_wcast(a, w, layer, *, n_out, w_transposed=False, epilogue="none", residual=None,
                 out_dtype=F32, row0=0):
    ma, kdim = a.shape
    tm = _pick(ma, WCAST_ROW_TILES if epilogue != "residual" else (1024, 512, 256, 128, 64, 32, 16, 8))
    tn = min(COL_TILE, n_out)
    assert n_out % tn == 0 and row0 % tm == 0 and w.shape[2 if w_transposed else 1] == kdim
    r0 = row0 // tm
    if w_transposed:
        w_spec = pl.BlockSpec((None, tn, kdim), lambda j, i: (layer, j, 0))
        wb_shape = (tn, kdim)
    else:
        w_spec = pl.BlockSpec((None, kdim, tn), lambda j, i: (layer, 0, j))
        wb_shape = (kdim, tn)
    in_specs = [pl.BlockSpec((tm, kdim), lambda j, i: (i, 0)), w_spec]
    args = [a, w]
    aliases = {}
    if epilogue == "residual":
        in_specs.append(pl.BlockSpec((tm, tn), lambda j, i: (r0 + i, j)))
        args.append(residual)
        aliases = {2: 0}
        out_rows = residual.shape[0]
    else:
        out_rows = ma
    return pl.pallas_call(
        functools.partial(_mm_wcast_kernel, epilogue=epilogue, w_transposed=w_transposed),
        grid=(n_out // tn, ma // tm),
        in_specs=in_specs,
        out_specs=pl.BlockSpec((tm, tn), lambda j, i: (r0 + i, j)),
        out_shape=jax.ShapeDtypeStruct((out_rows, n_out), out_dtype),
        scratch_shapes=[pltpu.VMEM(wb_shape, BF16)],
        input_output_aliases=aliases,
        compiler_params=_cparams(("arbitrary", "arbitrary")),
        name="matmul_wcast_" + epilogue,
    )(*args)


def _mm_fullk_kernel(a_ref, b_ref, r_ref, o_ref):
    acc = jnp.dot(a_ref[...], b_ref[...], preferred_element_type=F32)
    o_ref[...] = (acc + r_ref[...]).astype(o_ref.dtype)


def matmul_fullk_residual(a, b, layer, residual, *, row0, rows):
    _, kdim = a.shape
    n = b.shape[2]
    tm = _pick(rows, ROW_TILES)
    tn = min(FULLK_COL_TILE, n)
    assert n % tn == 0 and row0 % tm == 0
    r0 = row0 // tm
    return pl.pallas_call(
        _mm_fullk_kernel,
        grid=(rows // tm, n // tn),
        in_specs=[pl.BlockSpec((tm, kdim), lambda i, j: (r0 + i, 0)),
                  pl.BlockSpec((None, kdim, tn), lambda i, j: (layer, 0, j)),
                  pl.BlockSpec((tm, tn), lambda i, j: (r0 + i, j))],
        out_specs=pl.BlockSpec((tm, tn), lambda i, j: (i, j)),
        out_shape=jax.ShapeDtypeStruct((rows, n), F32),
        compiler_params=_cparams(("parallel", "arbitrary")),
        name="matmul_fullk_residual",
    )(a, b, residual)


def _gdn_kernel(*refs, C, G, H, DK, NS, has_cbuf, has_s0):
    it = iter(refs)
    seq_refs = [(next(it), next(it), next(it)) for _ in range(NS)]
    cw_ref = next(it)
    alog_ref = next(it)
    dtb_ref = next(it)
    onw_ref = next(it)
    cbuf_refs = [next(it) for _ in range(NS)] if has_cbuf else None
    s0_refs = [next(it) for _ in range(NS)] if has_s0 else None
    y_ref = next(it)
    sfin_ref = next(it)
    ext_ref = next(it)
    scat_ref = next(it)

    c_idx = pl.program_id(1)
    nc = pl.num_programs(1)
    R = G * C
    NG = H // G
    NQK = H * DK
    pad = SUBLANES

    @pl.when(c_idx == 0)
    def _init():
        for s in range(NS):
            ext_ref[s, 0:pad, :] = jnp.zeros((pad, ext_ref.shape[2]), F32)
            if has_cbuf:
                ext_ref[s, pad - 3:pad, :] = cbuf_refs[s][0]
            for h in range(H):
                if has_s0:
                    scat_ref[s, :, h * DK:(h + 1) * DK] = s0_refs[s][0, h]
                else:
                    scat_ref[s, :, h * DK:(h + 1) * DK] = jnp.zeros((DK, DK), F32)

    for s in range(NS):
        ext_ref[s, pad:pad + C, :] = seq_refs[s][0][...]

    def conv_act(s, col):
        sl = slice(col, col + DK)
        acc = ext_ref[s, pad:pad + C, sl] * cw_ref[3:4, sl]
        for i in range(3):
            acc = acc + ext_ref[s, pad - 3 + i:pad - 3 + i + C, sl] * cw_ref[i:i + 1, sl]
        return acc * _sigmoid(acc)

    beta_full = [_sigmoid(seq_refs[s][2][...]) for s in range(NS)]
    g_full = [-jnp.exp(alog_ref[...]) * _softplus(seq_refs[s][2][...] + dtb_ref[...]) for s in range(NS)]

    iotas = (lax.broadcasted_iota(jnp.int32, (R, R), 0), lax.broadcasted_iota(jnp.int32, (R, R), 1))

    def blk(axis, m):
        return lax.shift_right_logical(iotas[axis], m.bit_length() - 1)

    same = blk(0, C) == blk(1, C)
    eye = iotas[0] == iotas[1]
    incl = same & (iotas[1] <= iotas[0])
    strict = same & (iotas[1] < iotas[0])
    incl_t = same & (iotas[0] <= iotas[1])
    rowblk = lax.shift_right_logical(lax.broadcasted_iota(jnp.int32, (R, DK), 0), C.bit_length() - 1)
    cat = (lambda xs: xs[0]) if G == 1 else (lambda xs: jnp.concatenate(xs, axis=0))
    groups = [list(range(gi * G, (gi + 1) * G)) for gi in range(NG)]

    def group_program(s, heads):
        z_ref = seq_refs[s][1]
        qs, ks, vs = [], [], []
        for h in heads:
            q = conv_act(s, h * DK)
            k = conv_act(s, NQK + h * DK)
            qs.append(q * lax.rsqrt(jnp.sum(q * q, axis=-1, keepdims=True) + EPS) * (DK ** -0.5))
            ks.append(k * lax.rsqrt(jnp.sum(k * k, axis=-1, keepdims=True) + EPS))
            vs.append(conv_act(s, 2 * NQK + h * DK))
        Q, K, V = cat(qs), cat(ks), cat(vs)
        beta = cat([beta_full[s][:, h:h + 1] for h in heads])
        g_c = cat([g_full[s][:, H + h:H + h + 1] for h in heads])
        yield
        g_r = jnp.sum(jnp.where(eye, g_c, 0.0), axis=0, keepdims=True)
        gc = jnp.sum(jnp.where(incl, g_r, 0.0), axis=1, keepdims=True)
        gc_r = jnp.sum(jnp.where(incl_t, g_c, 0.0), axis=0, keepdims=True)
        gl = jnp.sum(jnp.where(same, g_r, 0.0), axis=1, keepdims=True)
        decay = jnp.exp(jnp.where(incl, gc - gc_r, NEG))
        KB = K * beta
        lmat = jnp.where(strict, _dot_nt(KB, K) * decay, 0.0)
        amat = jnp.where(incl, _dot_nt(Q, K) * decay, 0.0)
        yield
        ld = jnp.where(blk(0, SUBLANES) == blk(1, SUBLANES), lmat, 0.0)
        ld2 = _dot(ld, ld)
        yield
        ld4 = _dot(ld2, ld2)
        x = jnp.where(eye, 1.0, 0.0) - ld
        x = x + _dot(x, ld2)
        yield
        x = x + _dot(x, ld4)
        yield
        m = SUBLANES
        while m < C:
            sel = (blk(0, 2 * m) == blk(1, 2 * m)) & (blk(0, m) != blk(1, m))
            ox = _dot(jnp.where(sel, lmat, 0.0), x)
            yield
            x = x - _dot(x, ox)
            yield
            m *= 2
        eg = jnp.exp(gc)
        UW = _dot(x, jnp.concatenate([V * beta, KB * eg], axis=1))
        yield
        sc = scat_ref[s, :, heads[0] * DK:(heads[-1] + 1) * DK]
        WQS = _dot(jnp.concatenate([UW[:, DK:], Q * eg], axis=0), sc)
        yield
        vn, oi = [], []
        for j in range(G):
            rs = slice(j * C, (j + 1) * C)
            cs = slice(j * DK, (j + 1) * DK)
            vn.append(UW[rs, :DK] - WQS[rs, cs])
            oi.append(WQS[R + j * C:R + (j + 1) * C, cs])
        Vn = cat(vn)
        O = cat(oi) + _dot(amat, Vn)
        ke = K * jnp.exp(gl - gc)
        vwide = jnp.concatenate([jnp.where(rowblk == j, Vn, 0.0) for j in range(G)], axis=1)
        dS = _dot(ke.T, vwide)
        yield
        for j, h in enumerate(heads):
            cs = slice(j * DK, (j + 1) * DK)
            e_last = jnp.exp(gl[j * C:j * C + 1, :])
            scat_ref[s, :, h * DK:(h + 1) * DK] = sc[:, cs] * e_last + dS[:, cs]
            o = O[j * C:(j + 1) * C]
            o = o * lax.rsqrt(jnp.mean(o * o, axis=-1, keepdims=True) + EPS) * onw_ref[...]
            zg = z_ref[:, h * DK:(h + 1) * DK]
            y_ref[s, :, h * DK:(h + 1) * DK] = (o * (zg * _sigmoid(zg))).astype(y_ref.dtype)

    programs = [group_program(s, heads) for s in range(NS) for heads in groups]
    while programs:
        running = []
        for prog in programs:
            try:
                next(prog)
                running.append(prog)
            except StopIteration:
                pass
        programs = running

    for s in range(NS):
        carry = ext_ref[s, C:C + pad, :]
        ext_ref[s, 0:pad, :] = carry

    @pl.when(c_idx == nc - 1)
    def _fin():
        for s in range(NS):
            for h in range(H):
                sfin_ref[s, h] = scat_ref[s, :, h * DK:(h + 1) * DK]


def gdn_core(proj, gates, conv_w, alog_pad, dtb_pad, out_norm, conv_buf, s0, *,
             row0, nb, t, C, G, H, DK, out_dtype):
    nqkv = 3 * H * DK
    nc = t // C
    assert t % C == 0 and row0 % C == 0 and H % G == 0 and C & (C - 1) == 0
    NS = _pick(nb, tuple(n for n in (4, 2, 1) if n * C <= GDN_TOKENS_PER_STEP))
    rb0 = row0 // C
    has_cbuf = conv_buf is not None
    has_s0 = s0 is not None

    def rows(s, col):
        return lambda bb, c: (rb0 + (bb * NS + s) * nc + c, col)

    const = lambda bb, c: (0, 0)
    in_specs, args = [], []
    for s in range(NS):
        in_specs += [pl.BlockSpec((C, nqkv), rows(s, 0)),
                     pl.BlockSpec((C, H * DK), rows(s, 3)),
                     pl.BlockSpec((C, LANES), rows(s, 0))]
        args += [proj, proj, gates]
    in_specs += [pl.BlockSpec((4, nqkv), const), pl.BlockSpec((1, LANES), const),
                 pl.BlockSpec((1, LANES), const), pl.BlockSpec((1, DK), const)]
    args += [conv_w, alog_pad, dtb_pad, out_norm.reshape(1, DK)]
    if has_cbuf:
        for s in range(NS):
            in_specs.append(pl.BlockSpec((1, 3, nqkv), lambda bb, c, s=s: (bb * NS + s, 0, 0)))
            args.append(conv_buf)
    if has_s0:
        for s in range(NS):
            in_specs.append(pl.BlockSpec((1, H, DK, DK), lambda bb, c, s=s: (bb * NS + s, 0, 0, 0)))
            args.append(s0)
    y, sfin = pl.pallas_call(
        functools.partial(_gdn_kernel, C=C, G=G, H=H, DK=DK, NS=NS, has_cbuf=has_cbuf, has_s0=has_s0),
        grid=(nb // NS, nc),
        in_specs=in_specs,
        out_specs=[pl.BlockSpec((NS, C, H * DK), lambda bb, c: (bb, c, 0)),
                   pl.BlockSpec((NS, H, DK, DK), lambda bb, c: (bb, 0, 0, 0))],
        out_shape=[jax.ShapeDtypeStruct((nb, t, H * DK), out_dtype),
                   jax.ShapeDtypeStruct((nb, H, DK, DK), F32)],
        scratch_shapes=[pltpu.VMEM((NS, C + SUBLANES, nqkv), F32),
                        pltpu.VMEM((NS, DK, H * DK), F32)],
        compiler_params=_cparams(("parallel", "arbitrary")),
        name="gdn_core_c%d" % C,
    )(*args)
    return y.reshape(nb * t, H * DK), sfin


def _fox_prep_kernel(q_ref, k_ref, v_ref, f_ref, qn_ref, kn_ref, fb_ref,
                     qo_ref, ko_ref, kb_ref, vo_ref, vb_ref, lf_ref, *, H, HD):
    q_scale = (HD ** -0.5) * LOG2E
    for h in range(H):
        sl = slice(h * HD, (h + 1) * HD)
        q = q_ref[:, sl]
        q = q * lax.rsqrt(jnp.mean(q * q, axis=-1, keepdims=True) + EPS) * qn_ref[...]
        qo_ref[:, sl] = (q * q_scale).astype(qo_ref.dtype)
        k = k_ref[:, sl]
        k = k * lax.rsqrt(jnp.mean(k * k, axis=-1, keepdims=True) + EPS) * kn_ref[...]
        ko_ref[:, sl] = k
        kb_ref[:, sl] = k.astype(kb_ref.dtype)
    v = v_ref[...]
    vo_ref[...] = v
    vb_ref[...] = v.astype(vb_ref.dtype)
    x = f_ref[...] + fb_ref[...]
    lf_ref[...] = jnp.minimum(x, 0.0) - jnp.log1p(jnp.exp(-jnp.abs(x)))


def fox_prep(proj, fproj, q_norm, k_norm, fb_pad, *, row0, rows, H, HD):
    d = H * HD
    tm = _pick(rows, (256, 128, 64, 32, 16, 8))
    assert row0 % tm == 0
    r0 = row0 // tm
    col = lambda j: (lambda i: (r0 + i, j))
    out = lambda i: (i, 0)
    return pl.pallas_call(
        functools.partial(_fox_prep_kernel, H=H, HD=HD),
        grid=(rows // tm,),
        in_specs=[pl.BlockSpec((tm, d), col(0)), pl.BlockSpec((tm, d), col(1)),
                  pl.BlockSpec((tm, d), col(2)), pl.BlockSpec((tm, LANES), col(0)),
                  pl.BlockSpec((1, HD), lambda i: (0, 0)), pl.BlockSpec((1, HD), lambda i: (0, 0)),
                  pl.BlockSpec((1, LANES), lambda i: (0, 0))],
        out_specs=[pl.BlockSpec((tm, d), out)] * 5 + [pl.BlockSpec((tm, LANES), out)],
        out_shape=[jax.ShapeDtypeStruct((rows, d), BF16), jax.ShapeDtypeStruct((rows, d), F32),
                   jax.ShapeDtypeStruct((rows, d), BF16), jax.ShapeDtypeStruct((rows, d), F32),
                   jax.ShapeDtypeStruct((rows, d), BF16), jax.ShapeDtypeStruct((rows, LANES), F32)],
        compiler_params=_cparams(("parallel",)),
        name="fox_prep",
    )(proj, proj, proj, fproj, q_norm.reshape(1, HD), k_norm.reshape(1, HD), fb_pad)


def _cumsum_block(x, carry):
    n = x.shape[0]
    ri = lax.broadcasted_iota(jnp.int32, (n, n), 0)
    ci = lax.broadcasted_iota(jnp.int32, (n, n), 1)
    tril = jnp.where(ci <= ri, 1.0, 0.0)
    return _dot_exact_lhs01(tril, x) + carry


def _cumsum_kernel(x_ref, o_ref, carry_ref):
    @pl.when(pl.program_id(1) == 0)
    def _():
        carry_ref[...] = jnp.zeros_like(carry_ref)

    cs = _cumsum_block(x_ref[0], carry_ref[...])
    o_ref[0] = cs
    carry_ref[...] = cs[cs.shape[0] - 1:, :]


def cumsum_rows(x):
    b, t, w = x.shape
    tb = _pick(t, (256, 128, 64, 32, 16, 8))
    return pl.pallas_call(
        _cumsum_kernel,
        grid=(b, t // tb),
        in_specs=[pl.BlockSpec((1, tb, w), lambda i, j: (i, j, 0))],
        out_specs=pl.BlockSpec((1, tb, w), lambda i, j: (i, j, 0)),
        out_shape=jax.ShapeDtypeStruct((b, t, w), F32),
        scratch_shapes=[pltpu.VMEM((1, w), F32)],
        compiler_params=_cparams(("parallel", "arbitrary")),
        name="cumsum_rows",
    )(x)


def _cumsum_lanes(x, carry):
    n = x.shape[1]
    ri = lax.broadcasted_iota(jnp.int32, (n, n), 0)
    ci = lax.broadcasted_iota(jnp.int32, (n, n), 1)
    triu = jnp.where(ri <= ci, 1.0, 0.0).astype(BF16)
    hi, mid, lo = _split3(x)
    d = functools.partial(jnp.dot, preferred_element_type=F32)
    return d(hi, triu) + (d(mid, triu) + d(lo, triu)) + carry


def _paged_cumsum_kernel(pt_ref, *refs, pps):
    lp_refs = refs[:pps]
    ln_ref, o_ref, on_ref, carry_ref = refs[pps:]
    p = pl.program_id(1)

    @pl.when(p == 0)
    def _():
        carry_ref[...] = jnp.zeros_like(carry_ref)

    h = lp_refs[0].shape[1]
    local = _cumsum_lanes(jnp.concatenate([r[0] for r in lp_refs], axis=0) if pps > 1 else lp_refs[0][0], 0.0)
    carry = carry_ref[...]
    for j in range(pps):
        lj = local[j * h:(j + 1) * h]
        o_ref[0, j] = lj + carry
        carry = carry + lj[:, lj.shape[1] - 1:]
    carry_ref[...] = carry

    @pl.when(p == pl.num_programs(1) - 1)
    def _():
        on_ref[0] = _cumsum_lanes(ln_ref[0], carry)


def paged_cumsum(page_table_flat, logf_hp, lf_new_hp, *, layer_off, nb, npages, page, H):
    pps = _pick(npages, CUMSUM_PAGES_PER_STEP)
    ts = lf_new_hp.shape[2]
    assert ts == page

    def pool_map(j):
        return lambda b, p, pt: (layer_off + pt[b * npages + p * pps + j], 0, 0)

    return pl.pallas_call(
        functools.partial(_paged_cumsum_kernel, pps=pps),
        grid_spec=pltpu.PrefetchScalarGridSpec(
            num_scalar_prefetch=1,
            grid=(nb, npages // pps),
            in_specs=[pl.BlockSpec((1, H, page), pool_map(j)) for j in range(pps)]
            + [pl.BlockSpec((1, H, ts), lambda b, p, pt: (b, 0, 0))],
            out_specs=[pl.BlockSpec((1, pps, H, page), lambda b, p, pt: (b, p, 0, 0)),
                       pl.BlockSpec((1, H, ts), lambda b, p, pt: (b, 0, 0))],
            scratch_shapes=[pltpu.VMEM((H, 1), F32)],
        ),
        out_shape=[jax.ShapeDtypeStruct((nb, npages, H, page), F32),
                   jax.ShapeDtypeStruct((nb, H, ts), F32)],
        compiler_params=_cparams(("parallel", "arbitrary")),
        name="paged_cumsum",
    )(page_table_flat, *([logf_hp] * pps), lf_new_hp)


def _fox_prompt_kernel(q_ref, k_ref, v_ref, c_ref, *refs, tq, nq, tiles_per_step):
    g_refs = refs[:tiles_per_step]
    o_ref, vx_ref = refs[tiles_per_step:]
    step = pl.program_id(2)
    steps = nq // tiles_per_step
    hd = k_ref.shape[1]

    @pl.when(step == 0)
    def _():
        vx_ref[:, :hd] = v_ref[...]
        vx_ref[:, hd:] = jnp.ones((vx_ref.shape[0], hd), vx_ref.dtype)

    def tile(i, slot):
        kext = (i + 1) * tq
        s = _dot_nt(q_ref[0, slot], k_ref[0:kext, :]) - c_ref[0, 0, :, 0:kext]
        rpos = lax.broadcasted_iota(jnp.int32, (tq, tq), 0)
        cpos = lax.broadcasted_iota(jnp.int32, (tq, tq), 1)
        diag = jnp.where(cpos <= rpos, s[:, kext - tq:], NEG)
        s = diag if i == 0 else jnp.concatenate([s[:, :kext - tq], diag], axis=1)
        m = jnp.max(s, axis=1, keepdims=True)
        pv = _dot(jnp.exp2(s - m), vx_ref[0:kext, :])
        o_ref[0, slot] = ((pv[:, :hd] / pv[:, hd:]) * _sigmoid(g_refs[slot][...])).astype(o_ref.dtype)

    def tiles_of_step(p):
        for slot in range(tiles_per_step):
            tile(slot * steps + p, slot)

    for p in range(steps):
        pl.when(step == p)(functools.partial(tiles_of_step, p))


def fox_prompt_attention(qb, kb, vb, c_row, proj, *, nb, t, H, HD):
    tq = _pick(t, ATTN_TQ)
    nq = t // tq
    tps = _pick(nq, ATTN_TILES_PER_STEP)
    steps = nq // tps
    d = H * HD
    gate_specs = [pl.BlockSpec((tq, HD), lambda b, h, p, slot=slot: (b * nq + slot * steps + p, 3 * H + h))
                  for slot in range(tps)]
    out = pl.pallas_call(
        functools.partial(_fox_prompt_kernel, tq=tq, nq=nq, tiles_per_step=tps),
        grid=(nb, H, steps),
        in_specs=[pl.BlockSpec((1, tps, tq, HD), lambda b, h, p: (b, 0, p, h)),
                  pl.BlockSpec((t, HD), lambda b, h, p: (b, h)),
                  pl.BlockSpec((t, HD), lambda b, h, p: (b, h)),
                  pl.BlockSpec((1, 1, 1, t), lambda b, h, p: (b, h, 0, 0))] + gate_specs,
        out_specs=pl.BlockSpec((1, tps, tq, HD), lambda b, h, p: (b, 0, p, h)),
        out_shape=jax.ShapeDtypeStruct((nb, tps, t // tps, d), BF16),
        scratch_shapes=[pltpu.VMEM((t, 2 * HD), BF16)],
        compiler_params=_cparams(("parallel", "parallel", "arbitrary")),
        name="fox_prompt_attn",
    )(qb.reshape(nb, tps, t // tps, d), kb, vb, c_row, *([proj] * tps))
    return out.reshape(nb * t, d)


def _fox_decode_kernel(pt_ref, *refs, pps, H, HD, NQ):
    q_ref = refs[0]
    kp_refs = refs[1:1 + pps]
    vp_refs = refs[1 + pps:1 + 2 * pps]
    (kn_ref, vn_ref, cum_ref, cn_ref, g_ref,
     o_ref, qcat_ref, mask_ref, m_ref, acc_ref, s_ref, vx_ref) = refs[1 + 2 * pps:]
    p = pl.program_id(1)
    rows = H * NQ
    kc = mask_ref.shape[1]
    width = kp_refs[0].shape[1]

    @pl.when(p == 0)
    def _():
        q = q_ref[0].astype(F32)
        for h in range(H):
            qcat_ref[h * NQ:(h + 1) * NQ, :] = q[:, h * HD:(h + 1) * HD]
        rh = lax.broadcasted_iota(jnp.int32, (rows, kc), 0) // NQ
        ch = lax.broadcasted_iota(jnp.int32, (rows, kc), 1) % H
        mask_ref[...] = jnp.where(rh == ch, 0.0, NEG)
        m_ref[...] = jnp.full_like(m_ref, NEG)
        acc_ref[...] = jnp.zeros_like(acc_ref)
        vx_ref[:, :, HD:] = jnp.ones((vx_ref.shape[0], width, HD), vx_ref.dtype)

    qc = qcat_ref[...].astype(BF16)

    chunks = range(0, width, kc)
    slots = vx_ref.shape[0]

    def scores(j, c0, mx):
        s = _dot_nt(qc, kp_refs[j][0, c0:c0 + kc, :]) + (mask_ref[...] - cum_ref[0, j, :, c0:c0 + kc])
        s_ref[:, (j % slots) * width + c0:(j % slots) * width + c0 + kc] = s
        return jnp.maximum(mx, s)

    def weighted(j, c0, m_new, pv):
        pr = jnp.exp2(s_ref[:, (j % slots) * width + c0:(j % slots) * width + c0 + kc] - m_new)
        return pv + _dot(pr, vx_ref[j % slots, c0:c0 + kc, :])

    def stage_values(j):
        vx_ref[j % slots, :, :HD] = vp_refs[j][0].astype(vx_ref.dtype)

    m, acc = m_ref[...], acc_ref[...]
    neg = jnp.full((rows, kc), NEG, F32)
    stage_values(0)
    mx = neg
    for c0 in chunks:
        mx = scores(0, c0, mx)
    for j in range(pps):
        m_new = jnp.maximum(m, jnp.max(mx, axis=1, keepdims=True))
        pv = jnp.zeros((rows, 2 * HD), F32)
        nxt = j + 1 < pps
        if nxt:
            stage_values(j + 1)
        mx = neg
        for c0 in chunks:
            pv = weighted(j, c0, m_new, pv)
            if nxt:
                mx = scores(j + 1, c0, mx)
        acc = jnp.exp2(m - m_new) * acc + pv
        m = m_new
    m_ref[...], acc_ref[...] = m, acc

    @pl.when(p == pl.num_programs(1) - 1)
    def _():
        wn = kn_ref.shape[1]
        r = lax.broadcasted_iota(jnp.int32, (rows, wn), 0)
        c = lax.broadcasted_iota(jnp.int32, (rows, wn), 1)
        ok = (r // NQ == c % H) & (c // H <= r % NQ)
        s = jnp.where(ok, _dot_nt(qc, kn_ref[0]) - cn_ref[0], NEG)
        m_fin = jnp.maximum(m, jnp.max(s, axis=1, keepdims=True))
        vn = vn_ref[0]
        fin = (jnp.exp2(m - m_fin) * acc
               + _dot(jnp.exp2(s - m_fin), jnp.concatenate([vn, jnp.ones_like(vn)], axis=1)))
        out = fin[:, :HD] / fin[:, HD:]
        for h in range(H):
            cs = slice(h * HD, (h + 1) * HD)
            o_ref[:, cs] = (out[h * NQ:(h + 1) * NQ, :] * _sigmoid(g_ref[:, cs])).astype(o_ref.dtype)


def fox_decode_attention(page_table_flat, q_s, cache_k, cache_v, kn, vn, cum, cnew, gate_src, *,
                         layer_off, nb, npages, page, H, HD, NQ, gate_rb0):
    d = H * HD
    pps = _pick(npages, DECODE_PAGES_PER_STEP)
    rows = H * NQ
    kc = _pick(page * H, DECODE_KEY_CHUNK)
    slots = min(2, pps)
    assert kc % H == 0 and (page * H) % kc == 0

    def pool_map(j):
        return lambda b, p, pt: (layer_off + pt[b * npages + p * pps + j], 0, 0)

    per_b = lambda b, p, pt: (b, 0, 0)
    pool_specs = [pl.BlockSpec((1, page * H, HD), pool_map(j)) for j in range(pps)]
    return pl.pallas_call(
        functools.partial(_fox_decode_kernel, pps=pps, H=H, HD=HD, NQ=NQ),
        grid_spec=pltpu.PrefetchScalarGridSpec(
            num_scalar_prefetch=1,
            grid=(nb, npages // pps),
            in_specs=[pl.BlockSpec((1, NQ, d), per_b)] + pool_specs + pool_specs
            + [pl.BlockSpec((1, NQ * H, HD), per_b),
               pl.BlockSpec((1, NQ * H, HD), per_b),
               pl.BlockSpec((1, pps, 1, page * H), lambda b, p, pt: (b, p, 0, 0)),
               pl.BlockSpec((1, 1, NQ * H), per_b),
               pl.BlockSpec((NQ, d), lambda b, p, pt: (gate_rb0 + b, 3))],
            out_specs=pl.BlockSpec((NQ, d), lambda b, p, pt: (b, 0)),
            scratch_shapes=[pltpu.VMEM((rows, HD), F32),
                            pltpu.VMEM((rows, kc), F32),
                            pltpu.VMEM((rows, 1), F32),
                            pltpu.VMEM((rows, 2 * HD), F32),
                            pltpu.VMEM((rows, slots * page * H), F32),
                            pltpu.VMEM((slots, page * H, 2 * HD), BF16)],
        ),
        out_shape=jax.ShapeDtypeStruct((nb * NQ, d), F32),
        compiler_params=_cparams(("parallel", "arbitrary")),
        name="fox_decode_attn",
    )(page_table_flat, q_s, *([cache_k] * pps), *([cache_v] * pps), kn, vn, cum, cnew, gate_src)


def _pad_lanes(v, start=0):
    out = jnp.zeros((1, LANES), F32)
    return lax.dynamic_update_slice(out, v.reshape(1, -1).astype(F32), (0, start))


def kernel(x_prompt, x_sample, state_delta, state_conv, cache_k, cache_v, cache_logf, page_table,
           norm_mix, norm_mlp, a_w_in, a_conv, a_A_log, a_dt_bias, a_out_norm, a_w_out,
           b_w_in, b_f_bias, b_q_norm, b_k_norm, b_w_out, w_up, w_down):
    bp, t, d = x_prompt.shape
    bs, ts, _ = x_sample.shape
    depth = norm_mix.shape[0]
    n_mixers = 2
    H = state_delta.shape[2]
    DK = state_delta.shape[3]
    nqkv = state_conv.shape[-1]
    HB = cache_k.shape[3]
    HD = cache_k.shape[4]
    n_pool, page = cache_k.shape[1], cache_k.shape[2]
    npages = page_table.shape[1]
    d_ff = w_up.shape[2]
    mp, ms = bp * t, bs * ts
    assert H * DK == d and HB * HD == d and nqkv == 3 * d and DK == LANES and HD == LANES
    assert 2 * H <= LANES and HB <= LANES and 3 <= ts <= page and t >= 3

    x = jnp.concatenate([x_prompt.reshape(mp, d), x_sample.reshape(ms, d)], axis=0)
    pt_flat = page_table.reshape(-1).astype(jnp.int32)
    a_w_in_t = jnp.swapaxes(a_w_in, 1, 2)
    b_w_in_t = jnp.swapaxes(b_w_in, 1, 2)
    pad_rows = lambda w: jnp.pad(w, ((0, 0), (0, LANES - w.shape[1]), (0, 0)))
    a_w_gate_t = pad_rows(a_w_in_t[:, nqkv + d:, :])
    b_w_gate_t = pad_rows(b_w_in_t[:, 4 * d:, :])
    w_down_bf = w_down.astype(BF16)
    logf_hp = jnp.swapaxes(cache_logf, 2, 3).reshape(-1, HB, page)

    p_delta, p_conv, s_delta, s_conv = [], [], [], []
    p_k, p_v, p_lf, s_k, s_v, s_lf = [], [], [], [], [], []
    for i in range(depth):
        j = i // n_mixers
        hn = rmsnorm_bf16(x, norm_mix[i])
        if i % n_mixers == 0:
            proj = matmul_wcast(hn, a_w_in_t, j, n_out=nqkv + d, w_transposed=True)
            gates = matmul_wcast(hn, a_w_gate_t, j, n_out=LANES, w_transposed=True)
            alog_pad = _pad_lanes(a_A_log[j], H)
            dtb_pad = _pad_lanes(a_dt_bias[j], H)
            common = dict(H=H, DK=DK)
            cp = 64 if t % 64 == 0 else t
            yp, stp = gdn_core(proj, gates, a_conv[j], alog_pad, dtb_pad, a_out_norm[j], None, None,
                               row0=0, nb=bp, t=t, C=cp, G=max(1, min(H, GDN_STACK_ROWS // cp)),
                               out_dtype=BF16, **common)
            ys, sts = gdn_core(proj, gates, a_conv[j], alog_pad, dtb_pad, a_out_norm[j],
                               state_conv[j], state_delta[j],
                               row0=mp, nb=bs, t=ts, C=ts, G=H, out_dtype=F32, **common)
            x = matmul_wcast(yp, a_w_out, j, n_out=d, epilogue="residual", residual=x, row0=0)
            x = matmul_wcast(ys.astype(BF16), a_w_out, j, n_out=d, epilogue="residual", residual=x, row0=mp)
            p_conv.append(jnp.stack([proj[b * t + t - 3:(b + 1) * t, :nqkv] for b in range(bp)]))
            s_conv.append(proj[mp:].reshape(bs, ts, -1)[:, ts - 3:, :nqkv])
            p_delta.append(stp)
            s_delta.append(sts)
        else:
            proj = matmul_wcast(hn, b_w_in_t, j, n_out=4 * d, w_transposed=True)
            fproj = matmul_wcast(hn, b_w_gate_t, j, n_out=LANES, w_transposed=True)
            prep = functools.partial(fox_prep, proj, fproj, b_q_norm[j], b_k_norm[j],
                                     _pad_lanes(b_f_bias[j]), H=HB, HD=HD)
            qb_p, kf_p, kb_p, vf_p, vb_p, lf_p = prep(row0=0, rows=mp)
            qb_s, kf_s, kb_s, vf_s, vb_s, lf_s = prep(row0=mp, rows=ms)
            c_p = cumsum_rows(lf_p.reshape(bp, t, LANES))
            c_row = jnp.transpose(c_p[:, :, :HB], (0, 2, 1)).reshape(bp, HB, 1, t) * LOG2E
            op = fox_prompt_attention(qb_p, kb_p, vb_p, c_row, proj, nb=bp, t=t, H=HB, HD=HD)
            lf_new_hp = jnp.swapaxes(lf_s[:, :HB].reshape(bs, ts, HB), 1, 2)
            lf_new_hp = jnp.pad(lf_new_hp, ((0, 0), (0, 0), (0, page - ts)))
            cum_hp, cnew_hp = paged_cumsum(pt_flat, logf_hp, lf_new_hp, layer_off=j * n_pool,
                                           nb=bs, npages=npages, page=page, H=HB)
            cum = jnp.swapaxes(cum_hp, 2, 3).reshape(bs, npages, 1, page * HB) * LOG2E
            cnew = jnp.swapaxes(cnew_hp[:, :, :ts], 1, 2).reshape(bs, 1, ts * HB) * LOG2E
            os_ = fox_decode_attention(
                pt_flat, qb_s.reshape(bs, ts, d),
                cache_k.reshape(-1, page * HB, HD), cache_v.reshape(-1, page * HB, HD),
                kb_s.reshape(bs, ts * HB, HD), vb_s.reshape(bs, ts * HB, HD),
                cum, cnew, proj,
                layer_off=j * n_pool, nb=bs, npages=npages, page=page, H=HB, HD=HD, NQ=ts,
                gate_rb0=mp // ts)
            x = matmul_wcast(op, b_w_out, j, n_out=d, epilogue="residual", residual=x, row0=0)
            x = matmul_wcast(os_.astype(BF16), b_w_out, j, n_out=d, epilogue="residual", residual=x, row0=mp)
            p_k.append(kf_p.reshape(bp, t, HB, HD))
            p_v.append(vf_p.reshape(bp, t, HB, HD))
            p_lf.append(lf_p[:, :HB].reshape(bp, t, HB))
            s_k.append(kf_s.reshape(bs, ts, HB, HD))
            s_v.append(vf_s.reshape(bs, ts, HB, HD))
            s_lf.append(lf_s[:, :HB].reshape(bs, ts, HB))
        hm = rmsnorm_bf16(x, norm_mlp[i])
        a = matmul_wcast(hm, w_up, i, n_out=d_ff, epilogue="relu2", out_dtype=BF16)
        down = functools.partial(matmul_fullk_residual, a, w_down_bf, i, x)
        if i + 1 < depth:
            x = down(row0=0, rows=mp + ms)
    y_p, y_s = down(row0=0, rows=mp), down(row0=mp, rows=ms)
    return (y_p.reshape(bp, t, d), y_s.reshape(bs, ts, d),
            jnp.stack(p_delta), jnp.stack(p_conv), jnp.stack(p_k), jnp.stack(p_v), jnp.stack(p_lf),
            jnp.stack(s_delta), jnp.stack(s_conv), jnp.stack(s_k), jnp.stack(s_v), jnp.stack(s_lf))
```

```python
import functools

import jax
import jax.numpy as jnp
from jax import lax
from jax.experimental import pallas as pl
from jax.experimental.pallas import tpu as pltpu

F32 = jnp.float32
BF16 = jnp.bfloat16
EPS = 1e-6
NEG = -1e30
LOG2E = 1.4426950408889634
LANES = 128
SUBLANES = 8
MXU_DIM = 256
VMEM_LIMIT = 56 * 1024 * 1024

ROW_TILES = (768, 512, 528, 384, 256, 128, 64, 32, 16, 8)
WCAST_ROW_TILES = (1056,) + ROW_TILES
COL_TILE = 1024
FULLK_COL_TILE = 512
ATTN_TQ = (512, 256, 128)
ATTN_TILES_PER_STEP = (4, 2, 1)
DECODE_PAGES_PER_STEP = (8, 4, 2, 1)
DECODE_KEY_CHUNK = (256, 128)
CUMSUM_PAGES_PER_STEP = (32, 16, 8, 4, 2, 1)
GDN_TOKENS_PER_STEP = 128
GDN_STACK_ROWS = 128

def _pick(n, cands):
    for c in cands:
        if n % c == 0:
            return c
    return n


def _cparams(sem):
    return pltpu.CompilerParams(dimension_semantics=sem, vmem_limit_bytes=VMEM_LIMIT)


def _dot(a, b):
    return jnp.dot(a.astype(BF16), b.astype(BF16), preferred_element_type=F32)


def _dot_nt(a, b):
    return lax.dot_general(a.astype(BF16), b.astype(BF16), (((1,), (1,)), ((), ())),
                           preferred_element_type=F32)


def _split3(x):
    hi = x.astype(BF16)
    r = x - hi.astype(F32)
    mid = r.astype(BF16)
    lo = (r - mid.astype(F32)).astype(BF16)
    return hi, mid, lo


def _dot_exact_lhs01(a01, b):
    hi, mid, lo = _split3(b)
    a = a01.astype(BF16)
    d = functools.partial(jnp.dot, preferred_element_type=F32)
    return d(a, hi) + (d(a, mid) + d(a, lo))


def _sigmoid(x):
    return 1.0 / (1.0 + jnp.exp(-x))


def _softplus(x):
    return jnp.maximum(x, 0.0) + jnp.log(1.0 + jnp.exp(-jnp.abs(x)))


def _rmsnorm_kernel(x_ref, w_ref, o_ref):
    x = x_ref[...]
    ms = jnp.mean(x * x, axis=-1, keepdims=True)
    o_ref[...] = (x * lax.rsqrt(ms + EPS) * w_ref[...]).astype(o_ref.dtype)


def rmsnorm_bf16(x, w):
    m, d = x.shape
    tm = _pick(m, ROW_TILES)
    return pl.pallas_call(
        _rmsnorm_kernel,
        grid=(m // tm,),
        in_specs=[pl.BlockSpec((tm, d), lambda i: (i, 0)),
                  pl.BlockSpec((1, d), lambda i: (0, 0))],
        out_specs=pl.BlockSpec((tm, d), lambda i: (i, 0)),
        out_shape=jax.ShapeDtypeStruct((m, d), BF16),
        compiler_params=_cparams(("parallel",)),
        name="rmsnorm",
    )(x, w.reshape(1, d))


def _epilogue(acc, kind, r_ref):
    if kind == "relu2":
        acc = jnp.maximum(acc, 0.0)
        acc = acc * acc
    elif kind == "residual":
        acc = acc + r_ref[...]
    return acc


def _mm_wcast_kernel(*refs, epilogue, w_transposed):
    if epilogue == "residual":
        a_ref, w_ref, r_ref, o_ref, wb_ref = refs
    else:
        a_ref, w_ref, o_ref, wb_ref = refs
        r_ref = None

    @pl.when(pl.program_id(1) == 0)
    def _():
        wb_ref[...] = w_ref[...].astype(wb_ref.dtype)

    if w_transposed:
        acc = _dot_nt(a_ref[...], wb_ref[...])
    else:
        acc = jnp.dot(a_ref[...], wb_ref[...], preferred_element_type=F32)
    o_ref[...] = _epilogue(acc, epilogue, r_ref).astype(o_ref.dtype)


def matmul_wcast(a, w, layer, *, n_out, w_transposed=False, epilogue="none", residual=None,
                 out_dtype=F32, row0=0):
    ma, kdim = a.shape
    tm = _pick(ma, WCAST_ROW_TILES if epilogue != "residual" else (1024, 512, 256, 128, 64, 32, 16, 8))
    tn = min(COL_TILE, n_out)
    assert n_out % tn == 0 and row0 % tm == 0 and w.shape[2 if w_transposed else 1] == kdim
    r0 = row0 // tm
    if w_transposed:
        w_spec = pl.BlockSpec((None, tn, kdim), lambda j, i: (layer, j, 0))
        wb_shape = (tn, kdim)
    else:
        w_spec = pl.BlockSpec((None, kdim, tn), lambda j, i: (layer, 0, j))
        wb_shape = (kdim, tn)
    in_specs = [pl.BlockSpec((tm, kdim), lambda j, i: (i, 0)), w_spec]
    args = [a, w]
    aliases = {}
    if epilogue == "residual":
        in_specs.append(pl.BlockSpec((tm, tn), lambda j, i: (r0 + i, j)))
        args.append(residual)
        aliases = {2: 0}
        out_rows = residual.shape[0]
    else:
        out_rows = ma
    return pl.pallas_call(
        functools.partial(_mm_wcast_kernel, epilogue=epilogue, w_transposed=w_transposed),
        grid=(n_out // tn, ma // tm),
        in_specs=in_specs,
        out_specs=pl.BlockSpec((tm, tn), lambda j, i: (r0 + i, j)),
        out_shape=jax.ShapeDtypeStruct((out_rows, n_out), out_dtype),
        scratch_shapes=[pltpu.VMEM(wb_shape, BF16)],
        input_output_aliases=aliases,
        compiler_params=_cparams(("arbitrary", "arbitrary")),
        name="matmul_wcast_" + epilogue,
    )(*args)


def _mm_fullk_kernel(a_ref, b_ref, r_ref, o_ref):
    acc = jnp.dot(a_ref[...], b_ref[...], preferred_element_type=F32)
    o_ref[...] = (acc + r_ref[...]).astype(o_ref.dtype)


def matmul_fullk_residual(a, b, layer, residual, *, row0, rows):
    _, kdim = a.shape
    n = b.shape[2]
    tm = _pick(rows, ROW_TILES)
    tn = min(FULLK_COL_TILE, n)
    assert n % tn == 0 and row0 % tm == 0
    r0 = row0 // tm
    return pl.pallas_call(
        _mm_fullk_kernel,
        grid=(rows // tm, n // tn),
        in_specs=[pl.BlockSpec((tm, kdim), lambda i, j: (r0 + i, 0)),
                  pl.BlockSpec((None, kdim, tn), lambda i, j: (layer, 0, j)),
                  pl.BlockSpec((tm, tn), lambda i, j: (r0 + i, j))],
        out_specs=pl.BlockSpec((tm, tn), lambda i, j: (i, j)),
        out_shape=jax.ShapeDtypeStruct((rows, n), F32),
        compiler_params=_cparams(("parallel", "arbitrary")),
        name="matmul_fullk_residual",
    )(a, b, residual)


def _gdn_kernel(*refs, C, G, H, DK, NS, has_cbuf, has_s0):
    it = iter(refs)
    seq_refs = [(next(it), next(it), next(it)) for _ in range(NS)]
    cw_ref = next(it)
    alog_ref = next(it)
    dtb_ref = next(it)
    onw_ref = next(it)
    cbuf_refs = [next(it) for _ in range(NS)] if has_cbuf else None
    s0_refs = [next(it) for _ in range(NS)] if has_s0 else None
    y_ref = next(it)
    sfin_ref = next(it)
    ext_ref = next(it)
    scat_ref = next(it)

    c_idx = pl.program_id(1)
    nc = pl.num_programs(1)
    R = G * C
    NG = H // G
    NQK = H * DK
    pad = SUBLANES

    @pl.when(c_idx == 0)
    def _init():
        for s in range(NS):
            ext_ref[s, 0:pad, :] = jnp.zeros((pad, ext_ref.shape[2]), F32)
            if has_cbuf:
                ext_ref[s, pad - 3:pad, :] = cbuf_refs[s][0]
            for h in range(H):
                if has_s0:
                    scat_ref[s, :, h * DK:(h + 1) * DK] = s0_refs[s][0, h]
                else:
                    scat_ref[s, :, h * DK:(h + 1) * DK] = jnp.zeros((DK, DK), F32)

    for s in range(NS):
        ext_ref[s, pad:pad + C, :] = seq_refs[s][0][...]

    def conv_act(s, col):
        sl = slice(col, col + DK)
        acc = ext_ref[s, pad:pad + C, sl] * cw_ref[3:4, sl]
        for i in range(3):
            acc = acc + ext_ref[s, pad - 3 + i:pad - 3 + i + C, sl] * cw_ref[i:i + 1, sl]
        return acc * _sigmoid(acc)

    beta_full = [_sigmoid(seq_refs[s][2][...]) for s in range(NS)]
    g_full = [-jnp.exp(alog_ref[...]) * _softplus(seq_refs[s][2][...] + dtb_ref[...]) for s in range(NS)]

    iotas = (lax.broadcasted_iota(jnp.int32, (R, R), 0), lax.broadcasted_iota(jnp.int32, (R, R), 1))

    def blk(axis, m):
        return lax.shift_right_logical(iotas[axis], m.bit_length() - 1)

    same = blk(0, C) == blk(1, C)
    eye = iotas[0] == iotas[1]
    incl = same & (iotas[1] <= iotas[0])
    strict = same & (iotas[1] < iotas[0])
    incl_t = same & (iotas[0] <= iotas[1])
    rowblk = lax.shift_right_logical(lax.broadcasted_iota(jnp.int32, (R, DK), 0), C.bit_length() - 1)
    cat = (lambda xs: xs[0]) if G == 1 else (lambda xs: jnp.concatenate(xs, axis=0))
    groups = [list(range(gi * G, (gi + 1) * G)) for gi in range(NG)]

    def group_program(s, heads):
        z_ref = seq_refs[s][1]
        qs, ks, vs = [], [], []
        for h in heads:
            q = conv_act(s, h * DK)
            k = conv_act(s, NQK + h * DK)
            qs.append(q * lax.rsqrt(jnp.sum(q * q, axis=-1, keepdims=True) + EPS) * (DK ** -0.5))
            ks.append(k * lax.rsqrt(jnp.sum(k * k, axis=-1, keepdims=True) + EPS))
            vs.append(conv_act(s, 2 * NQK + h * DK))
        Q, K, V = cat(qs), cat(ks), cat(vs)
        beta = cat([beta_full[s][:, h:h + 1] for h in heads])
        g_c = cat([g_full[s][:, H + h:H + h + 1] for h in heads])
        yield
        g_r = jnp.sum(jnp.where(eye, g_c, 0.0), axis=0, keepdims=True)
        gc = jnp.sum(jnp.where(incl, g_r, 0.0), axis=1, keepdims=True)
        gc_r = jnp.sum(jnp.where(incl_t, g_c, 0.0), axis=0, keepdims=True)
        gl = jnp.sum(jnp.where(same, g_r, 0.0), axis=1, keepdims=True)
        decay = jnp.exp(jnp.where(incl, gc - gc_r, NEG))
        KB = K * beta
        lmat = jnp.where(strict, _dot_nt(KB, K) * decay, 0.0)
        amat = jnp.where(incl, _dot_nt(Q, K) * decay, 0.0)
        yield
        ld = jnp.where(blk(0, SUBLANES) == blk(1, SUBLANES), lmat, 0.0)
        ld2 = _dot(ld, ld)
        yield
        ld4 = _dot(ld2, ld2)
        x = jnp.where(eye, 1.0, 0.0) - ld
        x = x + _dot(x, ld2)
        yield
        x = x + _dot(x, ld4)
        yield
        m = SUBLANES
        while m < C:
            sel = (blk(0, 2 * m) == blk(1, 2 * m)) & (blk(0, m) != blk(1, m))
            ox = _dot(jnp.where(sel, lmat, 0.0), x)
            yield
            x = x - _dot(x, ox)
            yield
            m *= 2
        eg = jnp.exp(gc)
        UW = _dot(x, jnp.concatenate([V * beta, KB * eg], axis=1))
        yield
        sc = scat_ref[s, :, heads[0] * DK:(heads[-1] + 1) * DK]
        WQS = _dot(jnp.concatenate([UW[:, DK:], Q * eg], axis=0), sc)
        yield
        vn, oi = [], []
        for j in range(G):
            rs = slice(j * C, (j + 1) * C)
            cs = slice(j * DK, (j + 1) * DK)
            vn.append(UW[rs, :DK] - WQS[rs, cs])
            oi.append(WQS[R + j * C:R + (j + 1) * C, cs])
        Vn = cat(vn)
        O = cat(oi) + _dot(amat, Vn)
        ke = K * jnp.exp(gl - gc)
        vwide = jnp.concatenate([jnp.where(rowblk == j, Vn, 0.0) for j in range(G)], axis=1)
        dS = _dot(ke.T, vwide)
        yield
        for j, h in enumerate(heads):
            cs = slice(j * DK, (j + 1) * DK)
            e_last = jnp.exp(gl[j * C:j * C + 1, :])
            scat_ref[s, :, h * DK:(h + 1) * DK] = sc[:, cs] * e_last + dS[:, cs]
            o = O[j * C:(j + 1) * C]
            o = o * lax.rsqrt(jnp.mean(o * o, axis=-1, keepdims=True) + EPS) * onw_ref[...]
            zg = z_ref[:, h * DK:(h + 1) * DK]
            y_ref[s, :, h * DK:(h + 1) * DK] = (o * (zg * _sigmoid(zg))).astype(y_ref.dtype)

    programs = [group_program(s, heads) for s in range(NS) for heads in groups]
    while programs:
        running = []
        for prog in programs:
            try:
                next(prog)
                running.append(prog)
            except StopIteration:
                pass
        programs = running

    for s in range(NS):
        carry = ext_ref[s, C:C + pad, :]
        ext_ref[s, 0:pad, :] = carry

    @pl.when(c_idx == nc - 1)
    def _fin():
        for s in range(NS):
            for h in range(H):
                sfin_ref[s, h] = scat_ref[s, :, h * DK:(h + 1) * DK]


def gdn_core(proj, gates, conv_w, alog_pad, dtb_pad, out_norm, conv_buf, s0, *,
             row0, nb, t, C, G, H, DK, out_dtype):
    nqkv = 3 * H * DK
    nc = t // C
    assert t % C == 0 and row0 % C == 0 and H % G == 0 and C & (C - 1) == 0
    NS = _pick(nb, tuple(n for n in (4, 2, 1) if n * C <= GDN_TOKENS_PER_STEP))
    rb0 = row0 // C
    has_cbuf = conv_buf is not None
    has_s0 = s0 is not None

    def rows(s, col):
        return lambda bb, c: (rb0 + (bb * NS + s) * nc + c, col)

    const = lambda bb, c: (0, 0)
    in_specs, args = [], []
    for s in range(NS):
        in_specs += [pl.BlockSpec((C, nqkv), rows(s, 0)),
                     pl.BlockSpec((C, H * DK), rows(s, 3)),
                     pl.BlockSpec((C, LANES), rows(s, 0))]
        args += [proj, proj, gates]
    in_specs += [pl.BlockSpec((4, nqkv), const), pl.BlockSpec((1, LANES), const),
                 pl.BlockSpec((1, LANES), const), pl.BlockSpec((1, DK), const)]
    args += [conv_w, alog_pad, dtb_pad, out_norm.reshape(1, DK)]
    if has_cbuf:
        for s in range(NS):
            in_specs.append(pl.BlockSpec((1, 3, nqkv), lambda bb, c, s=s: (bb * NS + s, 0, 0)))
            args.append(conv_buf)
    if has_s0:
        for s in range(NS):
            in_specs.append(pl.BlockSpec((1, H, DK, DK), lambda bb, c, s=s: (bb * NS + s, 0, 0, 0)))
            args.append(s0)
    y, sfin = pl.pallas_call(
        functools.partial(_gdn_kernel, C=C, G=G, H=H, DK=DK, NS=NS, has_cbuf=has_cbuf, has_s0=has_s0),
        grid=(nb // NS, nc),
        in_specs=in_specs,
        out_specs=[pl.BlockSpec((NS, C, H * DK), lambda bb, c: (bb, c, 0)),
                   pl.BlockSpec((NS, H, DK, DK), lambda bb, c: (bb, 0, 0, 0))],
        out_shape=[jax.ShapeDtypeStruct((nb, t, H * DK), out_dtype),
                   jax.ShapeDtypeStruct((nb, H, DK, DK), F32)],
        scratch_shapes=[pltpu.VMEM((NS, C + SUBLANES, nqkv), F32),
                        pltpu.VMEM((NS, DK, H * DK), F32)],
        compiler_params=_cparams(("parallel", "arbitrary")),
        name="gdn_core_c%d" % C,
    )(*args)
    return y.reshape(nb * t, H * DK), sfin


def _fox_prep_kernel(q_ref, k_ref, v_ref, f_ref, qn_ref, kn_ref, fb_ref,
                     qo_ref, ko_ref, kb_ref, vo_ref, vb_ref, lf_ref, *, H, HD):
    q_scale = (HD ** -0.5) * LOG2E
    for h in range(H):
        sl = slice(h * HD, (h + 1) * HD)
        q = q_ref[:, sl]
        q = q * lax.rsqrt(jnp.mean(q * q, axis=-1, keepdims=True) + EPS) * qn_ref[...]
        qo_ref[:, sl] = (q * q_scale).astype(qo_ref.dtype)
        k = k_ref[:, sl]
        k = k * lax.rsqrt(jnp.mean(k * k, axis=-1, keepdims=True) + EPS) * kn_ref[...]
        ko_ref[:, sl] = k
        kb_ref[:, sl] = k.astype(kb_ref.dtype)
    v = v_ref[...]
    vo_ref[...] = v
    vb_ref[...] = v.astype(vb_ref.dtype)
    x = f_ref[...] + fb_ref[...]
    lf_ref[...] = jnp.minimum(x, 0.0) - jnp.log1p(jnp.exp(-jnp.abs(x)))


def fox_prep(proj, fproj, q_norm, k_norm, fb_pad, *, row0, rows, H, HD):
    d = H * HD
    tm = _pick(rows, (256, 128, 64, 32, 16, 8))
    assert row0 % tm == 0
    r0 = row0 // tm
    col = lambda j: (lambda i: (r0 + i, j))
    out = lambda i: (i, 0)
    return pl.pallas_call(
        functools.partial(_fox_prep_kernel, H=H, HD=HD),
        grid=(rows // tm,),
        in_specs=[pl.BlockSpec((tm, d), col(0)), pl.BlockSpec((tm, d), col(1)),
                  pl.BlockSpec((tm, d), col(2)), pl.BlockSpec((tm, LANES), col(0)),
                  pl.BlockSpec((1, HD), lambda i: (0, 0)), pl.BlockSpec((1, HD), lambda i: (0, 0)),
                  pl.BlockSpec((1, LANES), lambda i: (0, 0))],
        out_specs=[pl.BlockSpec((tm, d), out)] * 5 + [pl.BlockSpec((tm, LANES), out)],
        out_shape=[jax.ShapeDtypeStruct((rows, d), BF16), jax.ShapeDtypeStruct((rows, d), F32),
                   jax.ShapeDtypeStruct((rows, d), BF16), jax.ShapeDtypeStruct((rows, d), F32),
                   jax.ShapeDtypeStruct((rows, d), BF16), jax.ShapeDtypeStruct((rows, LANES), F32)],
        compiler_params=_cparams(("parallel",)),
        name="fox_prep",
    )(proj, proj, proj, fproj, q_norm.reshape(1, HD), k_norm.reshape(1, HD), fb_pad)


def _cumsum_block(x, carry):
    n = x.shape[0]
    ri = lax.broadcasted_iota(jnp.int32, (n, n), 0)
    ci = lax.broadcasted_iota(jnp.int32, (n, n), 1)
    tril = jnp.where(ci <= ri, 1.0, 0.0)
    return _dot_exact_lhs01(tril, x) + carry


def _cumsum_kernel(x_ref, o_ref, carry_ref):
    @pl.when(pl.program_id(1) == 0)
    def _():
        carry_ref[...] = jnp.zeros_like(carry_ref)

    cs = _cumsum_block(x_ref[0], carry_ref[...])
    o_ref[0] = cs
    carry_ref[...] = cs[cs.shape[0] - 1:, :]


def cumsum_rows(x):
    b, t, w = x.shape
    tb = _pick(t, (256, 128, 64, 32, 16, 8))
    return pl.pallas_call(
        _cumsum_kernel,
        grid=(b, t // tb),
        in_specs=[pl.BlockSpec((1, tb, w), lambda i, j: (i, j, 0))],
        out_specs=pl.BlockSpec((1, tb, w), lambda i, j: (i, j, 0)),
        out_shape=jax.ShapeDtypeStruct((b, t, w), F32),
        scratch_shapes=[pltpu.VMEM((1, w), F32)],
        compiler_params=_cparams(("parallel", "arbitrary")),
        name="cumsum_rows",
    )(x)


def _cumsum_lanes(x, carry):
    n = x.shape[1]
    ri = lax.broadcasted_iota(jnp.int32, (n, n), 0)
    ci = lax.broadcasted_iota(jnp.int32, (n, n), 1)
    triu = jnp.where(ri <= ci, 1.0, 0.0).astype(BF16)
    hi, mid, lo = _split3(x)
    d = functools.partial(jnp.dot, preferred_element_type=F32)
    return d(hi, triu) + (d(mid, triu) + d(lo, triu)) + carry


def _paged_cumsum_kernel(pt_ref, *refs, pps):
    lp_refs = refs[:pps]
    ln_ref, o_ref, on_ref, carry_ref = refs[pps:]
    p = pl.program_id(1)

    @pl.when(p == 0)
    def _():
        carry_ref[...] = jnp.zeros_like(carry_ref)

    h = lp_refs[0].shape[1]
    local = _cumsum_lanes(jnp.concatenate([r[0] for r in lp_refs], axis=0) if pps > 1 else lp_refs[0][0], 0.0)
    carry = carry_ref[...]
    for j in range(pps):
        lj = local[j * h:(j + 1) * h]
        o_ref[0, j] = lj + carry
        carry = carry + lj[:, lj.shape[1] - 1:]
    carry_ref[...] = carry

    @pl.when(p == pl.num_programs(1) - 1)
    def _():
        on_ref[0] = _cumsum_lanes(ln_ref[0], carry)


def paged_cumsum(page_table_flat, logf_hp, lf_new_hp, *, layer_off, nb, npages, page, H):
    pps = _pick(npages, CUMSUM_PAGES_PER_STEP)
    ts = lf_new_hp.shape[2]
    assert ts == page

    def pool_map(j):
        return lambda b, p, pt: (layer_off + pt[b * npages + p * pps + j], 0, 0)

    return pl.pallas_call(
        functools.partial(_paged_cumsum_kernel, pps=pps),
        grid_spec=pltpu.PrefetchScalarGridSpec(
            num_scalar_prefetch=1,
            grid=(nb, npages // pps),
            in_specs=[pl.BlockSpec((1, H, page), pool_map(j)) for j in range(pps)]
            + [pl.BlockSpec((1, H, ts), lambda b, p, pt: (b, 0, 0))],
            out_specs=[pl.BlockSpec((1, pps, H, page), lambda b, p, pt: (b, p, 0, 0)),
                       pl.BlockSpec((1, H, ts), lambda b, p, pt: (b, 0, 0))],
            scratch_shapes=[pltpu.VMEM((H, 1), F32)],
        ),
        out_shape=[jax.ShapeDtypeStruct((nb, npages, H, page), F32),
                   jax.ShapeDtypeStruct((nb, H, ts), F32)],
        compiler_params=_cparams(("parallel", "arbitrary")),
        name="paged_cumsum",
    )(page_table_flat, *([logf_hp] * pps), lf_new_hp)


def _prompt_stage_values(v_ref, vx_ref):
    hd = v_ref.shape[1]
    vx_ref[:, :hd] = v_ref[...]
    vx_ref[:, hd:] = jnp.ones((vx_ref.shape[0], hd), vx_ref.dtype)


def _prompt_tile(q, k_ref, vx_ref, c_ref, gate, i):
    tq, hd = q.shape
    kext = (i + 1) * tq
    s = _dot_nt(q, k_ref[0:kext, :]) - c_ref[0, 0, :, 0:kext]
    rpos = lax.broadcasted_iota(jnp.int32, (tq, tq), 0)
    cpos = lax.broadcasted_iota(jnp.int32, (tq, tq), 1)
    diag = jnp.where(cpos <= rpos, s[:, kext - tq:], NEG)
    s = diag if i == 0 else jnp.concatenate([s[:, :kext - tq], diag], axis=1)
    m = jnp.max(s, axis=1, keepdims=True)
    pv = _dot(jnp.exp2(s - m), vx_ref[0:kext, :])
    return (pv[:, :hd] / pv[:, hd:]) * _sigmoid(gate)


def _fox_prompt_kernel(q_ref, k_ref, v_ref, c_ref, *refs, tq, nq, tiles_per_step):
    g_refs = refs[:tiles_per_step]
    o_ref, vx_ref = refs[tiles_per_step:]
    step = pl.program_id(2)
    steps = nq // tiles_per_step

    pl.when(step == 0)(functools.partial(_prompt_stage_values, v_ref, vx_ref))

    def tiles_of_step(p):
        for slot in range(tiles_per_step):
            out = _prompt_tile(q_ref[0, slot], k_ref, vx_ref, c_ref, g_refs[slot][...], slot * steps + p)
            o_ref[0, slot] = out.astype(o_ref.dtype)

    for p in range(steps):
        pl.when(step == p)(functools.partial(tiles_of_step, p))


def fox_prompt_attention(qb, kb, vb, c_row, proj, *, nb, t, H, HD):
    tq = _pick(t, ATTN_TQ)
    nq = t // tq
    tps = _pick(nq, ATTN_TILES_PER_STEP)
    steps = nq // tps
    d = H * HD
    gate_specs = [pl.BlockSpec((tq, HD), lambda b, h, p, slot=slot: (b * nq + slot * steps + p, 3 * H + h))
                  for slot in range(tps)]
    out = pl.pallas_call(
        functools.partial(_fox_prompt_kernel, tq=tq, nq=nq, tiles_per_step=tps),
        grid=(nb, H, steps),
        in_specs=[pl.BlockSpec((1, tps, tq, HD), lambda b, h, p: (b, 0, p, h)),
                  pl.BlockSpec((t, HD), lambda b, h, p: (b, h)),
                  pl.BlockSpec((t, HD), lambda b, h, p: (b, h)),
                  pl.BlockSpec((1, 1, 1, t), lambda b, h, p: (b, h, 0, 0))] + gate_specs,
        out_specs=pl.BlockSpec((1, tps, tq, HD), lambda b, h, p: (b, 0, p, h)),
        out_shape=jax.ShapeDtypeStruct((nb, tps, t // tps, d), BF16),
        scratch_shapes=[pltpu.VMEM((t, 2 * HD), BF16)],
        compiler_params=_cparams(("parallel", "parallel", "arbitrary")),
        name="fox_prompt_attn",
    )(qb.reshape(nb, tps, t // tps, d), kb, vb, c_row, *([proj] * tps))
    return out.reshape(nb * t, d)


def _fox_decode_kernel(pt_ref, *refs, pps, H, HD, NQ, prompt_tiles):
    q_ref = refs[0]
    kp_refs = refs[1:1 + pps]
    vp_refs = refs[1 + pps:1 + 2 * pps]
    rest = refs[1 + 2 * pps:]
    if prompt_tiles:
        (kn_ref, vn_ref, cum_ref, cn_ref, g_ref, pq_ref, pk_ref, pv_ref, pc_ref, pg_ref,
         o_ref, po_ref, qcat_ref, mask_ref, m_ref, acc_ref, s_ref, vx_ref, pvx_ref) = rest
    else:
        (kn_ref, vn_ref, cum_ref, cn_ref, g_ref,
         o_ref, qcat_ref, mask_ref, m_ref, acc_ref, s_ref, vx_ref) = rest
    p = pl.program_id(1)
    rows = H * NQ
    kc = mask_ref.shape[1]
    width = kp_refs[0].shape[1]

    @pl.when(p == 0)
    def _():
        q = q_ref[0].astype(F32)
        for h in range(H):
            qcat_ref[h * NQ:(h + 1) * NQ, :] = q[:, h * HD:(h + 1) * HD]
        rh = lax.broadcasted_iota(jnp.int32, (rows, kc), 0) // NQ
        ch = lax.broadcasted_iota(jnp.int32, (rows, kc), 1) % H
        mask_ref[...] = jnp.where(rh == ch, 0.0, NEG)
        m_ref[...] = jnp.full_like(m_ref, NEG)
        acc_ref[...] = jnp.zeros_like(acc_ref)
        vx_ref[:, :, HD:] = jnp.ones((vx_ref.shape[0], width, HD), vx_ref.dtype)

    qc = qcat_ref[...].astype(BF16)

    chunks = range(0, width, kc)
    slots = vx_ref.shape[0]

    def scores(j, c0, mx):
        s = _dot_nt(qc, kp_refs[j][0, c0:c0 + kc, :]) + (mask_ref[...] - cum_ref[0, j, :, c0:c0 + kc])
        s_ref[:, (j % slots) * width + c0:(j % slots) * width + c0 + kc] = s
        return jnp.maximum(mx, s)

    def weighted(j, c0, m_new, pv):
        pr = jnp.exp2(s_ref[:, (j % slots) * width + c0:(j % slots) * width + c0 + kc] - m_new)
        return pv + _dot(pr, vx_ref[j % slots, c0:c0 + kc, :])

    def stage_values(j):
        vx_ref[j % slots, :, :HD] = vp_refs[j][0].astype(vx_ref.dtype)

    m, acc = m_ref[...], acc_ref[...]
    neg = jnp.full((rows, kc), NEG, F32)
    stage_values(0)
    mx = neg
    for c0 in chunks:
        mx = scores(0, c0, mx)
    for j in range(pps):
        m_new = jnp.maximum(m, jnp.max(mx, axis=1, keepdims=True))
        pv = jnp.zeros((rows, 2 * HD), F32)
        nxt = j + 1 < pps
        if nxt:
            stage_values(j + 1)
        mx = neg
        for c0 in chunks:
            pv = weighted(j, c0, m_new, pv)
            if nxt:
                mx = scores(j + 1, c0, mx)
        acc = jnp.exp2(m - m_new) * acc + pv
        m = m_new
    m_ref[...], acc_ref[...] = m, acc

    @pl.when(p == pl.num_programs(1) - 1)
    def _():
        wn = kn_ref.shape[1]
        r = lax.broadcasted_iota(jnp.int32, (rows, wn), 0)
        c = lax.broadcasted_iota(jnp.int32, (rows, wn), 1)
        ok = (r // NQ == c % H) & (c // H <= r % NQ)
        s = jnp.where(ok, _dot_nt(qc, kn_ref[0]) - cn_ref[0], NEG)
        m_fin = jnp.maximum(m, jnp.max(s, axis=1, keepdims=True))
        vn = vn_ref[0]
        fin = (jnp.exp2(m - m_fin) * acc
               + _dot(jnp.exp2(s - m_fin), jnp.concatenate([vn, jnp.ones_like(vn)], axis=1)))
        out = fin[:, :HD] / fin[:, HD:]
        for h in range(H):
            cs = slice(h * HD, (h + 1) * HD)
            o_ref[:, cs] = (out[h * NQ:(h + 1) * NQ, :] * _sigmoid(g_ref[:, cs])).astype(o_ref.dtype)

    if prompt_tiles:
        tile_idx = p % prompt_tiles
        pl.when(tile_idx == 0)(functools.partial(_prompt_stage_values, pv_ref, pvx_ref))

        def prompt_tile(i):
            po_ref[...] = _prompt_tile(pq_ref[...], pk_ref, pvx_ref, pc_ref, pg_ref[...], i).astype(po_ref.dtype)

        for i in range(prompt_tiles):
            pl.when(tile_idx == i)(functools.partial(prompt_tile, i))


def fox_decode_attention(page_table_flat, q_s, cache_k, cache_v, kn, vn, cum, cnew, gate_src, *,
                         layer_off, nb, npages, page, H, HD, NQ, gate_rb0, prompt=None):
    d = H * HD
    pps = _pick(npages, DECODE_PAGES_PER_STEP)
    steps = npages // pps
    rows = H * NQ
    kc = _pick(page * H, DECODE_KEY_CHUNK)
    slots = min(2, pps)
    assert kc % H == 0 and (page * H) % kc == 0

    def pool_map(j):
        return lambda b, p, pt: (layer_off + pt[b * npages + p * pps + j], 0, 0)

    per_b = lambda b, p, pt: (b, 0, 0)
    pool_specs = [pl.BlockSpec((1, page * H, HD), pool_map(j)) for j in range(pps)]
    in_specs = ([pl.BlockSpec((1, NQ, d), per_b)] + pool_specs + pool_specs
                + [pl.BlockSpec((1, NQ * H, HD), per_b),
                   pl.BlockSpec((1, NQ * H, HD), per_b),
                   pl.BlockSpec((1, pps, 1, page * H), lambda b, p, pt: (b, p, 0, 0)),
                   pl.BlockSpec((1, 1, NQ * H), per_b),
                   pl.BlockSpec((NQ, d), lambda b, p, pt: (gate_rb0 + b, 3))])
    args = [page_table_flat, q_s, *([cache_k] * pps), *([cache_v] * pps), kn, vn, cum, cnew, gate_src]
    out_specs = [pl.BlockSpec((NQ, d), lambda b, p, pt: (b, 0))]
    out_shape = [jax.ShapeDtypeStruct((nb * NQ, d), F32)]
    scratch = [pltpu.VMEM((rows, HD), F32),
               pltpu.VMEM((rows, kc), F32),
               pltpu.VMEM((rows, 1), F32),
               pltpu.VMEM((rows, 2 * HD), F32),
               pltpu.VMEM((rows, slots * page * H), F32),
               pltpu.VMEM((slots, page * H, 2 * HD), BF16)]
    nq = 0
    if prompt is not None:
        qb, kb, vb, c_row, nbp, t = prompt
        tq = _pick(t, ATTN_TQ)
        nq = t // tq
        assert steps % nq == 0 and nb * steps == nbp * H * nq
        upb = steps // nq

        def unit(b, p):
            u = b * upb + p // nq
            return u // H, u % H, p % nq

        def q_map(b, p, pt):
            bb, h, i = unit(b, p)
            return (bb * nq + i, h)

        def kv_map(b, p, pt):
            bb, h, _ = unit(b, p)
            return (bb, h)

        def c_map(b, p, pt):
            bb, h, _ = unit(b, p)
            return (bb, h, 0, 0)

        def g_map(b, p, pt):
            bb, h, i = unit(b, p)
            return (bb * nq + i, 3 * H + h)

        in_specs += [pl.BlockSpec((tq, HD), q_map), pl.BlockSpec((t, HD), kv_map),
                     pl.BlockSpec((t, HD), kv_map), pl.BlockSpec((1, 1, 1, t), c_map),
                     pl.BlockSpec((tq, HD), g_map)]
        args += [qb, kb, vb, c_row, gate_src]
        out_specs.append(pl.BlockSpec((tq, HD), q_map))
        out_shape.append(jax.ShapeDtypeStruct((nbp * t, d), BF16))
        scratch.append(pltpu.VMEM((t, 2 * HD), BF16))
    outs = pl.pallas_call(
        functools.partial(_fox_decode_kernel, pps=pps, H=H, HD=HD, NQ=NQ, prompt_tiles=nq),
        grid_spec=pltpu.PrefetchScalarGridSpec(
            num_scalar_prefetch=1,
            grid=(nb, steps),
            in_specs=in_specs,
            out_specs=out_specs,
            scratch_shapes=scratch,
        ),
        out_shape=out_shape,
        compiler_params=_cparams(("arbitrary", "arbitrary")),
        name="fox_decode_attn",
    )(*args)
    return outs if prompt is not None else outs[0]


def fused_prompt_tiles_fit(nb, npages, nbp, t, H):
    steps = npages // _pick(npages, DECODE_PAGES_PER_STEP)
    nq = t // _pick(t, ATTN_TQ)
    return steps % nq == 0 and nb * steps == nbp * H * nq


def _pad_lanes(v, start=0):
    out = jnp.zeros((1, LANES), F32)
    return lax.dynamic_update_slice(out, v.reshape(1, -1).astype(F32), (0, start))


def kernel(x_prompt, x_sample, state_delta, state_conv, cache_k, cache_v, cache_logf, page_table,
           norm_mix, norm_mlp, a_w_in, a_conv, a_A_log, a_dt_bias, a_out_norm, a_w_out,
           b_w_in, b_f_bias, b_q_norm, b_k_norm, b_w_out, w_up, w_down):
    bp, t, d = x_prompt.shape
    bs, ts, _ = x_sample.shape
    depth = norm_mix.shape[0]
    n_mixers = 2
    H = state_delta.shape[2]
    DK = state_delta.shape[3]
    nqkv = state_conv.shape[-1]
    HB = cache_k.shape[3]
    HD = cache_k.shape[4]
    n_pool, page = cache_k.shape[1], cache_k.shape[2]
    npages = page_table.shape[1]
    d_ff = w_up.shape[2]
    mp, ms = bp * t, bs * ts
    assert H * DK == d and HB * HD == d and nqkv == 3 * d and DK == LANES and HD == LANES
    assert 2 * H <= LANES and HB <= LANES and 3 <= ts <= page and t >= 3

    x = jnp.concatenate([x_prompt.reshape(mp, d), x_sample.reshape(ms, d)], axis=0)
    pt_flat = page_table.reshape(-1).astype(jnp.int32)
    a_w_in_t = jnp.swapaxes(a_w_in, 1, 2)
    b_w_in_t = jnp.swapaxes(b_w_in, 1, 2)
    pad_rows = lambda w: jnp.pad(w, ((0, 0), (0, LANES - w.shape[1]), (0, 0)))
    a_w_gate_t = pad_rows(a_w_in_t[:, nqkv + d:, :])
    b_w_gate_t = pad_rows(b_w_in_t[:, 4 * d:, :])
    w_down_bf = w_down.astype(BF16)
    logf_hp = jnp.swapaxes(cache_logf, 2, 3).reshape(-1, HB, page)

    p_delta, p_conv, s_delta, s_conv = [], [], [], []
    p_k, p_v, p_lf, s_k, s_v, s_lf = [], [], [], [], [], []
    for i in range(depth):
        j = i // n_mixers
        hn = rmsnorm_bf16(x, norm_mix[i])
        if i % n_mixers == 0:
            proj = matmul_wcast(hn, a_w_in_t, j, n_out=nqkv + d, w_transposed=True)
            gates = matmul_wcast(hn, a_w_gate_t, j, n_out=LANES, w_transposed=True)
            alog_pad = _pad_lanes(a_A_log[j], H)
            dtb_pad = _pad_lanes(a_dt_bias[j], H)
            common = dict(H=H, DK=DK)
            cp = 64 if t % 64 == 0 else t
            yp, stp = gdn_core(proj, gates, a_conv[j], alog_pad, dtb_pad, a_out_norm[j], None, None,
                               row0=0, nb=bp, t=t, C=cp, G=max(1, min(H, GDN_STACK_ROWS // cp)),
                               out_dtype=BF16, **common)
            ys, sts = gdn_core(proj, gates, a_conv[j], alog_pad, dtb_pad, a_out_norm[j],
                               state_conv[j], state_delta[j],
                               row0=mp, nb=bs, t=ts, C=ts, G=H, out_dtype=F32, **common)
            x = matmul_wcast(yp, a_w_out, j, n_out=d, epilogue="residual", residual=x, row0=0)
            x = matmul_wcast(ys.astype(BF16), a_w_out, j, n_out=d, epilogue="residual", residual=x, row0=mp)
            p_conv.append(jnp.stack([proj[b * t + t - 3:(b + 1) * t, :nqkv] for b in range(bp)]))
            s_conv.append(proj[mp:].reshape(bs, ts, -1)[:, ts - 3:, :nqkv])
            p_delta.append(stp)
            s_delta.append(sts)
        else:
            proj = matmul_wcast(hn, b_w_in_t, j, n_out=4 * d, w_transposed=True)
            fproj = matmul_wcast(hn, b_w_gate_t, j, n_out=LANES, w_transposed=True)
            prep = functools.partial(fox_prep, proj, fproj, b_q_norm[j], b_k_norm[j],
                                     _pad_lanes(b_f_bias[j]), H=HB, HD=HD)
            qb_p, kf_p, kb_p, vf_p, vb_p, lf_p = prep(row0=0, rows=mp)
            qb_s, kf_s, kb_s, vf_s, vb_s, lf_s = prep(row0=mp, rows=ms)
            c_p = cumsum_rows(lf_p.reshape(bp, t, LANES))
            c_row = jnp.transpose(c_p[:, :, :HB], (0, 2, 1)).reshape(bp, HB, 1, t) * LOG2E
            lf_new_hp = jnp.swapaxes(lf_s[:, :HB].reshape(bs, ts, HB), 1, 2)
            lf_new_hp = jnp.pad(lf_new_hp, ((0, 0), (0, 0), (0, page - ts)))
            cum_hp, cnew_hp = paged_cumsum(pt_flat, logf_hp, lf_new_hp, layer_off=j * n_pool,
                                           nb=bs, npages=npages, page=page, H=HB)
            cum = jnp.swapaxes(cum_hp, 2, 3).reshape(bs, npages, 1, page * HB) * LOG2E
            cnew = jnp.swapaxes(cnew_hp[:, :, :ts], 1, 2).reshape(bs, 1, ts * HB) * LOG2E
            decode = functools.partial(
                fox_decode_attention, pt_flat, qb_s.reshape(bs, ts, d),
                cache_k.reshape(-1, page * HB, HD), cache_v.reshape(-1, page * HB, HD),
                kb_s.reshape(bs, ts * HB, HD), vb_s.reshape(bs, ts * HB, HD),
                cum, cnew, proj,
                layer_off=j * n_pool, nb=bs, npages=npages, page=page, H=HB, HD=HD, NQ=ts,
                gate_rb0=mp // ts)
            if fused_prompt_tiles_fit(bs, npages, bp, t, HB):
                os_, op = decode(prompt=(qb_p, kb_p, vb_p, c_row, bp, t))
            else:
                os_ = decode()
                op = fox_prompt_attention(qb_p, kb_p, vb_p, c_row, proj, nb=bp, t=t, H=HB, HD=HD)
            x = matmul_wcast(op, b_w_out, j, n_out=d, epilogue="residual", residual=x, row0=0)
            x = matmul_wcast(os_.astype(BF16), b_w_out, j, n_out=d, epilogue="residual", residual=x, row0=mp)
            p_k.append(kf_p.reshape(bp, t, HB, HD))
            p_v.append(vf_p.reshape(bp, t, HB, HD))
            p_lf.append(lf_p[:, :HB].reshape(bp, t, HB))
            s_k.append(kf_s.reshape(bs, ts, HB, HD))
            s_v.append(vf_s.reshape(bs, ts, HB, HD))
            s_lf.append(lf_s[:, :HB].reshape(bs, ts, HB))
        hm = rmsnorm_bf16(x, norm_mlp[i])
        a = matmul_wcast(hm, w_up, i, n_out=d_ff, epilogue="relu2", out_dtype=BF16)
        down = functools.partial(matmul_fullk_residual, a, w_down_bf, i, x)
        if i + 1 < depth:
            x = down(row0=0, rows=mp + ms)
    y_p, y_s = down(row0=0, rows=mp), down(row0=mp, rows=ms)
    return (y_p.reshape(bp, t, d), y_s.reshape(bs, ts, d),
            jnp.stack(p_delta), jnp.stack(p_conv), jnp.stack(p_k), jnp.stack(p_v), jnp.stack(p_lf),
            jnp.stack(s_delta), jnp.stack(s_conv), jnp.stack(s_k), jnp.stack(s_v), jnp.stack(s_lf))
```

```python
import functools

import jax
import jax.numpy as jnp
from jax import lax
from jax.experimental import pallas as pl
from jax.experimental.pallas import tpu as pltpu

F32 = jnp.float32
BF16 = jnp.bfloat16
EPS = 1e-6
NEG = -1e30
LOG2E = 1.4426950408889634
LANES = 128
SUBLANES = 8
MXU_DIM = 256
VMEM_LIMIT = 56 * 1024 * 1024

ROW_TILES = (768, 512, 528, 384, 256, 128, 64, 32, 16, 8)
WCAST_ROW_TILES = (1056,) + ROW_TILES
COL_TILE = 1024
FULLK_COL_TILE = 512
ATTN_TQ = (512, 256, 128)
ATTN_TILES_PER_STEP = (4, 2, 1)
ATTN_HEADS_PER_STEP = (2, 1)
DECODE_PAGES_PER_STEP = (8, 4, 2, 1)
DECODE_KEY_CHUNK = (256, 128)
CUMSUM_PAGES_PER_STEP = (32, 16, 8, 4, 2, 1)
GDN_TOKENS_PER_STEP = 128
GDN_STACK_ROWS = 128

def _pick(n, cands):
    for c in cands:
        if n % c == 0:
            return c
    return n


def _cparams(sem):
    return pltpu.CompilerParams(dimension_semantics=sem, vmem_limit_bytes=VMEM_LIMIT)


def _dot(a, b):
    return jnp.dot(a.astype(BF16), b.astype(BF16), preferred_element_type=F32)


def _dot_nt(a, b):
    return lax.dot_general(a.astype(BF16), b.astype(BF16), (((1,), (1,)), ((), ())),
                           preferred_element_type=F32)


def _split3(x):
    hi = x.astype(BF16)
    r = x - hi.astype(F32)
    mid = r.astype(BF16)
    lo = (r - mid.astype(F32)).astype(BF16)
    return hi, mid, lo


def _dot_exact_lhs01(a01, b):
    hi, mid, lo = _split3(b)
    a = a01.astype(BF16)
    d = functools.partial(jnp.dot, preferred_element_type=F32)
    return d(a, hi) + (d(a, mid) + d(a, lo))


def _sigmoid(x):
    return 1.0 / (1.0 + jnp.exp(-x))


def _softplus(x):
    return jnp.maximum(x, 0.0) + jnp.log(1.0 + jnp.exp(-jnp.abs(x)))


def _rmsnorm_kernel(x_ref, w_ref, o_ref):
    x = x_ref[...]
    ms = jnp.mean(x * x, axis=-1, keepdims=True)
    o_ref[...] = (x * lax.rsqrt(ms + EPS) * w_ref[...]).astype(o_ref.dtype)


def rmsnorm_bf16(x, w):
    m, d = x.shape
    tm = _pick(m, ROW_TILES)
    return pl.pallas_call(
        _rmsnorm_kernel,
        grid=(m // tm,),
        in_specs=[pl.BlockSpec((tm, d), lambda i: (i, 0)),
                  pl.BlockSpec((1, d), lambda i: (0, 0))],
        out_specs=pl.BlockSpec((tm, d), lambda i: (i, 0)),
        out_shape=jax.ShapeDtypeStruct((m, d), BF16),
        compiler_params=_cparams(("parallel",)),
        name="rmsnorm",
    )(x, w.reshape(1, d))


def _epilogue(acc, kind, r_ref):
    if kind == "relu2":
        acc = jnp.maximum(acc, 0.0)
        acc = acc * acc
    elif kind == "residual":
        acc = acc + r_ref[...]
    return acc


def _mm_wcast_kernel(*refs, epilogue, w_transposed):
    if epilogue == "residual":
        a_ref, w_ref, r_ref, o_ref, wb_ref = refs
    else:
        a_ref, w_ref, o_ref, wb_ref = refs
        r_ref = None

    @pl.when(pl.program_id(1) == 0)
    def _():
        wb_ref[...] = w_ref[...].astype(wb_ref.dtype)

    if w_transposed:
        acc = _dot_nt(a_ref[...], wb_ref[...])
    else:
        acc = jnp.dot(a_ref[...], wb_ref[...], preferred_element_type=F32)
    o_ref[...] = _epilogue(acc, epilogue, r_ref).astype(o_ref.dtype)


def matmul_wcast(a, w, layer, *, n_out, w_transposed=False, epilogue="none", residual=None,
                 out_dtype=F32, row0=0):
    ma, kdim = a.shape
    tm = _pick(ma, WCAST_ROW_TILES if epilogue != "residual" else (1024, 512, 256, 128, 64, 32, 16, 8))
    tn = min(COL_TILE, n_out)
    assert n_out % tn == 0 and row0 % tm == 0 and w.shape[2 if w_transposed else 1] == kdim
    r0 = row0 // tm
    if w_transposed:
        w_spec = pl.BlockSpec((None, tn, kdim), lambda j, i: (layer, j, 0))
        wb_shape = (tn, kdim)
    else:
        w_spec = pl.BlockSpec((None, kdim, tn), lambda j, i: (layer, 0, j))
        wb_shape = (kdim, tn)
    in_specs = [pl.BlockSpec((tm, kdim), lambda j, i: (i, 0)), w_spec]
    args = [a, w]
    aliases = {}
    if epilogue == "residual":
        in_specs.append(pl.BlockSpec((tm, tn), lambda j, i: (r0 + i, j)))
        args.append(residual)
        aliases = {2: 0}
        out_rows = residual.shape[0]
    else:
        out_rows = ma
    return pl.pallas_call(
        functools.partial(_mm_wcast_kernel, epilogue=epilogue, w_transposed=w_transposed),
        grid=(n_out // tn, ma // tm),
        in_specs=in_specs,
        out_specs=pl.BlockSpec((tm, tn), lambda j, i: (r0 + i, j)),
        out_shape=jax.ShapeDtypeStruct((out_rows, n_out), out_dtype),
        scratch_shapes=[pltpu.VMEM(wb_shape, BF16)],
        input_output_aliases=aliases,
        compiler_params=_cparams(("arbitrary", "arbitrary")),
        name="matmul_wcast_" + epilogue,
    )(*args)


def _mm_fullk_kernel(a_ref, b_ref, r_ref, o_ref):
    acc = jnp.dot(a_ref[...], b_ref[...], preferred_element_type=F32)
    o_ref[...] = (acc + r_ref[...]).astype(o_ref.dtype)


def matmul_fullk_residual(a, b, layer, residual, *, row0, rows):
    _, kdim = a.shape
    n = b.shape[2]
    tm = _pick(rows, ROW_TILES)
    tn = min(FULLK_COL_TILE, n)
    assert n % tn == 0 and row0 % tm == 0
    r0 = row0 // tm
    return pl.pallas_call(
        _mm_fullk_kernel,
        grid=(rows // tm, n // tn),
        in_specs=[pl.BlockSpec((tm, kdim), lambda i, j: (r0 + i, 0)),
                  pl.BlockSpec((None, kdim, tn), lambda i, j: (layer, 0, j)),
                  pl.BlockSpec((tm, tn), lambda i, j: (r0 + i, j))],
        out_specs=pl.BlockSpec((tm, tn), lambda i, j: (i, j)),
        out_shape=jax.ShapeDtypeStruct((rows, n), F32),
        compiler_params=_cparams(("parallel", "arbitrary")),
        name="matmul_fullk_residual",
    )(a, b, residual)


def _gdn_kernel(*refs, C, G, H, DK, NS, has_cbuf, has_s0):
    it = iter(refs)
    seq_refs = [(next(it), next(it), next(it)) for _ in range(NS)]
    cw_ref = next(it)
    alog_ref = next(it)
    dtb_ref = next(it)
    onw_ref = next(it)
    cbuf_refs = [next(it) for _ in range(NS)] if has_cbuf else None
    s0_refs = [next(it) for _ in range(NS)] if has_s0 else None
    y_ref = next(it)
    sfin_ref = next(it)
    ext_ref = next(it)
    scat_ref = next(it)

    c_idx = pl.program_id(1)
    nc = pl.num_programs(1)
    R = G * C
    NG = H // G
    NQK = H * DK
    pad = SUBLANES

    @pl.when(c_idx == 0)
    def _init():
        for s in range(NS):
            ext_ref[s, 0:pad, :] = jnp.zeros((pad, ext_ref.shape[2]), F32)
            if has_cbuf:
                ext_ref[s, pad - 3:pad, :] = cbuf_refs[s][0]
            for h in range(H):
                if has_s0:
                    scat_ref[s, :, h * DK:(h + 1) * DK] = s0_refs[s][0, h]
                else:
                    scat_ref[s, :, h * DK:(h + 1) * DK] = jnp.zeros((DK, DK), F32)

    for s in range(NS):
        ext_ref[s, pad:pad + C, :] = seq_refs[s][0][...]

    def conv_act(s, col):
        sl = slice(col, col + DK)
        acc = ext_ref[s, pad:pad + C, sl] * cw_ref[3:4, sl]
        for i in range(3):
            acc = acc + ext_ref[s, pad - 3 + i:pad - 3 + i + C, sl] * cw_ref[i:i + 1, sl]
        return acc * _sigmoid(acc)

    beta_full = [_sigmoid(seq_refs[s][2][...]) for s in range(NS)]
    g_full = [-jnp.exp(alog_ref[...]) * _softplus(seq_refs[s][2][...] + dtb_ref[...]) for s in range(NS)]

    iotas = (lax.broadcasted_iota(jnp.int32, (R, R), 0), lax.broadcasted_iota(jnp.int32, (R, R), 1))

    def blk(axis, m):
        return lax.shift_right_logical(iotas[axis], m.bit_length() - 1)

    same = blk(0, C) == blk(1, C)
    eye = iotas[0] == iotas[1]
    incl = same & (iotas[1] <= iotas[0])
    strict = same & (iotas[1] < iotas[0])
    incl_t = same & (iotas[0] <= iotas[1])
    rowblk = lax.shift_right_logical(lax.broadcasted_iota(jnp.int32, (R, DK), 0), C.bit_length() - 1)
    cat = (lambda xs: xs[0]) if G == 1 else (lambda xs: jnp.concatenate(xs, axis=0))
    groups = [list(range(gi * G, (gi + 1) * G)) for gi in range(NG)]

    def group_program(s, heads):
        z_ref = seq_refs[s][1]
        qs, ks, vs = [], [], []
        for h in heads:
            q = conv_act(s, h * DK)
            k = conv_act(s, NQK + h * DK)
            qs.append(q * lax.rsqrt(jnp.sum(q * q, axis=-1, keepdims=True) + EPS) * (DK ** -0.5))
            ks.append(k * lax.rsqrt(jnp.sum(k * k, axis=-1, keepdims=True) + EPS))
            vs.append(conv_act(s, 2 * NQK + h * DK))
        Q, K, V = cat(qs), cat(ks), cat(vs)
        beta = cat([beta_full[s][:, h:h + 1] for h in heads])
        g_c = cat([g_full[s][:, H + h:H + h + 1] for h in heads])
        yield
        g_r = jnp.sum(jnp.where(eye, g_c, 0.0), axis=0, keepdims=True)
        gc = jnp.sum(jnp.where(incl, g_r, 0.0), axis=1, keepdims=True)
        gc_r = jnp.sum(jnp.where(incl_t, g_c, 0.0), axis=0, keepdims=True)
        gl = jnp.sum(jnp.where(same, g_r, 0.0), axis=1, keepdims=True)
        decay = jnp.exp(jnp.where(incl, gc - gc_r, NEG))
        KB = K * beta
        lmat = jnp.where(strict, _dot_nt(KB, K) * decay, 0.0)
        amat = jnp.where(incl, _dot_nt(Q, K) * decay, 0.0)
        yield
        ld = jnp.where(blk(0, SUBLANES) == blk(1, SUBLANES), lmat, 0.0)
        ld2 = _dot(ld, ld)
        yield
        ld4 = _dot(ld2, ld2)
        x = jnp.where(eye, 1.0, 0.0) - ld
        x = x + _dot(x, ld2)
        yield
        x = x + _dot(x, ld4)
        yield
        m = SUBLANES
        while m < C:
            sel = (blk(0, 2 * m) == blk(1, 2 * m)) & (blk(0, m) != blk(1, m))
            ox = _dot(jnp.where(sel, lmat, 0.0), x)
            yield
            x = x - _dot(x, ox)
            yield
            m *= 2
        eg = jnp.exp(gc)
        UW = _dot(x, jnp.concatenate([V * beta, KB * eg], axis=1))
        yield
        sc = scat_ref[s, :, heads[0] * DK:(heads[-1] + 1) * DK]
        WQS = _dot(jnp.concatenate([UW[:, DK:], Q * eg], axis=0), sc)
        yield
        vn, oi = [], []
        for j in range(G):
            rs = slice(j * C, (j + 1) * C)
            cs = slice(j * DK, (j + 1) * DK)
            vn.append(UW[rs, :DK] - WQS[rs, cs])
            oi.append(WQS[R + j * C:R + (j + 1) * C, cs])
        Vn = cat(vn)
        O = cat(oi) + _dot(amat, Vn)
        ke = K * jnp.exp(gl - gc)
        vwide = jnp.concatenate([jnp.where(rowblk == j, Vn, 0.0) for j in range(G)], axis=1)
        dS = _dot(ke.T, vwide)
        yield
        for j, h in enumerate(heads):
            cs = slice(j * DK, (j + 1) * DK)
            e_last = jnp.exp(gl[j * C:j * C + 1, :])
            scat_ref[s, :, h * DK:(h + 1) * DK] = sc[:, cs] * e_last + dS[:, cs]
            o = O[j * C:(j + 1) * C]
            o = o * lax.rsqrt(jnp.mean(o * o, axis=-1, keepdims=True) + EPS) * onw_ref[...]
            zg = z_ref[:, h * DK:(h + 1) * DK]
            y_ref[s, :, h * DK:(h + 1) * DK] = (o * (zg * _sigmoid(zg))).astype(y_ref.dtype)

    programs = [group_program(s, heads) for s in range(NS) for heads in groups]
    while programs:
        running = []
        for prog in programs:
            try:
                next(prog)
                running.append(prog)
            except StopIteration:
                pass
        programs = running

    for s in range(NS):
        carry = ext_ref[s, C:C + pad, :]
        ext_ref[s, 0:pad, :] = carry

    @pl.when(c_idx == nc - 1)
    def _fin():
        for s in range(NS):
            for h in range(H):
                sfin_ref[s, h] = scat_ref[s, :, h * DK:(h + 1) * DK]


def gdn_core(proj, gates, conv_w, alog_pad, dtb_pad, out_norm, conv_buf, s0, *,
             row0, nb, t, C, G, H, DK, out_dtype):
    nqkv = 3 * H * DK
    nc = t // C
    assert t % C == 0 and row0 % C == 0 and H % G == 0 and C & (C - 1) == 0
    NS = _pick(nb, tuple(n for n in (4, 2, 1) if n * C <= GDN_TOKENS_PER_STEP))
    rb0 = row0 // C
    has_cbuf = conv_buf is not None
    has_s0 = s0 is not None

    def rows(s, col):
        return lambda bb, c: (rb0 + (bb * NS + s) * nc + c, col)

    const = lambda bb, c: (0, 0)
    in_specs, args = [], []
    for s in range(NS):
        in_specs += [pl.BlockSpec((C, nqkv), rows(s, 0)),
                     pl.BlockSpec((C, H * DK), rows(s, 3)),
                     pl.BlockSpec((C, LANES), rows(s, 0))]
        args += [proj, proj, gates]
    in_specs += [pl.BlockSpec((4, nqkv), const), pl.BlockSpec((1, LANES), const),
                 pl.BlockSpec((1, LANES), const), pl.BlockSpec((1, DK), const)]
    args += [conv_w, alog_pad, dtb_pad, out_norm.reshape(1, DK)]
    if has_cbuf:
        for s in range(NS):
            in_specs.append(pl.BlockSpec((1, 3, nqkv), lambda bb, c, s=s: (bb * NS + s, 0, 0)))
            args.append(conv_buf)
    if has_s0:
        for s in range(NS):
            in_specs.append(pl.BlockSpec((1, H, DK, DK), lambda bb, c, s=s: (bb * NS + s, 0, 0, 0)))
            args.append(s0)
    y, sfin = pl.pallas_call(
        functools.partial(_gdn_kernel, C=C, G=G, H=H, DK=DK, NS=NS, has_cbuf=has_cbuf, has_s0=has_s0),
        grid=(nb // NS, nc),
        in_specs=in_specs,
        out_specs=[pl.BlockSpec((NS, C, H * DK), lambda bb, c: (bb, c, 0)),
                   pl.BlockSpec((NS, H, DK, DK), lambda bb, c: (bb, 0, 0, 0))],
        out_shape=[jax.ShapeDtypeStruct((nb, t, H * DK), out_dtype),
                   jax.ShapeDtypeStruct((nb, H, DK, DK), F32)],
        scratch_shapes=[pltpu.VMEM((NS, C + SUBLANES, nqkv), F32),
                        pltpu.VMEM((NS, DK, H * DK), F32)],
        compiler_params=_cparams(("parallel", "arbitrary")),
        name="gdn_core_c%d" % C,
    )(*args)
    return y.reshape(nb * t, H * DK), sfin


def _fox_prep_kernel(q_ref, k_ref, v_ref, f_ref, qn_ref, kn_ref, fb_ref,
                     qo_ref, ko_ref, kb_ref, vo_ref, vb_ref, lf_ref, *, H, HD):
    q_scale = (HD ** -0.5) * LOG2E
    for h in range(H):
        sl = slice(h * HD, (h + 1) * HD)
        q = q_ref[:, sl]
        q = q * lax.rsqrt(jnp.mean(q * q, axis=-1, keepdims=True) + EPS) * qn_ref[...]
        qo_ref[:, sl] = (q * q_scale).astype(qo_ref.dtype)
        k = k_ref[:, sl]
        k = k * lax.rsqrt(jnp.mean(k * k, axis=-1, keepdims=True) + EPS) * kn_ref[...]
        ko_ref[:, sl] = k
        kb_ref[:, sl] = k.astype(kb_ref.dtype)
    v = v_ref[...]
    vo_ref[...] = v
    vb_ref[...] = v.astype(vb_ref.dtype)
    x = f_ref[...] + fb_ref[...]
    lf_ref[...] = jnp.minimum(x, 0.0) - jnp.log1p(jnp.exp(-jnp.abs(x)))


def fox_prep(proj, fproj, q_norm, k_norm, fb_pad, *, row0, rows, H, HD):
    d = H * HD
    tm = _pick(rows, (256, 128, 64, 32, 16, 8))
    assert row0 % tm == 0
    r0 = row0 // tm
    col = lambda j: (lambda i: (r0 + i, j))
    out = lambda i: (i, 0)
    return pl.pallas_call(
        functools.partial(_fox_prep_kernel, H=H, HD=HD),
        grid=(rows // tm,),
        in_specs=[pl.BlockSpec((tm, d), col(0)), pl.BlockSpec((tm, d), col(1)),
                  pl.BlockSpec((tm, d), col(2)), pl.BlockSpec((tm, LANES), col(0)),
                  pl.BlockSpec((1, HD), lambda i: (0, 0)), pl.BlockSpec((1, HD), lambda i: (0, 0)),
                  pl.BlockSpec((1, LANES), lambda i: (0, 0))],
        out_specs=[pl.BlockSpec((tm, d), out)] * 5 + [pl.BlockSpec((tm, LANES), out)],
        out_shape=[jax.ShapeDtypeStruct((rows, d), BF16), jax.ShapeDtypeStruct((rows, d), F32),
                   jax.ShapeDtypeStruct((rows, d), BF16), jax.ShapeDtypeStruct((rows, d), F32),
                   jax.ShapeDtypeStruct((rows, d), BF16), jax.ShapeDtypeStruct((rows, LANES), F32)],
        compiler_params=_cparams(("parallel",)),
        name="fox_prep",
    )(proj, proj, proj, fproj, q_norm.reshape(1, HD), k_norm.reshape(1, HD), fb_pad)


def _cumsum_block(x, carry):
    n = x.shape[0]
    ri = lax.broadcasted_iota(jnp.int32, (n, n), 0)
    ci = lax.broadcasted_iota(jnp.int32, (n, n), 1)
    tril = jnp.where(ci <= ri, 1.0, 0.0)
    return _dot_exact_lhs01(tril, x) + carry


def _cumsum_kernel(x_ref, o_ref, carry_ref):
    @pl.when(pl.program_id(1) == 0)
    def _():
        carry_ref[...] = jnp.zeros_like(carry_ref)

    cs = _cumsum_block(x_ref[0], carry_ref[...])
    o_ref[0] = cs
    carry_ref[...] = cs[cs.shape[0] - 1:, :]


def cumsum_rows(x):
    b, t, w = x.shape
    tb = _pick(t, (256, 128, 64, 32, 16, 8))
    return pl.pallas_call(
        _cumsum_kernel,
        grid=(b, t // tb),
        in_specs=[pl.BlockSpec((1, tb, w), lambda i, j: (i, j, 0))],
        out_specs=pl.BlockSpec((1, tb, w), lambda i, j: (i, j, 0)),
        out_shape=jax.ShapeDtypeStruct((b, t, w), F32),
        scratch_shapes=[pltpu.VMEM((1, w), F32)],
        compiler_params=_cparams(("parallel", "arbitrary")),
        name="cumsum_rows",
    )(x)


def _cumsum_lanes(x, carry):
    n = x.shape[1]
    ri = lax.broadcasted_iota(jnp.int32, (n, n), 0)
    ci = lax.broadcasted_iota(jnp.int32, (n, n), 1)
    triu = jnp.where(ri <= ci, 1.0, 0.0).astype(BF16)
    hi, mid, lo = _split3(x)
    d = functools.partial(jnp.dot, preferred_element_type=F32)
    return d(hi, triu) + (d(mid, triu) + d(lo, triu)) + carry


def _paged_cumsum_kernel(pt_ref, *refs, pps):
    lp_refs = refs[:pps]
    ln_ref, o_ref, on_ref, carry_ref = refs[pps:]
    p = pl.program_id(1)

    @pl.when(p == 0)
    def _():
        carry_ref[...] = jnp.zeros_like(carry_ref)

    h = lp_refs[0].shape[1]
    local = _cumsum_lanes(jnp.concatenate([r[0] for r in lp_refs], axis=0) if pps > 1 else lp_refs[0][0], 0.0)
    carry = carry_ref[...]
    for j in range(pps):
        lj = local[j * h:(j + 1) * h]
        o_ref[0, j] = lj + carry
        carry = carry + lj[:, lj.shape[1] - 1:]
    carry_ref[...] = carry

    @pl.when(p == pl.num_programs(1) - 1)
    def _():
        on_ref[0] = _cumsum_lanes(ln_ref[0], carry)


def paged_cumsum(page_table_flat, logf_hp, lf_new_hp, *, layer_off, nb, npages, page, H):
    pps = _pick(npages, CUMSUM_PAGES_PER_STEP)
    ts = lf_new_hp.shape[2]
    assert ts == page

    def pool_map(j):
        return lambda b, p, pt: (layer_off + pt[b * npages + p * pps + j], 0, 0)

    return pl.pallas_call(
        functools.partial(_paged_cumsum_kernel, pps=pps),
        grid_spec=pltpu.PrefetchScalarGridSpec(
            num_scalar_prefetch=1,
            grid=(nb, npages // pps),
            in_specs=[pl.BlockSpec((1, H, page), pool_map(j)) for j in range(pps)]
            + [pl.BlockSpec((1, H, ts), lambda b, p, pt: (b, 0, 0))],
            out_specs=[pl.BlockSpec((1, pps, H, page), lambda b, p, pt: (b, p, 0, 0)),
                       pl.BlockSpec((1, H, ts), lambda b, p, pt: (b, 0, 0))],
            scratch_shapes=[pltpu.VMEM((H, 1), F32)],
        ),
        out_shape=[jax.ShapeDtypeStruct((nb, npages, H, page), F32),
                   jax.ShapeDtypeStruct((nb, H, ts), F32)],
        compiler_params=_cparams(("parallel", "arbitrary")),
        name="paged_cumsum",
    )(page_table_flat, *([logf_hp] * pps), lf_new_hp)


def _fox_prompt_kernel(q_ref, k_ref, v_ref, c_ref, *refs, tq, nq, tiles_per_step, hd):
    g_refs = refs[:tiles_per_step]
    o_ref, vx_ref = refs[tiles_per_step:]
    step = pl.program_id(2)
    steps = nq // tiles_per_step
    heads = k_ref.shape[1] // hd

    @pl.when(step == 0)
    def _():
        for hh in range(heads):
            vx_ref[hh, :, :hd] = v_ref[:, hh * hd:(hh + 1) * hd].astype(vx_ref.dtype)
            vx_ref[hh, :, hd:] = jnp.ones((vx_ref.shape[1], hd), vx_ref.dtype)

    def tile(i, slot, hh):
        cs = slice(hh * hd, (hh + 1) * hd)
        kext = (i + 1) * tq
        s = _dot_nt(q_ref[0, slot, :, cs], k_ref[0:kext, cs]) - c_ref[0, hh, :, 0:kext]
        rpos = lax.broadcasted_iota(jnp.int32, (tq, tq), 0)
        cpos = lax.broadcasted_iota(jnp.int32, (tq, tq), 1)
        diag = jnp.where(cpos <= rpos, s[:, kext - tq:], NEG)
        s = diag if i == 0 else jnp.concatenate([s[:, :kext - tq], diag], axis=1)
        m = jnp.max(s, axis=1, keepdims=True)
        pv = _dot(jnp.exp2(s - m), vx_ref[hh, 0:kext, :])
        gate = g_refs[slot][:, cs]
        o_ref[0, slot, :, cs] = ((pv[:, :hd] / pv[:, hd:]) * _sigmoid(gate)).astype(o_ref.dtype)

    def tiles_of_step(p):
        for slot in range(tiles_per_step):
            for hh in range(heads):
                tile(slot * steps + p, slot, hh)

    for p in range(steps):
        pl.when(step == p)(functools.partial(tiles_of_step, p))


def fox_prompt_attention(qb, kb, vb, c_row, proj, *, nb, t, H, HD):
    tq = _pick(t, ATTN_TQ)
    nq = t // tq
    tps = _pick(nq, ATTN_TILES_PER_STEP)
    steps = nq // tps
    hp = _pick(H, ATTN_HEADS_PER_STEP)
    w = hp * HD
    d = H * HD
    assert (3 * H) % hp == 0
    gate_specs = [pl.BlockSpec((tq, w), lambda b, g, p, slot=slot: (b * nq + slot * steps + p, 3 * H // hp + g))
                  for slot in range(tps)]
    out = pl.pallas_call(
        functools.partial(_fox_prompt_kernel, tq=tq, nq=nq, tiles_per_step=tps, hd=HD),
        grid=(nb, H // hp, steps),
        in_specs=[pl.BlockSpec((1, tps, tq, w), lambda b, g, p: (b, 0, p, g)),
                  pl.BlockSpec((t, w), lambda b, g, p: (b, g)),
                  pl.BlockSpec((t, w), lambda b, g, p: (b, g)),
                  pl.BlockSpec((1, hp, 1, t), lambda b, g, p: (b, g, 0, 0))] + gate_specs,
        out_specs=pl.BlockSpec((1, tps, tq, w), lambda b, g, p: (b, 0, p, g)),
        out_shape=jax.ShapeDtypeStruct((nb, tps, t // tps, d), BF16),
        scratch_shapes=[pltpu.VMEM((hp, t, 2 * HD), BF16)],
        compiler_params=_cparams(("parallel", "parallel", "arbitrary")),
        name="fox_prompt_attn",
    )(qb.reshape(nb, tps, t // tps, d), kb, vb, c_row, *([proj] * tps))
    return out.reshape(nb * t, d)


def _fox_decode_kernel(pt_ref, *refs, pps, H, HD, NQ):
    q_ref = refs[0]
    kp_refs = refs[1:1 + pps]
    vp_refs = refs[1 + pps:1 + 2 * pps]
    (kn_ref, vn_ref, cum_ref, cn_ref, g_ref,
     o_ref, qcat_ref, mask_ref, m_ref, acc_ref, s_ref, vx_ref) = refs[1 + 2 * pps:]
    p = pl.program_id(1)
    rows = H * NQ
    kc = mask_ref.shape[1]
    width = kp_refs[0].shape[1]

    @pl.when(p == 0)
    def _():
        q = q_ref[0].astype(F32)
        for h in range(H):
            qcat_ref[h * NQ:(h + 1) * NQ, :] = q[:, h * HD:(h + 1) * HD]
        rh = lax.broadcasted_iota(jnp.int32, (rows, kc), 0) // NQ
        ch = lax.broadcasted_iota(jnp.int32, (rows, kc), 1) % H
        mask_ref[...] = jnp.where(rh == ch, 0.0, NEG)
        m_ref[...] = jnp.full_like(m_ref, NEG)
        acc_ref[...] = jnp.zeros_like(acc_ref)
        vx_ref[:, :, HD:] = jnp.ones((vx_ref.shape[0], width, HD), vx_ref.dtype)

    qc = qcat_ref[...].astype(BF16)

    chunks = range(0, width, kc)
    slots = vx_ref.shape[0]

    def scores(j, c0, mx):
        s = _dot_nt(qc, kp_refs[j][0, c0:c0 + kc, :]) + (mask_ref[...] - cum_ref[0, j, :, c0:c0 + kc])
        s_ref[:, (j % slots) * width + c0:(j % slots) * width + c0 + kc] = s
        return jnp.maximum(mx, s)

    def weighted(j, c0, m_new, pv):
        pr = jnp.exp2(s_ref[:, (j % slots) * width + c0:(j % slots) * width + c0 + kc] - m_new)
        return pv + _dot(pr, vx_ref[j % slots, c0:c0 + kc, :])

    def stage_values(j):
        vx_ref[j % slots, :, :HD] = vp_refs[j][0].astype(vx_ref.dtype)

    m, acc = m_ref[...], acc_ref[...]
    neg = jnp.full((rows, kc), NEG, F32)
    stage_values(0)
    mx = neg
    for c0 in chunks:
        mx = scores(0, c0, mx)
    for j in range(pps):
        m_new = jnp.maximum(m, jnp.max(mx, axis=1, keepdims=True))
        pv = jnp.zeros((rows, 2 * HD), F32)
        nxt = j + 1 < pps
        if nxt:
            stage_values(j + 1)
        mx = neg
        for c0 in chunks:
            pv = weighted(j, c0, m_new, pv)
            if nxt:
                mx = scores(j + 1, c0, mx)
        acc = jnp.exp2(m - m_new) * acc + pv
        m = m_new
    m_ref[...], acc_ref[...] = m, acc

    @pl.when(p == pl.num_programs(1) - 1)
    def _():
        wn = kn_ref.shape[1]
        r = lax.broadcasted_iota(jnp.int32, (rows, wn), 0)
        c = lax.broadcasted_iota(jnp.int32, (rows, wn), 1)
        ok = (r // NQ == c % H) & (c // H <= r % NQ)
        s = jnp.where(ok, _dot_nt(qc, kn_ref[0]) - cn_ref[0], NEG)
        m_fin = jnp.maximum(m, jnp.max(s, axis=1, keepdims=True))
        vn = vn_ref[0]
        fin = (jnp.exp2(m - m_fin) * acc
               + _dot(jnp.exp2(s - m_fin), jnp.concatenate([vn, jnp.ones_like(vn)], axis=1)))
        out = fin[:, :HD] / fin[:, HD:]
        for h in range(H):
            cs = slice(h * HD, (h + 1) * HD)
            o_ref[:, cs] = (out[h * NQ:(h + 1) * NQ, :] * _sigmoid(g_ref[:, cs])).astype(o_ref.dtype)


def fox_decode_attention(page_table_flat, q_s, cache_k, cache_v, kn, vn, cum, cnew, gate_src, *,
                         layer_off, nb, npages, page, H, HD, NQ, gate_rb0):
    d = H * HD
    pps = _pick(npages, DECODE_PAGES_PER_STEP)
    rows = H * NQ
    kc = _pick(page * H, DECODE_KEY_CHUNK)
    slots = min(2, pps)
    assert kc % H == 0 and (page * H) % kc == 0

    def pool_map(j):
        return lambda b, p, pt: (layer_off + pt[b * npages + p * pps + j], 0, 0)

    per_b = lambda b, p, pt: (b, 0, 0)
    pool_specs = [pl.BlockSpec((1, page * H, HD), pool_map(j)) for j in range(pps)]
    return pl.pallas_call(
        functools.partial(_fox_decode_kernel, pps=pps, H=H, HD=HD, NQ=NQ),
        grid_spec=pltpu.PrefetchScalarGridSpec(
            num_scalar_prefetch=1,
            grid=(nb, npages // pps),
            in_specs=[pl.BlockSpec((1, NQ, d), per_b)] + pool_specs + pool_specs
            + [pl.BlockSpec((1, NQ * H, HD), per_b),
               pl.BlockSpec((1, NQ * H, HD), per_b),
               pl.BlockSpec((1, pps, 1, page * H), lambda b, p, pt: (b, p, 0, 0)),
               pl.BlockSpec((1, 1, NQ * H), per_b),
               pl.BlockSpec((NQ, d), lambda b, p, pt: (gate_rb0 + b, 3))],
            out_specs=pl.BlockSpec((NQ, d), lambda b, p, pt: (b, 0)),
            scratch_shapes=[pltpu.VMEM((rows, HD), F32),
                            pltpu.VMEM((rows, kc), F32),
                            pltpu.VMEM((rows, 1), F32),
                            pltpu.VMEM((rows, 2 * HD), F32),
                            pltpu.VMEM((rows, slots * page * H), F32),
                            pltpu.VMEM((slots, page * H, 2 * HD), BF16)],
        ),
        out_shape=jax.ShapeDtypeStruct((nb * NQ, d), F32),
        compiler_params=_cparams(("parallel", "arbitrary")),
        name="fox_decode_attn",
    )(page_table_flat, q_s, *([cache_k] * pps), *([cache_v] * pps), kn, vn, cum, cnew, gate_src)


def _pad_lanes(v, start=0):
    out = jnp.zeros((1, LANES), F32)
    return lax.dynamic_update_slice(out, v.reshape(1, -1).astype(F32), (0, start))


def kernel(x_prompt, x_sample, state_delta, state_conv, cache_k, cache_v, cache_logf, page_table,
           norm_mix, norm_mlp, a_w_in, a_conv, a_A_log, a_dt_bias, a_out_norm, a_w_out,
           b_w_in, b_f_bias, b_q_norm, b_k_norm, b_w_out, w_up, w_down):
    bp, t, d = x_prompt.shape
    bs, ts, _ = x_sample.shape
    depth = norm_mix.shape[0]
    n_mixers = 2
    H = state_delta.shape[2]
    DK = state_delta.shape[3]
    nqkv = state_conv.shape[-1]
    HB = cache_k.shape[3]
    HD = cache_k.shape[4]
    n_pool, page = cache_k.shape[1], cache_k.shape[2]
    npages = page_table.shape[1]
    d_ff = w_up.shape[2]
    mp, ms = bp * t, bs * ts
    assert H * DK == d and HB * HD == d and nqkv == 3 * d and DK == LANES and HD == LANES
    assert 2 * H <= LANES and HB <= LANES and 3 <= ts <= page and t >= 3

    x = jnp.concatenate([x_prompt.reshape(mp, d), x_sample.reshape(ms, d)], axis=0)
    pt_flat = page_table.reshape(-1).astype(jnp.int32)
    a_w_in_t = jnp.swapaxes(a_w_in, 1, 2)
    b_w_in_t = jnp.swapaxes(b_w_in, 1, 2)
    pad_rows = lambda w: jnp.pad(w, ((0, 0), (0, LANES - w.shape[1]), (0, 0)))
    a_w_gate_t = pad_rows(a_w_in_t[:, nqkv + d:, :])
    b_w_gate_t = pad_rows(b_w_in_t[:, 4 * d:, :])
    w_down_bf = w_down.astype(BF16)
    logf_hp = jnp.swapaxes(cache_logf, 2, 3).reshape(-1, HB, page)

    p_delta, p_conv, s_delta, s_conv = [], [], [], []
    p_k, p_v, p_lf, s_k, s_v, s_lf = [], [], [], [], [], []
    for i in range(depth):
        j = i // n_mixers
        hn = rmsnorm_bf16(x, norm_mix[i])
        if i % n_mixers == 0:
            proj = matmul_wcast(hn, a_w_in_t, j, n_out=nqkv + d, w_transposed=True)
            gates = matmul_wcast(hn, a_w_gate_t, j, n_out=LANES, w_transposed=True)
            alog_pad = _pad_lanes(a_A_log[j], H)
            dtb_pad = _pad_lanes(a_dt_bias[j], H)
            common = dict(H=H, DK=DK)
            cp = 64 if t % 64 == 0 else t
            yp, stp = gdn_core(proj, gates, a_conv[j], alog_pad, dtb_pad, a_out_norm[j], None, None,
                               row0=0, nb=bp, t=t, C=cp, G=max(1, min(H, GDN_STACK_ROWS // cp)),
                               out_dtype=BF16, **common)
            ys, sts = gdn_core(proj, gates, a_conv[j], alog_pad, dtb_pad, a_out_norm[j],
                               state_conv[j], state_delta[j],
                               row0=mp, nb=bs, t=ts, C=ts, G=H, out_dtype=F32, **common)
            x = matmul_wcast(yp, a_w_out, j, n_out=d, epilogue="residual", residual=x, row0=0)
            x = matmul_wcast(ys.astype(BF16), a_w_out, j, n_out=d, epilogue="residual", residual=x, row0=mp)
            p_conv.append(jnp.stack([proj[b * t + t - 3:(b + 1) * t, :nqkv] for b in range(bp)]))
            s_conv.append(proj[mp:].reshape(bs, ts, -1)[:, ts - 3:, :nqkv])
            p_delta.append(stp)
            s_delta.append(sts)
        else:
            proj = matmul_wcast(hn, b_w_in_t, j, n_out=4 * d, w_transposed=True)
            fproj = matmul_wcast(hn, b_w_gate_t, j, n_out=LANES, w_transposed=True)
            prep = functools.partial(fox_prep, proj, fproj, b_q_norm[j], b_k_norm[j],
                                     _pad_lanes(b_f_bias[j]), H=HB, HD=HD)
            qb_p, kf_p, kb_p, vf_p, vb_p, lf_p = prep(row0=0, rows=mp)
            qb_s, kf_s, kb_s, vf_s, vb_s, lf_s = prep(row0=mp, rows=ms)
            c_p = cumsum_rows(lf_p.reshape(bp, t, LANES))
            c_row = jnp.transpose(c_p[:, :, :HB], (0, 2, 1)).reshape(bp, HB, 1, t) * LOG2E
            op = fox_prompt_attention(qb_p, kb_p, vb_p, c_row, proj, nb=bp, t=t, H=HB, HD=HD)
            lf_new_hp = jnp.swapaxes(lf_s[:, :HB].reshape(bs, ts, HB), 1, 2)
            lf_new_hp = jnp.pad(lf_new_hp, ((0, 0), (0, 0), (0, page - ts)))
            cum_hp, cnew_hp = paged_cumsum(pt_flat, logf_hp, lf_new_hp, layer_off=j * n_pool,
                                           nb=bs, npages=npages, page=page, H=HB)
            cum = jnp.swapaxes(cum_hp, 2, 3).reshape(bs, npages, 1, page * HB) * LOG2E
            cnew = jnp.swapaxes(cnew_hp[:, :, :ts], 1, 2).reshape(bs, 1, ts * HB) * LOG2E
            os_ = fox_decode_attention(
                pt_flat, qb_s.reshape(bs, ts, d),
                cache_k.reshape(-1, page * HB, HD), cache_v.reshape(-1, page * HB, HD),
                kb_s.reshape(bs, ts * HB, HD), vb_s.reshape(bs, ts * HB, HD),
                cum, cnew, proj,
                layer_off=j * n_pool, nb=bs, npages=npages, page=page, H=HB, HD=HD, NQ=ts,
                gate_rb0=mp // ts)
            x = matmul_wcast(op, b_w_out, j, n_out=d, epilogue="residual", residual=x, row0=0)
            x = matmul_wcast(os_.astype(BF16), b_w_out, j, n_out=d, epilogue="residual", residual=x, row0=mp)
            p_k.append(kf_p.reshape(bp, t, HB, HD))
            p_v.append(vf_p.reshape(bp, t, HB, HD))
            p_lf.append(lf_p[:, :HB].reshape(bp, t, HB))
            s_k.append(kf_s.reshape(bs, ts, HB, HD))
            s_v.append(vf_s.reshape(bs, ts, HB, HD))
            s_lf.append(lf_s[:, :HB].reshape(bs, ts, HB))
        hm = rmsnorm_bf16(x, norm_mlp[i])
        a = matmul_wcast(hm, w_up, i, n_out=d_ff, epilogue="relu2", out_dtype=BF16)
        down = functools.partial(matmul_fullk_residual, a, w_down_bf, i, x)
        if i + 1 < depth:
            x = down(row0=0, rows=mp + ms)
    y_p, y_s = down(row0=0, rows=mp), down(row0=mp, rows=ms)
    return (y_p.reshape(bp, t, d), y_s.reshape(bs, ts, d),
            jnp.stack(p_delta), jnp.stack(p_conv), jnp.stack(p_k), jnp.stack(p_v), jnp.stack(p_lf),
            jnp.stack(s_delta), jnp.stack(s_conv), jnp.stack(s_k), jnp.stack(s_v), jnp.stack(s_lf))
```

```python
import functools

import jax
import jax.numpy as jnp
from jax import lax
from jax.experimental import pallas as pl
from jax.experimental.pallas import tpu as pltpu

F32 = jnp.float32
BF16 = jnp.bfloat16
EPS = 1e-6
NEG = -1e30
LOG2E = 1.4426950408889634
LANES = 128
SUBLANES = 8
MXU_DIM = 256
VMEM_LIMIT = 56 * 1024 * 1024

ROW_TILES = (768, 512, 528, 384, 256, 128, 64, 32, 16, 8)
WCAST_ROW_TILES = (1056,) + ROW_TILES
COL_TILE = 1024
FULLK_COL_TILE = 512
ATTN_TQ = (512, 256, 128)
ATTN_TILES_PER_STEP = (4, 2, 1)
ATTN_HEADS_PER_STEP = (4, 2, 1)
DECODE_PAGES_PER_STEP = (8, 4, 2, 1)
DECODE_KEY_CHUNK = (256, 128)
CUMSUM_PAGES_PER_STEP = (32, 16, 8, 4, 2, 1)
GDN_TOKENS_PER_STEP = 128
GDN_STACK_ROWS = 128

def _pick(n, cands):
    for c in cands:
        if n % c == 0:
            return c
    return n


def _cparams(sem):
    return pltpu.CompilerParams(dimension_semantics=sem, vmem_limit_bytes=VMEM_LIMIT)


def _dot(a, b):
    return jnp.dot(a.astype(BF16), b.astype(BF16), preferred_element_type=F32)


def _dot_nt(a, b):
    return lax.dot_general(a.astype(BF16), b.astype(BF16), (((1,), (1,)), ((), ())),
                           preferred_element_type=F32)


def _split3(x):
    hi = x.astype(BF16)
    r = x - hi.astype(F32)
    mid = r.astype(BF16)
    lo = (r - mid.astype(F32)).astype(BF16)
    return hi, mid, lo


def _dot_exact_lhs01(a01, b):
    hi, mid, lo = _split3(b)
    a = a01.astype(BF16)
    d = functools.partial(jnp.dot, preferred_element_type=F32)
    return d(a, hi) + (d(a, mid) + d(a, lo))


def _sigmoid(x):
    return 1.0 / (1.0 + jnp.exp(-x))


def _softplus(x):
    return jnp.maximum(x, 0.0) + jnp.log(1.0 + jnp.exp(-jnp.abs(x)))


def _rmsnorm_kernel(x_ref, w_ref, o_ref):
    x = x_ref[...]
    ms = jnp.mean(x * x, axis=-1, keepdims=True)
    o_ref[...] = (x * lax.rsqrt(ms + EPS) * w_ref[...]).astype(o_ref.dtype)


def rmsnorm_bf16(x, w):
    m, d = x.shape
    tm = _pick(m, ROW_TILES)
    return pl.pallas_call(
        _rmsnorm_kernel,
        grid=(m // tm,),
        in_specs=[pl.BlockSpec((tm, d), lambda i: (i, 0)),
                  pl.BlockSpec((1, d), lambda i: (0, 0))],
        out_specs=pl.BlockSpec((tm, d), lambda i: (i, 0)),
        out_shape=jax.ShapeDtypeStruct((m, d), BF16),
        compiler_params=_cparams(("parallel",)),
        name="rmsnorm",
    )(x, w.reshape(1, d))


def _epilogue(acc, kind, r_ref):
    if kind == "relu2":
        acc = jnp.maximum(acc, 0.0)
        acc = acc * acc
    elif kind == "residual":
        acc = acc + r_ref[...]
    return acc


def _mm_wcast_kernel(*refs, epilogue, w_transposed):
    if epilogue == "residual":
        a_ref, w_ref, r_ref, o_ref, wb_ref = refs
    else:
        a_ref, w_ref, o_ref, wb_ref = refs
        r_ref = None

    @pl.when(pl.program_id(1) == 0)
    def _():
        wb_ref[...] = w_ref[...].astype(wb_ref.dtype)

    if w_transposed:
        acc = _dot_nt(a_ref[...], wb_ref[...])
    else:
        acc = jnp.dot(a_ref[...], wb_ref[...], preferred_element_type=F32)
    o_ref[...] = _epilogue(acc, epilogue, r_ref).astype(o_ref.dtype)


def matmul_wcast(a, w, layer, *, n_out, w_transposed=False, epilogue="none", residual=None,
                 out_dtype=F32, row0=0):
    ma, kdim = a.shape
    tm = _pick(ma, WCAST_ROW_TILES if epilogue != "residual" else (1024, 512, 256, 128, 64, 32, 16, 8))
    tn = min(COL_TILE, n_out)
    assert n_out % tn == 0 and row0 % tm == 0 and w.shape[2 if w_transposed else 1] == kdim
    r0 = row0 // tm
    if w_transposed:
        w_spec = pl.BlockSpec((None, tn, kdim), lambda j, i: (layer, j, 0))
        wb_shape = (tn, kdim)
    else:
        w_spec = pl.BlockSpec((None, kdim, tn), lambda j, i: (layer, 0, j))
        wb_shape = (kdim, tn)
    in_specs = [pl.BlockSpec((tm, kdim), lambda j, i: (i, 0)), w_spec]
    args = [a, w]
    aliases = {}
    if epilogue == "residual":
        in_specs.append(pl.BlockSpec((tm, tn), lambda j, i: (r0 + i, j)))
        args.append(residual)
        aliases = {2: 0}
        out_rows = residual.shape[0]
    else:
        out_rows = ma
    return pl.pallas_call(
        functools.partial(_mm_wcast_kernel, epilogue=epilogue, w_transposed=w_transposed),
        grid=(n_out // tn, ma // tm),
        in_specs=in_specs,
        out_specs=pl.BlockSpec((tm, tn), lambda j, i: (r0 + i, j)),
        out_shape=jax.ShapeDtypeStruct((out_rows, n_out), out_dtype),
        scratch_shapes=[pltpu.VMEM(wb_shape, BF16)],
        input_output_aliases=aliases,
        compiler_params=_cparams(("arbitrary", "arbitrary")),
        name="matmul_wcast_" + epilogue,
    )(*args)


def _mm_fullk_kernel(a_ref, b_ref, r_ref, o_ref):
    acc = jnp.dot(a_ref[...], b_ref[...], preferred_element_type=F32)
    o_ref[...] = (acc + r_ref[...]).astype(o_ref.dtype)


def matmul_fullk_residual(a, b, layer, residual, *, row0, rows):
    _, kdim = a.shape
    n = b.shape[2]
    tm = _pick(rows, ROW_TILES)
    tn = min(FULLK_COL_TILE, n)
    assert n % tn == 0 and row0 % tm == 0
    r0 = row0 // tm
    return pl.pallas_call(
        _mm_fullk_kernel,
        grid=(rows // tm, n // tn),
        in_specs=[pl.BlockSpec((tm, kdim), lambda i, j: (r0 + i, 0)),
                  pl.BlockSpec((None, kdim, tn), lambda i, j: (layer, 0, j)),
                  pl.BlockSpec((tm, tn), lambda i, j: (r0 + i, j))],
        out_specs=pl.BlockSpec((tm, tn), lambda i, j: (i, j)),
        out_shape=jax.ShapeDtypeStruct((rows, n), F32),
        compiler_params=_cparams(("parallel", "arbitrary")),
        name="matmul_fullk_residual",
    )(a, b, residual)


def _gdn_kernel(*refs, C, G, H, DK, NS, has_cbuf, has_s0):
    it = iter(refs)
    seq_refs = [(next(it), next(it), next(it)) for _ in range(NS)]
    cw_ref = next(it)
    alog_ref = next(it)
    dtb_ref = next(it)
    onw_ref = next(it)
    cbuf_refs = [next(it) for _ in range(NS)] if has_cbuf else None
    s0_refs = [next(it) for _ in range(NS)] if has_s0 else None
    y_ref = next(it)
    sfin_ref = next(it)
    ext_ref = next(it)
    scat_ref = next(it)

    c_idx = pl.program_id(1)
    nc = pl.num_programs(1)
    R = G * C
    NG = H // G
    NQK = H * DK
    pad = SUBLANES

    @pl.when(c_idx == 0)
    def _init():
        for s in range(NS):
            ext_ref[s, 0:pad, :] = jnp.zeros((pad, ext_ref.shape[2]), F32)
            if has_cbuf:
                ext_ref[s, pad - 3:pad, :] = cbuf_refs[s][0]
            for h in range(H):
                if has_s0:
                    scat_ref[s, :, h * DK:(h + 1) * DK] = s0_refs[s][0, h]
                else:
                    scat_ref[s, :, h * DK:(h + 1) * DK] = jnp.zeros((DK, DK), F32)

    for s in range(NS):
        ext_ref[s, pad:pad + C, :] = seq_refs[s][0][...]

    def conv_act(s, col):
        sl = slice(col, col + DK)
        acc = ext_ref[s, pad:pad + C, sl] * cw_ref[3:4, sl]
        for i in range(3):
            acc = acc + ext_ref[s, pad - 3 + i:pad - 3 + i + C, sl] * cw_ref[i:i + 1, sl]
        return acc * _sigmoid(acc)

    beta_full = [_sigmoid(seq_refs[s][2][...]) for s in range(NS)]
    g_full = [-jnp.exp(alog_ref[...]) * _softplus(seq_refs[s][2][...] + dtb_ref[...]) for s in range(NS)]

    iotas = (lax.broadcasted_iota(jnp.int32, (R, R), 0), lax.broadcasted_iota(jnp.int32, (R, R), 1))

    def blk(axis, m):
        return lax.shift_right_logical(iotas[axis], m.bit_length() - 1)

    same = blk(0, C) == blk(1, C)
    eye = iotas[0] == iotas[1]
    incl = same & (iotas[1] <= iotas[0])
    strict = same & (iotas[1] < iotas[0])
    incl_t = same & (iotas[0] <= iotas[1])
    rowblk = lax.shift_right_logical(lax.broadcasted_iota(jnp.int32, (R, DK), 0), C.bit_length() - 1)
    cat = (lambda xs: xs[0]) if G == 1 else (lambda xs: jnp.concatenate(xs, axis=0))
    groups = [list(range(gi * G, (gi + 1) * G)) for gi in range(NG)]

    def group_program(s, heads):
        z_ref = seq_refs[s][1]
        qs, ks, vs = [], [], []
        for h in heads:
            q = conv_act(s, h * DK)
            k = conv_act(s, NQK + h * DK)
            qs.append(q * lax.rsqrt(jnp.sum(q * q, axis=-1, keepdims=True) + EPS) * (DK ** -0.5))
            ks.append(k * lax.rsqrt(jnp.sum(k * k, axis=-1, keepdims=True) + EPS))
            vs.append(conv_act(s, 2 * NQK + h * DK))
        Q, K, V = cat(qs), cat(ks), cat(vs)
        beta = cat([beta_full[s][:, h:h + 1] for h in heads])
        g_c = cat([g_full[s][:, H + h:H + h + 1] for h in heads])
        yield
        g_r = jnp.sum(jnp.where(eye, g_c, 0.0), axis=0, keepdims=True)
        gc = jnp.sum(jnp.where(incl, g_r, 0.0), axis=1, keepdims=True)
        gc_r = jnp.sum(jnp.where(incl_t, g_c, 0.0), axis=0, keepdims=True)
        gl = jnp.sum(jnp.where(same, g_r, 0.0), axis=1, keepdims=True)
        decay = jnp.exp(jnp.where(incl, gc - gc_r, NEG))
        KB = K * beta
        lmat = jnp.where(strict, _dot_nt(KB, K) * decay, 0.0)
        amat = jnp.where(incl, _dot_nt(Q, K) * decay, 0.0)
        yield
        ld = jnp.where(blk(0, SUBLANES) == blk(1, SUBLANES), lmat, 0.0)
        ld2 = _dot(ld, ld)
        yield
        ld4 = _dot(ld2, ld2)
        x = jnp.where(eye, 1.0, 0.0) - ld
        x = x + _dot(x, ld2)
        yield
        x = x + _dot(x, ld4)
        yield
        m = SUBLANES
        while m < C:
            sel = (blk(0, 2 * m) == blk(1, 2 * m)) & (blk(0, m) != blk(1, m))
            ox = _dot(jnp.where(sel, lmat, 0.0), x)
            yield
            x = x - _dot(x, ox)
            yield
            m *= 2
        eg = jnp.exp(gc)
        UW = _dot(x, jnp.concatenate([V * beta, KB * eg], axis=1))
        yield
        sc = scat_ref[s, :, heads[0] * DK:(heads[-1] + 1) * DK]
        WQS = _dot(jnp.concatenate([UW[:, DK:], Q * eg], axis=0), sc)
        yield
        vn, oi = [], []
        for j in range(G):
            rs = slice(j * C, (j + 1) * C)
            cs = slice(j * DK, (j + 1) * DK)
            vn.append(UW[rs, :DK] - WQS[rs, cs])
            oi.append(WQS[R + j * C:R + (j + 1) * C, cs])
        Vn = cat(vn)
        O = cat(oi) + _dot(amat, Vn)
        ke = K * jnp.exp(gl - gc)
        vwide = jnp.concatenate([jnp.where(rowblk == j, Vn, 0.0) for j in range(G)], axis=1)
        dS = _dot(ke.T, vwide)
        yield
        for j, h in enumerate(heads):
            cs = slice(j * DK, (j + 1) * DK)
            e_last = jnp.exp(gl[j * C:j * C + 1, :])
            scat_ref[s, :, h * DK:(h + 1) * DK] = sc[:, cs] * e_last + dS[:, cs]
            o = O[j * C:(j + 1) * C]
            o = o * lax.rsqrt(jnp.mean(o * o, axis=-1, keepdims=True) + EPS) * onw_ref[...]
            zg = z_ref[:, h * DK:(h + 1) * DK]
            y_ref[s, :, h * DK:(h + 1) * DK] = (o * (zg * _sigmoid(zg))).astype(y_ref.dtype)

    programs = [group_program(s, heads) for s in range(NS) for heads in groups]
    while programs:
        running = []
        for prog in programs:
            try:
                next(prog)
                running.append(prog)
            except StopIteration:
                pass
        programs = running

    for s in range(NS):
        carry = ext_ref[s, C:C + pad, :]
        ext_ref[s, 0:pad, :] = carry

    @pl.when(c_idx == nc - 1)
    def _fin():
        for s in range(NS):
            for h in range(H):
                sfin_ref[s, h] = scat_ref[s, :, h * DK:(h + 1) * DK]


def gdn_core(proj, gates, conv_w, alog_pad, dtb_pad, out_norm, conv_buf, s0, *,
             row0, nb, t, C, G, H, DK, out_dtype):
    nqkv = 3 * H * DK
    nc = t // C
    assert t % C == 0 and row0 % C == 0 and H % G == 0 and C & (C - 1) == 0
    NS = _pick(nb, tuple(n for n in (4, 2, 1) if n * C <= GDN_TOKENS_PER_STEP))
    rb0 = row0 // C
    has_cbuf = conv_buf is not None
    has_s0 = s0 is not None

    def rows(s, col):
        return lambda bb, c: (rb0 + (bb * NS + s) * nc + c, col)

    const = lambda bb, c: (0, 0)
    in_specs, args = [], []
    for s in range(NS):
        in_specs += [pl.BlockSpec((C, nqkv), rows(s, 0)),
                     pl.BlockSpec((C, H * DK), rows(s, 3)),
                     pl.BlockSpec((C, LANES), rows(s, 0))]
        args += [proj, proj, gates]
    in_specs += [pl.BlockSpec((4, nqkv), const), pl.BlockSpec((1, LANES), const),
                 pl.BlockSpec((1, LANES), const), pl.BlockSpec((1, DK), const)]
    args += [conv_w, alog_pad, dtb_pad, out_norm.reshape(1, DK)]
    if has_cbuf:
        for s in range(NS):
            in_specs.append(pl.BlockSpec((1, 3, nqkv), lambda bb, c, s=s: (bb * NS + s, 0, 0)))
            args.append(conv_buf)
    if has_s0:
        for s in range(NS):
            in_specs.append(pl.BlockSpec((1, H, DK, DK), lambda bb, c, s=s: (bb * NS + s, 0, 0, 0)))
            args.append(s0)
    y, sfin = pl.pallas_call(
        functools.partial(_gdn_kernel, C=C, G=G, H=H, DK=DK, NS=NS, has_cbuf=has_cbuf, has_s0=has_s0),
        grid=(nb // NS, nc),
        in_specs=in_specs,
        out_specs=[pl.BlockSpec((NS, C, H * DK), lambda bb, c: (bb, c, 0)),
                   pl.BlockSpec((NS, H, DK, DK), lambda bb, c: (bb, 0, 0, 0))],
        out_shape=[jax.ShapeDtypeStruct((nb, t, H * DK), out_dtype),
                   jax.ShapeDtypeStruct((nb, H, DK, DK), F32)],
        scratch_shapes=[pltpu.VMEM((NS, C + SUBLANES, nqkv), F32),
                        pltpu.VMEM((NS, DK, H * DK), F32)],
        compiler_params=_cparams(("parallel", "arbitrary")),
        name="gdn_core_c%d" % C,
    )(*args)
    return y.reshape(nb * t, H * DK), sfin


def _fox_prep_kernel(q_ref, k_ref, v_ref, f_ref, qn_ref, kn_ref, fb_ref,
                     qo_ref, ko_ref, kb_ref, vo_ref, vb_ref, lf_ref, *, H, HD):
    q_scale = (HD ** -0.5) * LOG2E
    for h in range(H):
        sl = slice(h * HD, (h + 1) * HD)
        q = q_ref[:, sl]
        q = q * lax.rsqrt(jnp.mean(q * q, axis=-1, keepdims=True) + EPS) * qn_ref[...]
        qo_ref[:, sl] = (q * q_scale).astype(qo_ref.dtype)
        k = k_ref[:, sl]
        k = k * lax.rsqrt(jnp.mean(k * k, axis=-1, keepdims=True) + EPS) * kn_ref[...]
        ko_ref[:, sl] = k
        kb_ref[:, sl] = k.astype(kb_ref.dtype)
    v = v_ref[...]
    vo_ref[...] = v
    vb_ref[...] = v.astype(vb_ref.dtype)
    x = f_ref[...] + fb_ref[...]
    lf_ref[...] = jnp.minimum(x, 0.0) - jnp.log1p(jnp.exp(-jnp.abs(x)))


def fox_prep(proj, fproj, q_norm, k_norm, fb_pad, *, row0, rows, H, HD):
    d = H * HD
    tm = _pick(rows, (256, 128, 64, 32, 16, 8))
    assert row0 % tm == 0
    r0 = row0 // tm
    col = lambda j: (lambda i: (r0 + i, j))
    out = lambda i: (i, 0)
    return pl.pallas_call(
        functools.partial(_fox_prep_kernel, H=H, HD=HD),
        grid=(rows // tm,),
        in_specs=[pl.BlockSpec((tm, d), col(0)), pl.BlockSpec((tm, d), col(1)),
                  pl.BlockSpec((tm, d), col(2)), pl.BlockSpec((tm, LANES), col(0)),
                  pl.BlockSpec((1, HD), lambda i: (0, 0)), pl.BlockSpec((1, HD), lambda i: (0, 0)),
                  pl.BlockSpec((1, LANES), lambda i: (0, 0))],
        out_specs=[pl.BlockSpec((tm, d), out)] * 5 + [pl.BlockSpec((tm, LANES), out)],
        out_shape=[jax.ShapeDtypeStruct((rows, d), BF16), jax.ShapeDtypeStruct((rows, d), F32),
                   jax.ShapeDtypeStruct((rows, d), BF16), jax.ShapeDtypeStruct((rows, d), F32),
                   jax.ShapeDtypeStruct((rows, d), BF16), jax.ShapeDtypeStruct((rows, LANES), F32)],
        compiler_params=_cparams(("parallel",)),
        name="fox_prep",
    )(proj, proj, proj, fproj, q_norm.reshape(1, HD), k_norm.reshape(1, HD), fb_pad)


def _cumsum_block(x, carry):
    n = x.shape[0]
    ri = lax.broadcasted_iota(jnp.int32, (n, n), 0)
    ci = lax.broadcasted_iota(jnp.int32, (n, n), 1)
    tril = jnp.where(ci <= ri, 1.0, 0.0)
    return _dot_exact_lhs01(tril, x) + carry


def _cumsum_kernel(x_ref, o_ref, carry_ref):
    @pl.when(pl.program_id(1) == 0)
    def _():
        carry_ref[...] = jnp.zeros_like(carry_ref)

    cs = _cumsum_block(x_ref[0], carry_ref[...])
    o_ref[0] = cs
    carry_ref[...] = cs[cs.shape[0] - 1:, :]


def cumsum_rows(x):
    b, t, w = x.shape
    tb = _pick(t, (256, 128, 64, 32, 16, 8))
    return pl.pallas_call(
        _cumsum_kernel,
        grid=(b, t // tb),
        in_specs=[pl.BlockSpec((1, tb, w), lambda i, j: (i, j, 0))],
        out_specs=pl.BlockSpec((1, tb, w), lambda i, j: (i, j, 0)),
        out_shape=jax.ShapeDtypeStruct((b, t, w), F32),
        scratch_shapes=[pltpu.VMEM((1, w), F32)],
        compiler_params=_cparams(("parallel", "arbitrary")),
        name="cumsum_rows",
    )(x)


def _cumsum_lanes(x, carry):
    n = x.shape[1]
    ri = lax.broadcasted_iota(jnp.int32, (n, n), 0)
    ci = lax.broadcasted_iota(jnp.int32, (n, n), 1)
    triu = jnp.where(ri <= ci, 1.0, 0.0).astype(BF16)
    hi, mid, lo = _split3(x)
    d = functools.partial(jnp.dot, preferred_element_type=F32)
    return d(hi, triu) + (d(mid, triu) + d(lo, triu)) + carry


def _paged_cumsum_kernel(pt_ref, *refs, pps):
    lp_refs = refs[:pps]
    ln_ref, o_ref, on_ref, carry_ref = refs[pps:]
    p = pl.program_id(1)

    @pl.when(p == 0)
    def _():
        carry_ref[...] = jnp.zeros_like(carry_ref)

    h = lp_refs[0].shape[1]
    local = _cumsum_lanes(jnp.concatenate([r[0] for r in lp_refs], axis=0) if pps > 1 else lp_refs[0][0], 0.0)
    carry = carry_ref[...]
    for j in range(pps):
        lj = local[j * h:(j + 1) * h]
        o_ref[0, j] = lj + carry
        carry = carry + lj[:, lj.shape[1] - 1:]
    carry_ref[...] = carry

    @pl.when(p == pl.num_programs(1) - 1)
    def _():
        on_ref[0] = _cumsum_lanes(ln_ref[0], carry)


def paged_cumsum(page_table_flat, logf_hp, lf_new_hp, *, layer_off, nb, npages, page, H):
    pps = _pick(npages, CUMSUM_PAGES_PER_STEP)
    ts = lf_new_hp.shape[2]
    assert ts == page

    def pool_map(j):
        return lambda b, p, pt: (layer_off + pt[b * npages + p * pps + j], 0, 0)

    return pl.pallas_call(
        functools.partial(_paged_cumsum_kernel, pps=pps),
        grid_spec=pltpu.PrefetchScalarGridSpec(
            num_scalar_prefetch=1,
            grid=(nb, npages // pps),
            in_specs=[pl.BlockSpec((1, H, page), pool_map(j)) for j in range(pps)]
            + [pl.BlockSpec((1, H, ts), lambda b, p, pt: (b, 0, 0))],
            out_specs=[pl.BlockSpec((1, pps, H, page), lambda b, p, pt: (b, p, 0, 0)),
                       pl.BlockSpec((1, H, ts), lambda b, p, pt: (b, 0, 0))],
            scratch_shapes=[pltpu.VMEM((H, 1), F32)],
        ),
        out_shape=[jax.ShapeDtypeStruct((nb, npages, H, page), F32),
                   jax.ShapeDtypeStruct((nb, H, ts), F32)],
        compiler_params=_cparams(("parallel", "arbitrary")),
        name="paged_cumsum",
    )(page_table_flat, *([logf_hp] * pps), lf_new_hp)


def _fox_prompt_kernel(q_ref, k_ref, v_ref, c_ref, *refs, tq, nq, tiles_per_step, hd):
    g_refs = refs[:tiles_per_step]
    o_ref, vx_ref = refs[tiles_per_step:]
    step = pl.program_id(2)
    steps = nq // tiles_per_step
    heads = k_ref.shape[1] // hd

    @pl.when(step == 0)
    def _():
        for hh in range(heads):
            vx_ref[hh, :, :hd] = v_ref[:, hh * hd:(hh + 1) * hd].astype(vx_ref.dtype)
            vx_ref[hh, :, hd:] = jnp.ones((vx_ref.shape[1], hd), vx_ref.dtype)

    def tile(i, slot, hh):
        cs = slice(hh * hd, (hh + 1) * hd)
        kext = (i + 1) * tq
        s = _dot_nt(q_ref[0, slot, :, cs], k_ref[0:kext, cs]) - c_ref[0, hh, :, 0:kext]
        rpos = lax.broadcasted_iota(jnp.int32, (tq, tq), 0)
        cpos = lax.broadcasted_iota(jnp.int32, (tq, tq), 1)
        diag = jnp.where(cpos <= rpos, s[:, kext - tq:], NEG)
        s = diag if i == 0 else jnp.concatenate([s[:, :kext - tq], diag], axis=1)
        m = jnp.max(s, axis=1, keepdims=True)
        pv = _dot(jnp.exp2(s - m), vx_ref[hh, 0:kext, :])
        gate = g_refs[slot][:, cs]
        o_ref[0, slot, :, cs] = ((pv[:, :hd] / pv[:, hd:]) * _sigmoid(gate)).astype(o_ref.dtype)

    def tiles_of_step(p):
        for slot in range(tiles_per_step):
            for hh in range(heads):
                tile(slot * steps + p, slot, hh)

    for p in range(steps):
        pl.when(step == p)(functools.partial(tiles_of_step, p))


def fox_prompt_attention(qb, kb, vb, c_row, proj, *, nb, t, H, HD):
    tq = _pick(t, ATTN_TQ)
    nq = t // tq
    tps = _pick(nq, ATTN_TILES_PER_STEP)
    steps = nq // tps
    hp = _pick(H, ATTN_HEADS_PER_STEP)
    w = hp * HD
    d = H * HD
    assert (3 * H) % hp == 0
    gate_specs = [pl.BlockSpec((tq, w), lambda b, g, p, slot=slot: (b * nq + slot * steps + p, 3 * H // hp + g))
                  for slot in range(tps)]
    out = pl.pallas_call(
        functools.partial(_fox_prompt_kernel, tq=tq, nq=nq, tiles_per_step=tps, hd=HD),
        grid=(nb, H // hp, steps),
        in_specs=[pl.BlockSpec((1, tps, tq, w), lambda b, g, p: (b, 0, p, g)),
                  pl.BlockSpec((t, w), lambda b, g, p: (b, g)),
                  pl.BlockSpec((t, w), lambda b, g, p: (b, g)),
                  pl.BlockSpec((1, hp, 1, t), lambda b, g, p: (b, g, 0, 0))] + gate_specs,
        out_specs=pl.BlockSpec((1, tps, tq, w), lambda b, g, p: (b, 0, p, g)),
        out_shape=jax.ShapeDtypeStruct((nb, tps, t // tps, d), BF16),
        scratch_shapes=[pltpu.VMEM((hp, t, 2 * HD), BF16)],
        compiler_params=_cparams(("parallel", "parallel", "arbitrary")),
        name="fox_prompt_attn",
    )(qb.reshape(nb, tps, t // tps, d), kb, vb, c_row, *([proj] * tps))
    return out.reshape(nb * t, d)


def _fox_decode_kernel(pt_ref, *refs, pps, H, HD, NQ):
    q_ref = refs[0]
    kp_refs = refs[1:1 + pps]
    vp_refs = refs[1 + pps:1 + 2 * pps]
    (kn_ref, vn_ref, cum_ref, cn_ref, g_ref,
     o_ref, qcat_ref, mask_ref, m_ref, acc_ref, s_ref, vx_ref) = refs[1 + 2 * pps:]
    p = pl.program_id(1)
    rows = H * NQ
    kc = mask_ref.shape[1]
    width = kp_refs[0].shape[1]

    @pl.when(p == 0)
    def _():
        q = q_ref[0].astype(F32)
        for h in range(H):
            qcat_ref[h * NQ:(h + 1) * NQ, :] = q[:, h * HD:(h + 1) * HD]
        rh = lax.broadcasted_iota(jnp.int32, (rows, kc), 0) // NQ
        ch = lax.broadcasted_iota(jnp.int32, (rows, kc), 1) % H
        mask_ref[...] = jnp.where(rh == ch, 0.0, NEG)
        m_ref[...] = jnp.full_like(m_ref, NEG)
        acc_ref[...] = jnp.zeros_like(acc_ref)
        vx_ref[:, :, HD:] = jnp.ones((vx_ref.shape[0], width, HD), vx_ref.dtype)

    qc = qcat_ref[...].astype(BF16)

    chunks = range(0, width, kc)
    slots = vx_ref.shape[0]

    def scores(j, c0, mx):
        s = _dot_nt(qc, kp_refs[j][0, c0:c0 + kc, :]) + (mask_ref[...] - cum_ref[0, j, :, c0:c0 + kc])
        s_ref[:, (j % slots) * width + c0:(j % slots) * width + c0 + kc] = s
        return jnp.maximum(mx, s)

    def weighted(j, c0, m_new, pv):
        pr = jnp.exp2(s_ref[:, (j % slots) * width + c0:(j % slots) * width + c0 + kc] - m_new)
        return pv + _dot(pr, vx_ref[j % slots, c0:c0 + kc, :])

    def stage_values(j):
        vx_ref[j % slots, :, :HD] = vp_refs[j][0].astype(vx_ref.dtype)

    m, acc = m_ref[...], acc_ref[...]
    neg = jnp.full((rows, kc), NEG, F32)
    stage_values(0)
    mx = neg
    for c0 in chunks:
        mx = scores(0, c0, mx)
    for j in range(pps):
        m_new = jnp.maximum(m, jnp.max(mx, axis=1, keepdims=True))
        pv = jnp.zeros((rows, 2 * HD), F32)
        nxt = j + 1 < pps
        if nxt:
            stage_values(j + 1)
        mx = neg
        for c0 in chunks:
            pv = weighted(j, c0, m_new, pv)
            if nxt:
                mx = scores(j + 1, c0, mx)
        acc = jnp.exp2(m - m_new) * acc + pv
        m = m_new
    m_ref[...], acc_ref[...] = m, acc

    @pl.when(p == pl.num_programs(1) - 1)
    def _():
        wn = kn_ref.shape[1]
        r = lax.broadcasted_iota(jnp.int32, (rows, wn), 0)
        c = lax.broadcasted_iota(jnp.int32, (rows, wn), 1)
        ok = (r // NQ == c % H) & (c // H <= r % NQ)
        s = jnp.where(ok, _dot_nt(qc, kn_ref[0]) - cn_ref[0], NEG)
        m_fin = jnp.maximum(m, jnp.max(s, axis=1, keepdims=True))
        vn = vn_ref[0]
        fin = (jnp.exp2(m - m_fin) * acc
               + _dot(jnp.exp2(s - m_fin), jnp.concatenate([vn, jnp.ones_like(vn)], axis=1)))
        out = fin[:, :HD] / fin[:, HD:]
        for h in range(H):
            cs = slice(h * HD, (h + 1) * HD)
            o_ref[:, cs] = (out[h * NQ:(h + 1) * NQ, :] * _sigmoid(g_ref[:, cs])).astype(o_ref.dtype)


def fox_decode_attention(page_table_flat, q_s, cache_k, cache_v, kn, vn, cum, cnew, gate_src, *,
                         layer_off, nb, npages, page, H, HD, NQ, gate_rb0):
    d = H * HD
    pps = _pick(npages, DECODE_PAGES_PER_STEP)
    rows = H * NQ
    kc = _pick(page * H, DECODE_KEY_CHUNK)
    slots = min(2, pps)
    assert kc % H == 0 and (page * H) % kc == 0

    def pool_map(j):
        return lambda b, p, pt: (layer_off + pt[b * npages + p * pps + j], 0, 0)

    per_b = lambda b, p, pt: (b, 0, 0)
    pool_specs = [pl.BlockSpec((1, page * H, HD), pool_map(j)) for j in range(pps)]
    return pl.pallas_call(
        functools.partial(_fox_decode_kernel, pps=pps, H=H, HD=HD, NQ=NQ),
        grid_spec=pltpu.PrefetchScalarGridSpec(
            num_scalar_prefetch=1,
            grid=(nb, npages // pps),
            in_specs=[pl.BlockSpec((1, NQ, d), per_b)] + pool_specs + pool_specs
            + [pl.BlockSpec((1, NQ * H, HD), per_b),
               pl.BlockSpec((1, NQ * H, HD), per_b),
               pl.BlockSpec((1, pps, 1, page * H), lambda b, p, pt: (b, p, 0, 0)),
               pl.BlockSpec((1, 1, NQ * H), per_b),
               pl.BlockSpec((NQ, d), lambda b, p, pt: (gate_rb0 + b, 3))],
            out_specs=pl.BlockSpec((NQ, d), lambda b, p, pt: (b, 0)),
            scratch_shapes=[pltpu.VMEM((rows, HD), F32),
                            pltpu.VMEM((rows, kc), F32),
                            pltpu.VMEM((rows, 1), F32),
                            pltpu.VMEM((rows, 2 * HD), F32),
                            pltpu.VMEM((rows, slots * page * H), F32),
                            pltpu.VMEM((slots, page * H, 2 * HD), BF16)],
        ),
        out_shape=jax.ShapeDtypeStruct((nb * NQ, d), F32),
        compiler_params=_cparams(("parallel", "arbitrary")),
        name="fox_decode_attn",
    )(page_table_flat, q_s, *([cache_k] * pps), *([cache_v] * pps), kn, vn, cum, cnew, gate_src)


def _pad_lanes(v, start=0):
    out = jnp.zeros((1, LANES), F32)
    return lax.dynamic_update_slice(out, v.reshape(1, -1).astype(F32), (0, start))


def kernel(x_prompt, x_sample, state_delta, state_conv, cache_k, cache_v, cache_logf, page_table,
           norm_mix, norm_mlp, a_w_in, a_conv, a_A_log, a_dt_bias, a_out_norm, a_w_out,
           b_w_in, b_f_bias, b_q_norm, b_k_norm, b_w_out, w_up, w_down):
    bp, t, d = x_prompt.shape
    bs, ts, _ = x_sample.shape
    depth = norm_mix.shape[0]
    n_mixers = 2
    H = state_delta.shape[2]
    DK = state_delta.shape[3]
    nqkv = state_conv.shape[-1]
    HB = cache_k.shape[3]
    HD = cache_k.shape[4]
    n_pool, page = cache_k.shape[1], cache_k.shape[2]
    npages = page_table.shape[1]
    d_ff = w_up.shape[2]
    mp, ms = bp * t, bs * ts
    assert H * DK == d and HB * HD == d and nqkv == 3 * d and DK == LANES and HD == LANES
    assert 2 * H <= LANES and HB <= LANES and 3 <= ts <= page and t >= 3

    x = jnp.concatenate([x_prompt.reshape(mp, d), x_sample.reshape(ms, d)], axis=0)
    pt_flat = page_table.reshape(-1).astype(jnp.int32)
    a_w_in_t = jnp.swapaxes(a_w_in, 1, 2)
    b_w_in_t = jnp.swapaxes(b_w_in, 1, 2)
    pad_rows = lambda w: jnp.pad(w, ((0, 0), (0, LANES - w.shape[1]), (0, 0)))
    a_w_gate_t = pad_rows(a_w_in_t[:, nqkv + d:, :])
    b_w_gate_t = pad_rows(b_w_in_t[:, 4 * d:, :])
    w_down_bf = w_down.astype(BF16)
    logf_hp = jnp.swapaxes(cache_logf, 2, 3).reshape(-1, HB, page)

    p_delta, p_conv, s_delta, s_conv = [], [], [], []
    p_k, p_v, p_lf, s_k, s_v, s_lf = [], [], [], [], [], []
    for i in range(depth):
        j = i // n_mixers
        hn = rmsnorm_bf16(x, norm_mix[i])
        if i % n_mixers == 0:
            proj = matmul_wcast(hn, a_w_in_t, j, n_out=nqkv + d, w_transposed=True)
            gates = matmul_wcast(hn, a_w_gate_t, j, n_out=LANES, w_transposed=True)
            alog_pad = _pad_lanes(a_A_log[j], H)
            dtb_pad = _pad_lanes(a_dt_bias[j], H)
            common = dict(H=H, DK=DK)
            cp = 64 if t % 64 == 0 else t
            yp, stp = gdn_core(proj, gates, a_conv[j], alog_pad, dtb_pad, a_out_norm[j], None, None,
                               row0=0, nb=bp, t=t, C=cp, G=max(1, min(H, GDN_STACK_ROWS // cp)),
                               out_dtype=BF16, **common)
            ys, sts = gdn_core(proj, gates, a_conv[j], alog_pad, dtb_pad, a_out_norm[j],
                               state_conv[j], state_delta[j],
                               row0=mp, nb=bs, t=ts, C=ts, G=H, out_dtype=F32, **common)
            x = matmul_wcast(yp, a_w_out, j, n_out=d, epilogue="residual", residual=x, row0=0)
            x = matmul_wcast(ys.astype(BF16), a_w_out, j, n_out=d, epilogue="residual", residual=x, row0=mp)
            p_conv.append(jnp.stack([proj[b * t + t - 3:(b + 1) * t, :nqkv] for b in range(bp)]))
            s_conv.append(proj[mp:].reshape(bs, ts, -1)[:, ts - 3:, :nqkv])
            p_delta.append(stp)
            s_delta.append(sts)
        else:
            proj = matmul_wcast(hn, b_w_in_t, j, n_out=4 * d, w_transposed=True)
            fproj = matmul_wcast(hn, b_w_gate_t, j, n_out=LANES, w_transposed=True)
            prep = functools.partial(fox_prep, proj, fproj, b_q_norm[j], b_k_norm[j],
                                     _pad_lanes(b_f_bias[j]), H=HB, HD=HD)
            qb_p, kf_p, kb_p, vf_p, vb_p, lf_p = prep(row0=0, rows=mp)
            qb_s, kf_s, kb_s, vf_s, vb_s, lf_s = prep(row0=mp, rows=ms)
            c_p = cumsum_rows(lf_p.reshape(bp, t, LANES))
            c_row = jnp.transpose(c_p[:, :, :HB], (0, 2, 1)).reshape(bp, HB, 1, t) * LOG2E
            op = fox_prompt_attention(qb_p, kb_p, vb_p, c_row, proj, nb=bp, t=t, H=HB, HD=HD)
            lf_new_hp = jnp.swapaxes(lf_s[:, :HB].reshape(bs, ts, HB), 1, 2)
            lf_new_hp = jnp.pad(lf_new_hp, ((0, 0), (0, 0), (0, page - ts)))
            cum_hp, cnew_hp = paged_cumsum(pt_flat, logf_hp, lf_new_hp, layer_off=j * n_pool,
                                           nb=bs, npages=npages, page=page, H=HB)
            cum = jnp.swapaxes(cum_hp, 2, 3).reshape(bs, npages, 1, page * HB) * LOG2E
            cnew = jnp.swapaxes(cnew_hp[:, :, :ts], 1, 2).reshape(bs, 1, ts * HB) * LOG2E
            os_ = fox_decode_attention(
                pt_flat, qb_s.reshape(bs, ts, d),
                cache_k.reshape(-1, page * HB, HD), cache_v.reshape(-1, page * HB, HD),
                kb_s.reshape(bs, ts * HB, HD), vb_s.reshape(bs, ts * HB, HD),
                cum, cnew, proj,
                layer_off=j * n_pool, nb=bs, npages=npages, page=page, H=HB, HD=HD, NQ=ts,
                gate_rb0=mp // ts)
            x = matmul_wcast(op, b_w_out, j, n_out=d, epilogue="residual", residual=x, row0=0)
            x = matmul_wcast(os_.astype(BF16), b_w_out, j, n_out=d, epilogue="residual", residual=x, row0=mp)
            p_k.append(kf_p.reshape(bp, t, HB, HD))
            p_v.append(vf_p.reshape(bp, t, HB, HD))
            p_lf.append(lf_p[:, :HB].reshape(bp, t, HB))
            s_k.append(kf_s.reshape(bs, ts, HB, HD))
            s_v.append(vf_s.reshape(bs, ts, HB, HD))
            s_lf.append(lf_s[:, :HB].reshape(bs, ts, HB))
        hm = rmsnorm_bf16(x, norm_mlp[i])
        a = matmul_wcast(hm, w_up, i, n_out=d_ff, epilogue="relu2", out_dtype=BF16)
        down = functools.partial(matmul_fullk_residual, a, w_down_bf, i, x)
        if i + 1 < depth:
            x = down(row0=0, rows=mp + ms)
    y_p, y_s = down(row0=0, rows=mp), down(row0=mp, rows=ms)
    return (y_p.reshape(bp, t, d), y_s.reshape(bs, ts, d),
            jnp.stack(p_delta), jnp.stack(p_conv), jnp.stack(p_k), jnp.stack(p_v), jnp.stack(p_lf),
            jnp.stack(s_delta), jnp.stack(s_conv), jnp.stack(s_k), jnp.stack(s_v), jnp.stack(s_lf))
```
